```python
import math
import jax
import jax.numpy as jnp
from jax import lax
import numpy as np

D_MODEL = 4096
BATCH = 32
SEQ = 256
DEPTH = 4
DEC_BATCH = 8
DEC_SEQ = 2048
PAST_LEN = 512

GRID_W = 64
ROPE_BASE = 10000.0
EPS = 1e-6
NEG_INF = -1e30
Q_BLOCK = 128

H_A = 8
D_A = 64
H_B = 4
DK_B = 256
DV_B = 256
CHUNK_B = 128
H_C = 4
DK_C = 128
DV_C = 256
GATE_RANK_C = 16
GATE_TAU_C = 16.0
CHUNK_C = 64
H_D = 8
HKV_D = 4
DH_D = 128
WINDOW_D = 128
N_BRANCH = 4
BRANCH_W = 1024
N_GROUPS = 4
EXPERTS_PER_GROUP = 4
N_EXPERTS = N_GROUPS * EXPERTS_PER_GROUP
TOP_K_INNER = 2
D_EXPERT = 1024

IN_SPLITS = (
    ('a_q', H_A * 2 * D_A), ('a_k', H_A * 2 * D_A), ('a_v', H_A * 2 * D_A),
    ('b_q', H_B * DK_B), ('b_k', H_B * DK_B), ('b_v', H_B * DV_B), ('b_o', H_B * DV_B), ('b_if', 2 * 2 * H_B),
    ('c_q', H_C * DK_C), ('c_k', H_C * DK_C), ('c_v', H_C * DV_C), ('c_r', H_C * DV_C), ('c_glr', 2 * GATE_RANK_C),
    ('d_q', H_D * DH_D), ('d_k', HKV_D * DH_D), ('d_v', HKV_D * DH_D),
)
D_IN = (3 * H_A * 2 * D_A + 2 * H_B * DK_B + 2 * H_B * DV_B + 4 * H_B
        + 2 * H_C * DK_C + 2 * H_C * DV_C + 2 * GATE_RANK_C + H_D * DH_D + 2 * HKV_D * DH_D)

F32 = jnp.float32

kernel_name = 'hybrid_diffusion_prefix_step'


def _rms(x, g):
    xf = x.astype(F32)
    y = xf * lax.rsqrt(jnp.mean(xf * xf, axis=-1, keepdims=True) + EPS)
    return (y * g.astype(F32)).astype(x.dtype)


def _modulation(cond, w, b):
    m = jax.nn.silu(cond) @ w + b
    return jnp.split(m, 6, axis=-1)


def _modulate(x, g, shift, scale):
    return _rms(x, g) * (1.0 + scale) + shift


def _project(u, w_in):
    proj = u @ w_in
    parts = {}
    off = 0
    for name, width in IN_SPLITS:
        parts[name] = proj[..., off:off + width]
        off += width
    return parts


def _flip(t):
    return jnp.flip(t, axis=1)


def _axial_rope_tables(rows, dim):
    row = jnp.repeat(jnp.arange(rows, dtype=F32), GRID_W)
    col = jnp.tile(jnp.arange(GRID_W, dtype=F32), rows)
    n_freq = dim // 4
    inv = ROPE_BASE ** (-jnp.arange(n_freq, dtype=F32) / n_freq)
    ang = jnp.concatenate([row[:, None] * inv[None], col[:, None] * inv[None]], axis=-1)
    return jnp.cos(ang), jnp.sin(ang)


def _rope(x, cos, sin):
    half = x.shape[-1] // 2
    shp = (1, x.shape[1]) + (1,) * (x.ndim - 3) + (half,)
    c = cos.reshape(shp)
    s = sin.reshape(shp)
    xf = x.astype(F32)
    x1, x2 = xf[..., :half], xf[..., half:]
    return jnp.concatenate([x1 * c - x2 * s, x1 * s + x2 * c], axis=-1).astype(x.dtype)


def _softmax_with_sink(s, sink):
    sk = jnp.broadcast_to(sink.astype(F32)[None, :, :, None, None], s.shape[:-1] + (1,))
    return jax.nn.softmax(jnp.concatenate([s, sk], axis=-1), axis=-1)[..., :-1]


def _diff_lambda(w_lam, lam_init):
    lf = w_lam.astype(F32)
    return jnp.exp(jnp.sum(lf[0] * lf[1])) - jnp.exp(jnp.sum(lf[2] * lf[3])) + lam_init


def _diff_attention(q, k, v, lam):
    B, Nq = q.shape[:2]
    nb = Nq // Q_BLOCK
    qb = jnp.moveaxis(q.reshape((B, nb, Q_BLOCK) + q.shape[2:]), 1, 0)
    scale = D_A ** -0.5

    def block(qi):
        s = jnp.einsum('bqhcd,bkhcd->bhcqk', qi, k).astype(F32) * scale
        p = jax.nn.softmax(s, axis=-1)
        w = p[:, :, 0] - lam * p[:, :, 1]
        return jnp.einsum('bhqk,bkhe->bqhe', w.astype(v.dtype), v)

    o = lax.map(block, qb)
    return jnp.moveaxis(o, 0, 1).reshape(B, Nq, H_A, 2 * D_A)


def _diff_out(o, g, lam_init):
    B, N = o.shape[:2]
    return (_rms(o, g) * (1.0 - lam_init)).reshape(B, N, H_A * 2 * D_A)


def _gqa_dense(q, k, v, sink):
    B, Nq = q.shape[:2]
    G = H_D // HKV_D
    nb = Nq // Q_BLOCK
    qb = jnp.moveaxis(q.reshape(B, nb, Q_BLOCK, HKV_D, G, DH_D), 1, 0)
    sink_g = sink.reshape(HKV_D, G)
    scale = DH_D ** -0.5

    def block(qi):
        s = jnp.einsum('bqhgd,bkhd->bhgqk', qi, k).astype(F32) * scale
        p = _softmax_with_sink(s, sink_g).astype(v.dtype)
        return jnp.einsum('bhgqk,bkhd->bqhgd', p, v)

    o = lax.map(block, qb)
    return jnp.moveaxis(o, 0, 1).reshape(B, Nq, H_D * DH_D)


def _gqa_banded(q, k, v, kc, vc, sink):
    B, N = q.shape[:2]
    W = WINDOW_D
    G = H_D // HKV_D
    nb = N // W
    pad = ((0, 0), (W, W), (0, 0), (0, 0))

    def band(t):
        tp = jnp.pad(t, pad).reshape(B, nb + 2, W, HKV_D, DH_D)
        return jnp.concatenate([tp[:, :-2], tp[:, 1:-1], tp[:, 2:]], axis=2)

    kb = jnp.moveaxis(band(k), 1, 0)
    vb = jnp.moveaxis(band(v), 1, 0)
    blk = jnp.arange(nb)[:, None]
    qpos = blk * W + jnp.arange(W)[None, :]
    kpos = (blk - 1) * W + jnp.arange(3 * W)[None, :]
    mask = ((jnp.abs(qpos[:, :, None] - kpos[:, None, :]) <= W)
            & (kpos[:, None, :] >= 0) & (kpos[:, None, :] < N))
    qb = jnp.moveaxis(q.reshape(B, nb, W, HKV_D, G, DH_D), 1, 0)
    sink_g = sink.reshape(HKV_D, G)
    scale = DH_D ** -0.5
    n_loc = 3 * W

    def block(args):
        qi, ki, vi, mi = args
        s_loc = jnp.einsum('bqhgd,bkhd->bhgqk', qi, ki).astype(F32) * scale
        s_loc = jnp.where(mi[None, None, None], s_loc, NEG_INF)
        s_ctx = jnp.einsum('bqhgd,bkhd->bhgqk', qi, kc).astype(F32) * scale
        p = _softmax_with_sink(jnp.concatenate([s_loc, s_ctx], axis=-1), sink_g).astype(vi.dtype)
        return (jnp.einsum('bhgqk,bkhd->bqhgd', p[..., :n_loc], vi)
                + jnp.einsum('bhgqk,bkhd->bqhgd', p[..., n_loc:], vc))

    o = lax.map(block, (qb, kb, vb, mask))
    return jnp.moveaxis(o, 0, 1).reshape(B, N, H_D * DH_D)


def _to_chunks(x, L):
    B, N, H = x.shape[:3]
    rest = x.shape[3:]
    x = x.reshape((B, N // L, L, H) + rest)
    return x.transpose((1, 0, 3, 2) + tuple(range(4, x.ndim)))


def _from_chunks(y):
    nc, B, H, L = y.shape[:4]
    rest = y.shape[4:]
    y = y.transpose((1, 0, 3, 2) + tuple(range(4, y.ndim)))
    return y.reshape((B, nc * L, H) + rest)


def _mlstm_scan(q, k, v, ig, fg, state0):
    L = CHUNK_B
    qs = _to_chunks(q.astype(F32), L)
    ks = _to_chunks(k.astype(F32) * DK_B ** -0.5, L)
    vs = _to_chunks(v.astype(F32), L)
    i_s = _to_chunks(ig.astype(F32), L)
    f_s = _to_chunks(fg.astype(F32), L)
    tril = jnp.tril(jnp.ones((L, L), dtype=bool))

    def step(carry, xs):
        C, n, m = carry
        qc, kc, vc, ic, fc = xs
        b = jnp.cumsum(jax.nn.log_sigmoid(fc), axis=-1)
        a = b + m[..., None]
        d = jnp.where(tril, b[..., :, None] - b[..., None, :] + ic[..., None, :], NEG_INF)
        m_t = jnp.maximum(a, jnp.max(d, axis=-1))
        w_inter = jnp.exp(a - m_t)
        w_intra = jnp.exp(d - m_t[..., None])
        qk = jnp.einsum('bhtd,bhsd->bhts', qc, kc) * w_intra
        num = (w_inter[..., None] * jnp.einsum('bhtd,bhde->bhte', qc, C)
               + jnp.einsum('bhts,bhse->bhte', qk, vc))
        den = w_inter * jnp.einsum('bhtd,bhd->bht', qc, n) + jnp.sum(qk, axis=-1)
        h = num / jnp.maximum(jnp.abs(den), jnp.exp(-m_t))[..., None]
        kw = kc * w_intra[..., -1, :, None]
        decay = w_inter[..., -1]
        C_new = decay[..., None, None] * C + jnp.einsum('bhsd,bhse->bhde', kw, vc)
        n_new = decay[..., None] * n + jnp.sum(kw, axis=-2)
        return (C_new, n_new, m_t[..., -1]), h

    init = (state0[0].astype(F32), state0[1].astype(F32), state0[2].astype(F32))
    state, h = lax.scan(step, init, (qs, ks, vs, i_s, f_s))
    return _from_chunks(h), state


def _mlstm_bidir(p, b_if, st_f, st_b):
    B, N = p['b_q'].shape[:2]
    q = p['b_q'].reshape(B, N, H_B, DK_B)
    k = p['b_k'].reshape(B, N, H_B, DK_B)
    v = p['b_v'].reshape(B, N, H_B, DV_B)
    g = (p['b_if'] + b_if.reshape(-1)).reshape(B, N, 2, 2, H_B)
    h_f, s_f = _mlstm_scan(q, k, v, g[:, :, 0, 0], g[:, :, 0, 1], st_f)
    h_b, s_b = _mlstm_scan(_flip(q), _flip(k), _flip(v), _flip(g[:, :, 1, 0]), _flip(g[:, :, 1, 1]), st_b)
    return h_f + _flip(h_b), s_f, s_b


def _mlstm_out(h, o_pre, g):
    B, N = o_pre.shape[:2]
    hn = _rms(h.astype(o_pre.dtype), g.reshape(H_B, DV_B))
    return (hn * jax.nn.sigmoid(o_pre.reshape(B, N, H_B, DV_B))).reshape(B, N, H_B * DV_B)


def _gla_scan(q, k, v, log_a, S0):
    L = CHUNK_C
    qs = _to_chunks(q.astype(F32) * DK_C ** -0.5, L)
    ks = _to_chunks(k.astype(F32), L)
    vs = _to_chunks(v.astype(F32), L)
    gs = _to_chunks(log_a.astype(F32), L)
    tril = jnp.tril(jnp.ones((L, L), dtype=bool))

    def step(S, xs):
        qc, kc, vc, gc = xs
        Bc = jnp.cumsum(gc, axis=-2)
        inter = jnp.einsum('bhtd,bhde->bhte', qc * jnp.exp(Bc), S)
        diff = Bc[:, :, :, None, :] - Bc[:, :, None, :, :]
        decay = jnp.exp(jnp.where(tril[:, :, None], diff, NEG_INF))
        A = jnp.sum(qc[:, :, :, None, :] * kc[:, :, None, :, :] * decay, axis=-1)
        o = inter + jnp.einsum('bhts,bhse->bhte', A, vc)
        BL = Bc[:, :, -1, :]
        S_new = (jnp.exp(BL)[..., None] * S
                 + jnp.einsum('bhsd,bhse->bhde', kc * jnp.exp(BL[:, :, None, :] - Bc), vc))
        return S_new, o

    S, o = lax.scan(step, S0.astype(F32), (qs, ks, vs, gs))
    return _from_chunks(o), S


def _gla_bidir(p, w_up, b_up, S_f0, S_b0):
    B, N = p['c_q'].shape[:2]
    q = p['c_q'].reshape(B, N, H_C, DK_C)
    k = p['c_k'].reshape(B, N, H_C, DK_C)
    v = p['c_v'].reshape(B, N, H_C, DV_C)
    lr = p['c_glr'].reshape(B, N, 2, GATE_RANK_C)

    def log_gate(d):
        z = lr[:, :, d] @ w_up[d] + b_up[d]
        return (jax.nn.log_sigmoid(z.astype(F32)) / GATE_TAU_C).reshape(B, N, H_C, DK_C)

    o_f, S_f = _gla_scan(q, k, v, log_gate(0), S_f0)
    o_b, S_b = _gla_scan(_flip(q), _flip(k), _flip(v), _flip(log_gate(1)), S_b0)
    return o_f + _flip(o_b), S_f, S_b


def _gla_out(o, r_pre, g):
    B, N = r_pre.shape[:2]
    on = _rms(o.astype(r_pre.dtype), g.reshape(H_C, DV_C))
    return (on * jax.nn.silu(r_pre.reshape(B, N, H_C, DV_C))).reshape(B, N, H_C * DV_C)


def _merge(u, ys, w_branch, w_merge_gate, w_out):
    merged = None
    for i in range(N_BRANCH):
        gate = jax.nn.sigmoid(u @ w_merge_gate[:, i * D_MODEL:(i + 1) * D_MODEL])
        term = gate * (ys[i] @ w_branch[i])
        merged = term if merged is None else merged + term
    return merged @ w_out


def _mixer_context(u, lp, lam_init):
    B, N, _ = u.shape
    p = _project(u, lp['w_in'])
    ka = p['a_k'].reshape(B, N, H_A, 2, D_A)
    va = p['a_v'].reshape(B, N, H_A, 2 * D_A)
    lam = _diff_lambda(lp['w_a_lambda'], lam_init)
    y_a = _diff_out(_diff_attention(p['a_q'].reshape(B, N, H_A, 2, D_A), ka, va, lam), lp['w_a_subln'], lam_init)
    zero_b = (jnp.zeros((B, H_B, DK_B, DV_B), F32), jnp.zeros((B, H_B, DK_B), F32), jnp.zeros((B, H_B), F32))
    h_b, (C_f, n_f, m_f), (C_bw, n_bw, m_bw) = _mlstm_bidir(p, lp['b_b_if'], zero_b, zero_b)
    y_b = _mlstm_out(h_b, p['b_o'], lp['w_b_norm'])
    zero_c = jnp.zeros((B, H_C, DK_C, DV_C), F32)
    o_c, S_f, S_bw = _gla_bidir(p, lp['w_c_alpha_up'], lp['b_c_alpha'], zero_c, zero_c)
    y_c = _gla_out(o_c, p['c_r'], lp['w_c_norm'])
    kd = p['d_k'].reshape(B, N, HKV_D, DH_D)
    vd = p['d_v'].reshape(B, N, HKV_D, DH_D)
    y_d = _gqa_dense(p['d_q'].reshape(B, N, H_D, DH_D), kd, vd, lp['w_d_sink'])
    out = _merge(u, (y_a, y_b, y_c, y_d), lp['w_branch'], lp['w_merge_gate'], lp['w_out'])
    ctx = (ka.reshape(B, N, H_A, 2 * D_A), va,
           jnp.stack([C_f, C_bw], axis=1), jnp.stack([n_f, n_bw], axis=1), jnp.stack([m_f, m_bw], axis=1),
           jnp.stack([S_f, S_bw], axis=1), kd, vd)
    return out, ctx


def _mixer_latent(u, lp, lam_init, ak, av, bC, bn, bm, cS, kdc, vdc, rope_a, rope_d):
    B, N, _ = u.shape
    L = ak.shape[1]
    p = _project(u, lp['w_in'])
    cos_a, sin_a = rope_a
    qa = _rope(p['a_q'].reshape(B, N, H_A, 2, D_A), cos_a, sin_a)
    ka = _rope(p['a_k'].reshape(B, N, H_A, 2, D_A), cos_a, sin_a)
    va = p['a_v'].reshape(B, N, H_A, 2 * D_A)
    keys = jnp.concatenate([ka, ak.reshape(B, L, H_A, 2, D_A)], axis=1)
    vals = jnp.concatenate([va, av], axis=1)
    lam = _diff_lambda(lp['w_a_lambda'], lam_init)
    y_a = _diff_out(_diff_attention(qa, keys, vals, lam), lp['w_a_subln'], lam_init)
    h_b, _, _ = _mlstm_bidir(p, lp['b_b_if'], (bC[:, 0], bn[:, 0], bm[:, 0]), (bC[:, 1], bn[:, 1], bm[:, 1]))
    y_b = _mlstm_out(h_b, p['b_o'], lp['w_b_norm'])
    o_c, _, _ = _gla_bidir(p, lp['w_c_alpha_up'], lp['b_c_alpha'], cS[:, 0], cS[:, 1])
    y_c = _gla_out(o_c, p['c_r'], lp['w_c_norm'])
    cos_d, sin_d = rope_d
    qd = _rope(p['d_q'].reshape(B, N, H_D, DH_D), cos_d, sin_d)
    kd = _rope(p['d_k'].reshape(B, N, HKV_D, DH_D), cos_d, sin_d)
    vd = p['d_v'].reshape(B, N, HKV_D, DH_D)
    y_d = _gqa_banded(qd, kd, vd, kdc, vdc, lp['w_d_sink'])
    return _merge(u, (y_a, y_b, y_c, y_d), lp['w_branch'], lp['w_merge_gate'], lp['w_out'])


def _hier_moe(x, lp):
    B, N, D = x.shape
    T = B * N
    xt = x.reshape(T, D)
    lg = (xt @ lp['w_router_group'] + lp['b_router_group']).astype(F32)
    pg = jax.nn.softmax(lg, axis=-1)
    gsel = jnp.argmax(lg, axis=-1)
    pg_sel = jnp.take_along_axis(pg, gsel[:, None], axis=1)
    le = (xt @ lp['w_router_expert'] + lp['b_router_expert']).astype(F32).reshape(T, N_GROUPS, EXPERTS_PER_GROUP)
    le_sel = jnp.take_along_axis(le, gsel[:, None, None], axis=1)[:, 0]
    top_v, top_i = lax.top_k(le_sel, TOP_K_INNER)
    w_top = jax.nn.softmax(top_v, axis=-1) * pg_sel
    eidx = gsel[:, None] * EXPERTS_PER_GROUP + top_i
    gate = jnp.sum(jax.nn.one_hot(eidx, N_EXPERTS, dtype=F32) * w_top[..., None], axis=1)
    out = jnp.zeros((T, D), F32)
    for e in range(N_EXPERTS):
        hid = xt @ lp['w_exp_in'][e]
        y = (jax.nn.silu(hid[:, :D_EXPERT]) * hid[:, D_EXPERT:]) @ lp['w_exp_out'][e]
        out = out + gate[:, e:e + 1] * y
    return out.astype(x.dtype).reshape(B, N, D)


def setup_inputs(seed: int = 0) -> dict:
    key = jax.random.key(seed)
    ks = jax.random.split(key, 40)

    def nrm(i, shape, scale):
        return jax.random.normal(ks[i], shape, F32) * scale

    gate_offset = jnp.stack([jnp.zeros((H_B,), F32), jnp.linspace(3.0, 6.0, H_B, dtype=F32)])
    return {
        'x_prompt': nrm(0, (BATCH, SEQ, D_MODEL), 1.0),
        'x_sample': nrm(1, (DEC_BATCH, DEC_SEQ, D_MODEL), 1.0),
        'c': nrm(2, (DEC_BATCH, D_MODEL), 1.0),
        'cache_a_k': nrm(3, (DEC_BATCH, DEPTH, PAST_LEN, H_A, 2 * D_A), 1.0),
        'cache_a_v': nrm(4, (DEC_BATCH, DEPTH, PAST_LEN, H_A, 2 * D_A), 1.0),
        'state_b_C': nrm(5, (DEC_BATCH, DEPTH, 2, H_B, DK_B, DV_B), 0.1),
        'state_b_n': nrm(6, (DEC_BATCH, DEPTH, 2, H_B, DK_B), 0.1),
        'state_b_m': nrm(7, (DEC_BATCH, DEPTH, 2, H_B), 1.0),
        'state_c_S': nrm(8, (DEC_BATCH, DEPTH, 2, H_C, DK_C, DV_C), 0.1),
        'cache_d_k': nrm(9, (DEC_BATCH, DEPTH, PAST_LEN, HKV_D, DH_D), 1.0),
        'cache_d_v': nrm(10, (DEC_BATCH, DEPTH, PAST_LEN, HKV_D, DH_D), 1.0),
        'c_ctx': nrm(11, (D_MODEL,), 1.0),
        'w_ada': nrm(12, (DEPTH, D_MODEL, 6 * D_MODEL), 0.5 * D_MODEL ** -0.5),
        'b_ada': nrm(13, (DEPTH, 6 * D_MODEL), 0.01),
        'w_norm1': 1.0 + nrm(14, (DEPTH, D_MODEL), 0.02),
        'w_norm2': 1.0 + nrm(15, (DEPTH, D_MODEL), 0.02),
        'w_in': nrm(16, (DEPTH, D_MODEL, D_IN), D_MODEL ** -0.5),
        'w_a_lambda': nrm(17, (DEPTH, 4, D_A), 0.1),
        'w_a_subln': 1.0 + nrm(18, (DEPTH, 2 * D_A), 0.02),
        'b_b_if': gate_offset[None, None] + nrm(19, (DEPTH, 2, 2, H_B), 0.1),
        'w_b_norm': 1.0 + nrm(20, (DEPTH, H_B * DV_B), 0.02),
        'w_c_alpha_up': nrm(21, (DEPTH, 2, GATE_RANK_C, H_C * DK_C), GATE_RANK_C ** -0.5),
        'b_c_alpha': nrm(22, (DEPTH, 2, H_C * DK_C), 0.1),
        'w_c_norm': 1.0 + nrm(23, (DEPTH, H_C * DV_C), 0.02),
        'w_d_sink': nrm(24, (DEPTH, H_D), 1.0),
        'w_branch': nrm(25, (DEPTH, N_BRANCH, BRANCH_W, D_MODEL), BRANCH_W ** -0.5),
        'w_merge_gate': nrm(26, (DEPTH, D_MODEL, N_BRANCH * D_MODEL), D_MODEL ** -0.5),
        'w_out': nrm(27, (DEPTH, D_MODEL, D_MODEL), D_MODEL ** -0.5),
        'w_router_group': nrm(28, (DEPTH, D_MODEL, N_GROUPS), D_MODEL ** -0.5),
        'b_router_group': nrm(29, (DEPTH, N_GROUPS), 0.01),
        'w_router_expert': nrm(30, (DEPTH, D_MODEL, N_EXPERTS), D_MODEL ** -0.5),
        'b_router_expert': nrm(31, (DEPTH, N_EXPERTS), 0.01),
        'w_exp_in': nrm(32, (DEPTH, N_EXPERTS, D_MODEL, 2 * D_EXPERT), D_MODEL ** -0.5),
        'w_exp_out': nrm(33, (DEPTH, N_EXPERTS, D_EXPERT, D_MODEL), D_EXPERT ** -0.5),
        'w_norm_f': 1.0 + nrm(34, (D_MODEL,), 0.02),
    }


def reference(x_prompt, x_sample, c, cache_a_k, cache_a_v, state_b_C, state_b_n, state_b_m, state_c_S,
              cache_d_k, cache_d_v, c_ctx, w_ada, b_ada, w_norm1, w_norm2, w_in, w_a_lambda, w_a_subln,
              b_b_if, w_b_norm, w_c_alpha_up, b_c_alpha, w_c_norm, w_d_sink, w_branch, w_merge_gate, w_out,
              w_router_group, b_router_group, w_router_expert, b_router_expert, w_exp_in, w_exp_out, w_norm_f):
    rows = x_sample.shape[1] // GRID_W
    rope_a = _axial_rope_tables(rows, D_A)
    rope_d = _axial_rope_tables(rows, DH_D)
    xp = x_prompt
    xs = x_sample
    ctx_out = [[] for _ in range(8)]
    for l in range(DEPTH):
        lp = {
            'w_in': w_in[l], 'w_a_lambda': w_a_lambda[l], 'w_a_subln': w_a_subln[l],
            'b_b_if': b_b_if[l], 'w_b_norm': w_b_norm[l],
            'w_c_alpha_up': w_c_alpha_up[l], 'b_c_alpha': b_c_alpha[l], 'w_c_norm': w_c_norm[l],
            'w_d_sink': w_d_sink[l], 'w_branch': w_branch[l], 'w_merge_gate': w_merge_gate[l], 'w_out': w_out[l],
            'w_router_group': w_router_group[l], 'b_router_group': b_router_group[l],
            'w_router_expert': w_router_expert[l], 'b_router_expert': b_router_expert[l],
            'w_exp_in': w_exp_in[l], 'w_exp_out': w_exp_out[l],
        }
        lam_init = 0.8 - 0.6 * math.exp(-0.3 * l)
        sh1, sc1, g1, sh2, sc2, g2 = _modulation(c_ctx, w_ada[l], b_ada[l])
        mix, ctx = _mixer_context(_modulate(xp, w_norm1[l], sh1, sc1), lp, lam_init)
        xp = xp + g1 * mix
        xp = xp + g2 * _hier_moe(_modulate(xp, w_norm2[l], sh2, sc2), lp)
        for lst, t in zip(ctx_out, ctx):
            lst.append(t)
        sh1, sc1, g1, sh2, sc2, g2 = [m[:, None, :] for m in _modulation(c, w_ada[l], b_ada[l])]
        mix = _mixer_latent(_modulate(xs, w_norm1[l], sh1, sc1), lp, lam_init,
                            cache_a_k[:, l], cache_a_v[:, l], state_b_C[:, l], state_b_n[:, l], state_b_m[:, l],
                            state_c_S[:, l], cache_d_k[:, l], cache_d_v[:, l], rope_a, rope_d)
        xs = xs + g1 * mix
        xs = xs + g2 * _hier_moe(_modulate(xs, w_norm2[l], sh2, sc2), lp)
    y_prompt = _rms(xp, w_norm_f)
    y_sample = _rms(xs, w_norm_f)
    new_cache_a_k = jnp.stack(ctx_out[0], axis=1)
    new_cache_a_v = jnp.stack(ctx_out[1], axis=1)
    new_state_b_C = jnp.stack(ctx_out[2], axis=1)
    new_state_b_n = jnp.stack(ctx_out[3], axis=1)
    new_state_b_m = jnp.stack(ctx_out[4], axis=1)
    new_state_c_S = jnp.stack(ctx_out[5], axis=1)
    new_cache_d_k = jnp.stack(ctx_out[6], axis=1)
    new_cache_d_v = jnp.stack(ctx_out[7], axis=1)
    return (y_prompt, y_sample, new_cache_a_k, new_cache_a_v, new_state_b_C, new_state_b_n, new_state_b_m,
            new_state_c_S, new_cache_d_k, new_cache_d_v)
```

```python
import functools
import math

import jax
import jax.numpy as jnp
from jax import lax
from jax.experimental import pallas as pl
from jax.experimental.pallas import tpu as pltpu

D_MODEL = 4096
BATCH = 32
SEQ = 256
DEPTH = 4
DEC_BATCH = 8
DEC_SEQ = 2048
PAST_LEN = 512
GRID_W = 64
ROPE_BASE = 10000.0
EPS = 1e-6
NEG_INF = -1e30
Q_BLOCK = 128
H_A = 8
D_A = 64
H_B = 4
DK_B = 256
DV_B = 256
CHUNK_B = 128
H_C = 4
DK_C = 128
DV_C = 256
GATE_RANK_C = 16
GATE_TAU_C = 16.0
CHUNK_C = 64
H_D = 8
HKV_D = 4
DH_D = 128
WINDOW_D = 128
N_BRANCH = 4
BRANCH_W = 1024
N_GROUPS = 4
EXPERTS_PER_GROUP = 4
N_EXPERTS = N_GROUPS * EXPERTS_PER_GROUP
TOP_K_INNER = 2
D_EXPERT = 1024

F32 = jnp.float32
BF16 = jnp.bfloat16

T_CTX = BATCH * SEQ
T_LAT = DEC_BATCH * DEC_SEQ
T_ALL = T_CTX + T_LAT
N_SEG = 1 + DEC_BATCH

BIG_PARTS = (
    ('a_q', 1024), ('a_k', 1024), ('a_v', 1024),
    ('b_q', 1024), ('b_k', 1024), ('b_v', 1024), ('b_o', 1024),
    ('c_q', 512), ('c_k', 512), ('c_v', 1024), ('c_r', 1024),
    ('d_q', 1024), ('d_k', 512), ('d_v', 512),
)
D_BIG = sum(w for _, w in BIG_PARTS)
N_B_IF = 2 * 2 * H_B
N_C_GLR = 2 * GATE_RANK_C
D_SMALL = 128
OFF_B_IF = 7 * 1024
OFF_C_Q = OFF_B_IF + N_B_IF
OFF_C_GLR = OFF_C_Q + 512 + 512 + 1024 + 1024
OFF_D_Q = OFF_C_GLR + N_C_GLR

VMEM_LIMIT = 56 * 1024 * 1024


def _cparams(sem):
    return pltpu.CompilerParams(dimension_semantics=sem, vmem_limit_bytes=VMEM_LIMIT)


def _seg_of_tile(i, tm):
    n_ctx = T_CTX // tm
    per_lat = DEC_SEQ // tm
    return jnp.where(i < n_ctx, 0, 1 + (i - n_ctx) // per_lat)


def _norm_mod_kernel(x_ref, g_ref, sh_ref, sc_ref, o_ref):
    x = x_ref[...]
    y = x * lax.rsqrt(jnp.mean(x * x, axis=-1, keepdims=True) + EPS) * g_ref[...]
    o_ref[...] = (y * (1.0 + sc_ref[...]) + sh_ref[...]).astype(o_ref.dtype)


def _norm_mod(x, g, shift, scale, tm=256):
    T, D = x.shape
    seg = lambda i: (_seg_of_tile(i, tm), 0, 0)
    return pl.pallas_call(
        _norm_mod_kernel,
        grid=(T // tm,),
        in_specs=[pl.BlockSpec((tm, D), lambda i: (i, 0)),
                  pl.BlockSpec((1, D), lambda i: (0, 0)),
                  pl.BlockSpec((None, 1, D), seg),
                  pl.BlockSpec((None, 1, D), seg)],
        out_specs=pl.BlockSpec((tm, D), lambda i: (i, 0)),
        out_shape=jax.ShapeDtypeStruct((T, D), BF16),
        compiler_params=_cparams(("parallel",)),
        name="norm_mod",
    )(x, g, shift, scale)


def _norm_kernel(x_ref, g_ref, o_ref):
    x = x_ref[...]
    o_ref[...] = x * lax.rsqrt(jnp.mean(x * x, axis=-1, keepdims=True) + EPS) * g_ref[...]


def _norm(x, g, tm=256):
    T, D = x.shape
    return pl.pallas_call(
        _norm_kernel,
        grid=(T // tm,),
        in_specs=[pl.BlockSpec((tm, D), lambda i: (i, 0)),
                  pl.BlockSpec((1, D), lambda i: (0, 0))],
        out_specs=pl.BlockSpec((tm, D), lambda i: (i, 0)),
        out_shape=jax.ShapeDtypeStruct((T, D), F32),
        compiler_params=_cparams(("parallel",)),
        name="final_norm",
    )(x, g)


def _mm_kernel(a_ref, w_ref, o_ref):
    o_ref[...] = jnp.dot(a_ref[...], w_ref[...], preferred_element_type=F32).astype(o_ref.dtype)


def _mm(a, w, layer, out_dtype, tm=1024, tn=1024):
    T, K = a.shape
    N = w.shape[-1]
    return pl.pallas_call(
        _mm_kernel,
        grid=(T // tm, N // tn),
        in_specs=[pl.BlockSpec((tm, K), lambda i, j: (i, 0)),
                  pl.BlockSpec((None, K, tn), lambda i, j: (layer, 0, j))],
        out_specs=pl.BlockSpec((tm, tn), lambda i, j: (i, j)),
        out_shape=jax.ShapeDtypeStruct((T, N), out_dtype),
        compiler_params=_cparams(("parallel", "parallel")),
        name="proj_mm",
    )(a, w)


def _mm_res_kernel(a_ref, w_ref, x_ref, g_ref, o_ref):
    o_ref[...] = x_ref[...] + g_ref[...] * jnp.dot(a_ref[...], w_ref[...], preferred_element_type=F32)


def _mm_residual(a, w, layer, x, gate, tm=1024, tn=512):
    T, K = a.shape
    N = w.shape[-1]
    return pl.pallas_call(
        _mm_res_kernel,
        grid=(T // tm, N // tn),
        in_specs=[pl.BlockSpec((tm, K), lambda i, j: (i, 0)),
                  pl.BlockSpec((None, K, tn), lambda i, j: (layer, 0, j)),
                  pl.BlockSpec((tm, tn), lambda i, j: (i, j)),
                  pl.BlockSpec((None, 1, tn), lambda i, j: (_seg_of_tile(i, tm), 0, j))],
        out_specs=pl.BlockSpec((tm, tn), lambda i, j: (i, j)),
        out_shape=jax.ShapeDtypeStruct((T, N), F32),
        compiler_params=_cparams(("parallel", "parallel")),
        name="out_proj_residual",
    )(a, w, x, gate)


def _merge_kernel(u_ref, wg_ref, y_ref, wb_ref, o_ref, acc_ref):
    b = pl.program_id(2)
    gate = jax.nn.sigmoid(jnp.dot(u_ref[...], wg_ref[...], preferred_element_type=F32))
    term = gate * jnp.dot(y_ref[...], wb_ref[...], preferred_element_type=F32)

    @pl.when(b == 0)
    def _():
        acc_ref[...] = term

    @pl.when(b > 0)
    def _():
        acc_ref[...] += term

    @pl.when(b == N_BRANCH - 1)
    def _():
        o_ref[...] = acc_ref[...].astype(o_ref.dtype)


def _merge(u, ys, w_gate, w_branch, layer, tm=1024, tn=512):
    T, D = u.shape
    nj = D // tn
    return pl.pallas_call(
        _merge_kernel,
        grid=(T // tm, nj, N_BRANCH),
        in_specs=[pl.BlockSpec((tm, D), lambda i, j, b: (i, 0)),
                  pl.BlockSpec((None, D, tn), lambda i, j, b: (layer, 0, b * nj + j)),
                  pl.BlockSpec((None, tm, BRANCH_W), lambda i, j, b: (b, i, 0)),
                  pl.BlockSpec((None, None, BRANCH_W, tn), lambda i, j, b: (layer, b, 0, j))],
        out_specs=pl.BlockSpec((tm, tn), lambda i, j, b: (i, j)),
        out_shape=jax.ShapeDtypeStruct((T, D), BF16),
        scratch_shapes=[pltpu.VMEM((tm, tn), F32)],
        compiler_params=_cparams(("parallel", "parallel", "arbitrary")),
        name="branch_merge",
    )(u, w_gate, ys, w_branch)


MOE_TM = 512
MOE_HC = 256
MOE_ROWS = 2 * T_ALL + N_EXPERTS * MOE_TM
MOE_TILES = MOE_ROWS // MOE_TM


def _moe_kernel(te_ref, nt_ref, x_ref, w1_ref, w2_ref, wo_ref, rs_ref, o_ref):
    j = pl.program_id(0)
    h = pl.program_id(1)
    nh = pl.num_programs(1)
    used = j < nt_ref[0]

    @pl.when(used)
    def _():
        x = x_ref[...]
        h1 = jnp.dot(x, w1_ref[...], preferred_element_type=F32)
        h2 = jnp.dot(x, w2_ref[...], preferred_element_type=F32)
        act = (h1 * jax.nn.sigmoid(h1) * h2).astype(BF16)
        part = jnp.dot(act, wo_ref[...], preferred_element_type=F32)

        @pl.when(h == 0)
        def _():
            o_ref[...] = part

        @pl.when(h > 0)
        def _():
            o_ref[...] += part

        @pl.when(h == nh - 1)
        def _():
            o_ref[...] *= rs_ref[...]

    @pl.when(jnp.logical_and(jnp.logical_not(used), h == 0))
    def _():
        o_ref[...] = jnp.zeros_like(o_ref)


def _moe_experts(xs, tile_expert, n_tiles, row_scale, w_in, w_out, layer):
    D = xs.shape[1]
    nh = D_EXPERT // MOE_HC

    def xrow(j, h, te, nt):
        return (jnp.minimum(j, nt[0] - 1), 0)

    grid_spec = pltpu.PrefetchScalarGridSpec(
        num_scalar_prefetch=2,
        grid=(MOE_TILES, nh),
        in_specs=[pl.BlockSpec((MOE_TM, D), xrow),
                  pl.BlockSpec((None, None, D, MOE_HC), lambda j, h, te, nt: (layer, te[j], 0, h)),
                  pl.BlockSpec((None, None, D, MOE_HC), lambda j, h, te, nt: (layer, te[j], 0, h + nh)),
                  pl.BlockSpec((None, None, MOE_HC, D), lambda j, h, te, nt: (layer, te[j], h, 0)),
                  pl.BlockSpec((MOE_TM, 1), xrow)],
        out_specs=pl.BlockSpec((MOE_TM, D), lambda j, h, te, nt: (j, 0)),
    )
    return pl.pallas_call(
        _moe_kernel,
        grid_spec=grid_spec,
        out_shape=jax.ShapeDtypeStruct((MOE_ROWS, D), F32),
        compiler_params=_cparams(("arbitrary", "arbitrary")),
        name="moe_experts",
    )(tile_expert, n_tiles, xs, w_in, w_in, w_out, row_scale)


def _route(u2, w_rg, b_rg, w_re, b_re):
    T = u2.shape[0]
    uf = u2.astype(F32)
    lg = jnp.dot(uf, w_rg, precision=lax.Precision.HIGHEST) + b_rg
    pg = jax.nn.softmax(lg, axis=-1)
    gsel = jnp.argmax(lg, axis=-1)
    pg_sel = jnp.take_along_axis(pg, gsel[:, None], axis=1)
    le = (jnp.dot(uf, w_re, precision=lax.Precision.HIGHEST) + b_re).reshape(T, N_GROUPS, EXPERTS_PER_GROUP)
    le_sel = jnp.take_along_axis(le, gsel[:, None, None], axis=1)[:, 0]
    top_v, top_i = lax.top_k(le_sel, TOP_K_INNER)
    w_top = jax.nn.softmax(top_v, axis=-1) * pg_sel
    eidx = (gsel[:, None] * EXPERTS_PER_GROUP + top_i).astype(jnp.int32)
    return eidx, w_top


def _dispatch(eidx, w_top):
    T = eidx.shape[0]
    flat_e = eidx.reshape(-1)
    onehot = (flat_e[:, None] == jnp.arange(N_EXPERTS, dtype=jnp.int32)[None, :]).astype(jnp.int32)
    csum = jnp.cumsum(onehot, axis=0)
    counts = csum[-1]
    rank = jnp.sum(csum * onehot, axis=1) - 1
    padded = ((counts + MOE_TM - 1) // MOE_TM) * MOE_TM
    pend = jnp.cumsum(padded)
    pstart = pend - padded
    dest = (pstart[flat_e] + rank).astype(jnp.int32)
    tok = jnp.arange(2 * T, dtype=jnp.int32) // TOP_K_INNER
    src_tok = jnp.zeros((MOE_ROWS,), jnp.int32).at[dest].set(tok)
    row_scale = jnp.zeros((MOE_ROWS,), F32).at[dest].set(w_top.reshape(-1))
    n_tiles = (pend[-1] // MOE_TM).astype(jnp.int32)
    tile_start = jnp.arange(MOE_TILES, dtype=jnp.int32) * MOE_TM
    tile_expert = jnp.searchsorted(pend, tile_start, side='right').astype(jnp.int32)
    last_e = jnp.searchsorted(pend, (n_tiles - 1) * MOE_TM, side='right').astype(jnp.int32)
    tile_expert = jnp.where(jnp.arange(MOE_TILES) < n_tiles, tile_expert, last_e)
    return src_tok, row_scale[:, None], tile_expert, n_tiles.reshape(1), dest.reshape(T, TOP_K_INNER)


def _hier_moe(u2, lw, layer):
    eidx, w_top = _route(u2, lw['w_rg'], lw['b_rg'], lw['w_re'], lw['b_re'])
    src_tok, row_scale, tile_expert, n_tiles, dest = _dispatch(eidx, w_top)
    xs = jnp.take(u2, src_tok, axis=0)
    ys = _moe_experts(xs, tile_expert, n_tiles, row_scale, lw['w_exp_in'], lw['w_exp_out'], layer)
    return jnp.take(ys, dest[:, 0], axis=0), jnp.take(ys, dest[:, 1], axis=0)


def _residual_pair_kernel(x_ref, g_ref, a_ref, b_ref, o_ref):
    o_ref[...] = x_ref[...] + g_ref[...] * (a_ref[...] + b_ref[...])


def _residual_pair(x, gate, ya, yb, tm=256):
    T, D = x.shape
    row = pl.BlockSpec((tm, D), lambda i: (i, 0))
    return pl.pallas_call(
        _residual_pair_kernel,
        grid=(T // tm,),
        in_specs=[row, pl.BlockSpec((None, 1, D), lambda i: (_seg_of_tile(i, tm), 0, 0)), row, row],
        out_specs=row,
        out_shape=jax.ShapeDtypeStruct((T, D), F32),
        compiler_params=_cparams(("parallel",)),
        name="moe_residual",
    )(x, gate, ya, yb)


def _rms(x, g):
    xf = x.astype(F32)
    y = xf * lax.rsqrt(jnp.mean(xf * xf, axis=-1, keepdims=True) + EPS)
    return (y * g.astype(F32)).astype(x.dtype)


def _flip(t):
    return jnp.flip(t, axis=1)


def _axial_rope_tables(rows, dim):
    row = jnp.repeat(jnp.arange(rows, dtype=F32), GRID_W)
    col = jnp.tile(jnp.arange(GRID_W, dtype=F32), rows)
    n_freq = dim // 4
    inv = ROPE_BASE ** (-jnp.arange(n_freq, dtype=F32) / n_freq)
    ang = jnp.concatenate([row[:, None] * inv[None], col[:, None] * inv[None]], axis=-1)
    return jnp.cos(ang), jnp.sin(ang)


def _rope(x, cos, sin):
    half = x.shape[-1] // 2
    shp = (1, x.shape[1]) + (1,) * (x.ndim - 3) + (half,)
    c = cos.reshape(shp)
    s = sin.reshape(shp)
    x1, x2 = x[..., :half], x[..., half:]
    return jnp.concatenate([x1 * c - x2 * s, x1 * s + x2 * c], axis=-1)


def _softmax_with_sink(s, sink):
    sk = jnp.broadcast_to(sink.astype(F32)[None, :, :, None, None], s.shape[:-1] + (1,))
    return jax.nn.softmax(jnp.concatenate([s, sk], axis=-1), axis=-1)[..., :-1]


def _diff_lambda(w_lam, lam_init):
    lf = w_lam.astype(F32)
    return jnp.exp(jnp.sum(lf[0] * lf[1])) - jnp.exp(jnp.sum(lf[2] * lf[3])) + lam_init


def _diff_attention(q, k, v, lam):
    B, Nq = q.shape[:2]
    nb = Nq // Q_BLOCK
    qb = jnp.moveaxis(q.reshape((B, nb, Q_BLOCK) + q.shape[2:]), 1, 0)
    scale = D_A ** -0.5

    def block(qi):
        s = jnp.einsum('bqhcd,bkhcd->bhcqk', qi, k).astype(F32) * scale
        p = jax.nn.softmax(s, axis=-1)
        w = p[:, :, 0] - lam * p[:, :, 1]
        return jnp.einsum('bhqk,bkhe->bqhe', w, v)

    o = lax.map(block, qb)
    return jnp.moveaxis(o, 0, 1).reshape(B, Nq, H_A, 2 * D_A)


def _diff_out(o, g, lam_init):
    B, N = o.shape[:2]
    return (_rms(o, g) * (1.0 - lam_init)).reshape(B, N, H_A * 2 * D_A)


def _gqa_dense(q, k, v, sink):
    B, Nq = q.shape[:2]
    G = H_D // HKV_D
    nb = Nq // Q_BLOCK
    qb = jnp.moveaxis(q.reshape(B, nb, Q_BLOCK, HKV_D, G, DH_D), 1, 0)
    sink_g = sink.reshape(HKV_D, G)
    scale = DH_D ** -0.5

    def block(qi):
        s = jnp.einsum('bqhgd,bkhd->bhgqk', qi, k).astype(F32) * scale
        p = _softmax_with_sink(s, sink_g)
        return jnp.einsum('bhgqk,bkhd->bqhgd', p, v)

    o = lax.map(block, qb)
    return jnp.moveaxis(o, 0, 1).reshape(B, Nq, H_D * DH_D)


def _gqa_banded(q, k, v, kc, vc, sink):
    B, N = q.shape[:2]
    W = WINDOW_D
    G = H_D // HKV_D
    nb = N // W
    pad = ((0, 0), (W, W), (0, 0), (0, 0))

    def band(t):
        tp = jnp.pad(t, pad).reshape(B, nb + 2, W, HKV_D, DH_D)
        return jnp.concatenate([tp[:, :-2], tp[:, 1:-1], tp[:, 2:]], axis=2)

    kb = jnp.moveaxis(band(k), 1, 0)
    vb = jnp.moveaxis(band(v), 1, 0)
    blk = jnp.arange(nb)[:, None]
    qpos = blk * W + jnp.arange(W)[None, :]
    kpos = (blk - 1) * W + jnp.arange(3 * W)[None, :]
    mask = ((jnp.abs(qpos[:, :, None] - kpos[:, None, :]) <= W)
            & (kpos[:, None, :] >= 0) & (kpos[:, None, :] < N))
    qb = jnp.moveaxis(q.reshape(B, nb, W, HKV_D, G, DH_D), 1, 0)
    sink_g = sink.reshape(HKV_D, G)
    scale = DH_D ** -0.5
    n_loc = 3 * W

    def block(args):
        qi, ki, vi, mi = args
        s_loc = jnp.einsum('bqhgd,bkhd->bhgqk', qi, ki).astype(F32) * scale
        s_loc = jnp.where(mi[None, None, None], s_loc, NEG_INF)
        s_ctx = jnp.einsum('bqhgd,bkhd->bhgqk', qi, kc).astype(F32) * scale
        p = _softmax_with_sink(jnp.concatenate([s_loc, s_ctx], axis=-1), sink_g)
        return (jnp.einsum('bhgqk,bkhd->bqhgd', p[..., :n_loc], vi)
                + jnp.einsum('bhgqk,bkhd->bqhgd', p[..., n_loc:], vc))

    o = lax.map(block, (qb, kb, vb, mask))
    return jnp.moveaxis(o, 0, 1).reshape(B, N, H_D * DH_D)


def _to_chunks(x, L):
    B, N, H = x.shape[:3]
    rest = x.shape[3:]
    x = x.reshape((B, N // L, L, H) + rest)
    return x.transpose((1, 0, 3, 2) + tuple(range(4, x.ndim)))


def _from_chunks(y):
    nc, B, H, L = y.shape[:4]
    rest = y.shape[4:]
    y = y.transpose((1, 0, 3, 2) + tuple(range(4, y.ndim)))
    return y.reshape((B, nc * L, H) + rest)


def _mlstm_scan(q, k, v, ig, fg, state0):
    L = CHUNK_B
    qs = _to_chunks(q, L)
    ks = _to_chunks(k * DK_B ** -0.5, L)
    vs = _to_chunks(v, L)
    i_s = _to_chunks(ig, L)
    f_s = _to_chunks(fg, L)
    tril = jnp.tril(jnp.ones((L, L), dtype=bool))

    def step(carry, xs):
        C, n, m = carry
        qc, kc, vc, ic, fc = xs
        b = jnp.cumsum(jax.nn.log_sigmoid(fc), axis=-1)
        a = b + m[..., None]
        d = jnp.where(tril, b[..., :, None] - b[..., None, :] + ic[..., None, :], NEG_INF)
        m_t = jnp.maximum(a, jnp.max(d, axis=-1))
        w_inter = jnp.exp(a - m_t)
        w_intra = jnp.exp(d - m_t[..., None])
        qk = jnp.einsum('bhtd,bhsd->bhts', qc, kc) * w_intra
        num = (w_inter[..., None] * jnp.einsum('bhtd,bhde->bhte', qc, C)
               + jnp.einsum('bhts,bhse->bhte', qk, vc))
        den = w_inter * jnp.einsum('bhtd,bhd->bht', qc, n) + jnp.sum(qk, axis=-1)
        h = num / jnp.maximum(jnp.abs(den), jnp.exp(-m_t))[..., None]
        kw = kc * w_intra[..., -1, :, None]
        decay = w_inter[..., -1]
        C_new = decay[..., None, None] * C + jnp.einsum('bhsd,bhse->bhde', kw, vc)
        n_new = decay[..., None] * n + jnp.sum(kw, axis=-2)
        return (C_new, n_new, m_t[..., -1]), h

    state, h = lax.scan(step, state0, (qs, ks, vs, i_s, f_s))
    return _from_chunks(h), state


def _mlstm_bidir(p, b_if, st_f, st_b):
    B, N = p['b_q'].shape[:2]
    q = p['b_q'].reshape(B, N, H_B, DK_B)
    k = p['b_k'].reshape(B, N, H_B, DK_B)
    v = p['b_v'].reshape(B, N, H_B, DV_B)
    g = (p['b_if'] + b_if.reshape(-1)).reshape(B, N, 2, 2, H_B)
    h_f, s_f = _mlstm_scan(q, k, v, g[:, :, 0, 0], g[:, :, 0, 1], st_f)
    h_b, s_b = _mlstm_scan(_flip(q), _flip(k), _flip(v), _flip(g[:, :, 1, 0]), _flip(g[:, :, 1, 1]), st_b)
    return h_f + _flip(h_b), s_f, s_b


def _mlstm_out(h, o_pre, g):
    B, N = o_pre.shape[:2]
    hn = _rms(h, g.reshape(H_B, DV_B))
    return (hn * jax.nn.sigmoid(o_pre.reshape(B, N, H_B, DV_B))).reshape(B, N, H_B * DV_B)


def _gla_scan(q, k, v, log_a, S0):
    L = CHUNK_C
    qs = _to_chunks(q * DK_C ** -0.5, L)
    ks = _to_chunks(k, L)
    vs = _to_chunks(v, L)
    gs = _to_chunks(log_a, L)
    tril = jnp.tril(jnp.ones((L, L), dtype=bool))

    def step(S, xs):
        qc, kc, vc, gc = xs
        Bc = jnp.cumsum(gc, axis=-2)
        inter = jnp.einsum('bhtd,bhde->bhte', qc * jnp.exp(Bc), S)
        diff = Bc[:, :, :, None, :] - Bc[:, :, None, :, :]
        decay = jnp.exp(jnp.where(tril[:, :, None], diff, NEG_INF))
        A = jnp.sum(qc[:, :, :, None, :] * kc[:, :, None, :, :] * decay, axis=-1)
        o = inter + jnp.einsum('bhts,bhse->bhte', A, vc)
        BL = Bc[:, :, -1, :]
        S_new = (jnp.exp(BL)[..., None] * S
                 + jnp.einsum('bhsd,bhse->bhde', kc * jnp.exp(BL[:, :, None, :] - Bc), vc))
        return S_new, o

    S, o = lax.scan(step, S0, (qs, ks, vs, gs))
    return _from_chunks(o), S


def _gla_bidir(p, w_up, b_up, S_f0, S_b0):
    B, N = p['c_q'].shape[:2]
    q = p['c_q'].reshape(B, N, H_C, DK_C)
    k = p['c_k'].reshape(B, N, H_C, DK_C)
    v = p['c_v'].reshape(B, N, H_C, DV_C)
    lr = p['c_glr'].reshape(B, N, 2, GATE_RANK_C)

    def log_gate(d):
        z = lr[:, :, d] @ w_up[d] + b_up[d]
        return (jax.nn.log_sigmoid(z) / GATE_TAU_C).reshape(B, N, H_C, DK_C)

    o_f, S_f = _gla_scan(q, k, v, log_gate(0), S_f0)
    o_b, S_b = _gla_scan(_flip(q), _flip(k), _flip(v), _flip(log_gate(1)), S_b0)
    return o_f + _flip(o_b), S_f, S_b


def _gla_out(o, r_pre, g):
    B, N = r_pre.shape[:2]
    on = _rms(o, g.reshape(H_C, DV_C))
    return (on * jax.nn.silu(r_pre.reshape(B, N, H_C, DV_C))).reshape(B, N, H_C * DV_C)


def _split_parts(big, small, B, N):
    p = {}
    off = 0
    for name, width in BIG_PARTS:
        p[name] = big[:, off:off + width].reshape(B, N, width)
        off += width
    p['b_if'] = small[:, :N_B_IF].reshape(B, N, N_B_IF)
    p['c_glr'] = small[:, N_B_IF:N_B_IF + N_C_GLR].reshape(B, N, N_C_GLR)
    return p


def _mixer_context(p, lw, lam_init):
    B, N = BATCH, SEQ
    ka = p['a_k'].reshape(B, N, H_A, 2, D_A)
    va = p['a_v'].reshape(B, N, H_A, 2 * D_A)
    lam = _diff_lambda(lw['w_a_lambda'], lam_init)
    y_a = _diff_out(_diff_attention(p['a_q'].reshape(B, N, H_A, 2, D_A), ka, va, lam), lw['w_a_subln'], lam_init)
    zero_b = (jnp.zeros((B, H_B, DK_B, DV_B), F32), jnp.zeros((B, H_B, DK_B), F32), jnp.zeros((B, H_B), F32))
    h_b, (C_f, n_f, m_f), (C_bw, n_bw, m_bw) = _mlstm_bidir(p, lw['b_b_if'], zero_b, zero_b)
    y_b = _mlstm_out(h_b, p['b_o'], lw['w_b_norm'])
    zero_c = jnp.zeros((B, H_C, DK_C, DV_C), F32)
    o_c, S_f, S_bw = _gla_bidir(p, lw['w_c_alpha_up'], lw['b_c_alpha'], zero_c, zero_c)
    y_c = _gla_out(o_c, p['c_r'], lw['w_c_norm'])
    kd = p['d_k'].reshape(B, N, HKV_D, DH_D)
    vd = p['d_v'].reshape(B, N, HKV_D, DH_D)
    y_d = _gqa_dense(p['d_q'].reshape(B, N, H_D, DH_D), kd, vd, lw['w_d_sink'])
    ctx = (ka.reshape(B, N, H_A, 2 * D_A), va,
           jnp.stack([C_f, C_bw], axis=1), jnp.stack([n_f, n_bw], axis=1), jnp.stack([m_f, m_bw], axis=1),
           jnp.stack([S_f, S_bw], axis=1), kd, vd)
    return (y_a, y_b, y_c, y_d), ctx


def _mixer_latent(p, lw, lam_init, ak, av, bC, bn, bm, cS, kdc, vdc, rope_a, rope_d):
    B, N = DEC_BATCH, DEC_SEQ
    L = ak.shape[1]
    cos_a, sin_a = rope_a
    qa = _rope(p['a_q'].reshape(B, N, H_A, 2, D_A), cos_a, sin_a)
    ka = _rope(p['a_k'].reshape(B, N, H_A, 2, D_A), cos_a, sin_a)
    va = p['a_v'].reshape(B, N, H_A, 2 * D_A)
    keys = jnp.concatenate([ka, ak.reshape(B, L, H_A, 2, D_A)], axis=1)
    vals = jnp.concatenate([va, av], axis=1)
    lam = _diff_lambda(lw['w_a_lambda'], lam_init)
    y_a = _diff_out(_diff_attention(qa, keys, vals, lam), lw['w_a_subln'], lam_init)
    h_b, _, _ = _mlstm_bidir(p, lw['b_b_if'], (bC[:, 0], bn[:, 0], bm[:, 0]), (bC[:, 1], bn[:, 1], bm[:, 1]))
    y_b = _mlstm_out(h_b, p['b_o'], lw['w_b_norm'])
    o_c, _, _ = _gla_bidir(p, lw['w_c_alpha_up'], lw['b_c_alpha'], cS[:, 0], cS[:, 1])
    y_c = _gla_out(o_c, p['c_r'], lw['w_c_norm'])
    cos_d, sin_d = rope_d
    qd = _rope(p['d_q'].reshape(B, N, H_D, DH_D), cos_d, sin_d)
    kd = _rope(p['d_k'].reshape(B, N, HKV_D, DH_D), cos_d, sin_d)
    vd = p['d_v'].reshape(B, N, HKV_D, DH_D)
    y_d = _gqa_banded(qd, kd, vd, kdc, vdc, lw['w_d_sink'])
    return (y_a, y_b, y_c, y_d)


def _modulation_tables(c_ctx, c, w_ada_l, b_ada_l):
    cond = jnp.concatenate([c_ctx[None, :], c], axis=0)
    m = jnp.dot(jax.nn.silu(cond), w_ada_l, precision=lax.Precision.HIGHEST) + b_ada_l
    return m.reshape(N_SEG, 6, 1, D_MODEL)


def kernel(x_prompt, x_sample, c, cache_a_k, cache_a_v, state_b_C, state_b_n, state_b_m, state_c_S, cache_d_k, cache_d_v, c_ctx, w_ada, b_ada, w_norm1, w_norm2, w_in, w_a_lambda, w_a_subln, b_b_if, w_b_norm, w_c_alpha_up, b_c_alpha, w_c_norm, w_d_sink, w_branch, w_merge_gate, w_out, w_router_group, b_router_group, w_router_expert, b_router_expert, w_exp_in, w_exp_out, w_norm_f):
    rows = DEC_SEQ // GRID_W
    rope_a = _axial_rope_tables(rows, D_A)
    rope_d = _axial_rope_tables(rows, DH_D)

    w_big = jnp.concatenate([w_in[:, :, :OFF_B_IF], w_in[:, :, OFF_C_Q:OFF_C_GLR], w_in[:, :, OFF_D_Q:]],
                            axis=-1).astype(BF16)
    w_small = jnp.concatenate([w_in[:, :, OFF_B_IF:OFF_C_Q], w_in[:, :, OFF_C_GLR:OFF_D_Q],
                               jnp.zeros((DEPTH, D_MODEL, D_SMALL - N_B_IF - N_C_GLR), F32)], axis=-1).astype(BF16)
    w_gate_bf = w_merge_gate.astype(BF16)
    w_branch_bf = w_branch.astype(BF16)
    w_out_bf = w_out.astype(BF16)
    w_exp_in_bf = w_exp_in.astype(BF16)
    w_exp_out_bf = w_exp_out.astype(BF16)

    x = jnp.concatenate([x_prompt.reshape(T_CTX, D_MODEL), x_sample.reshape(T_LAT, D_MODEL)], axis=0)
    ctx_out = [[] for _ in range(8)]
    for l in range(DEPTH):
        lw = {
            'w_a_lambda': w_a_lambda[l], 'w_a_subln': w_a_subln[l], 'b_b_if': b_b_if[l], 'w_b_norm': w_b_norm[l],
            'w_c_alpha_up': w_c_alpha_up[l], 'b_c_alpha': b_c_alpha[l], 'w_c_norm': w_c_norm[l],
            'w_d_sink': w_d_sink[l],
            'w_rg': w_router_group[l], 'b_rg': b_router_group[l],
            'w_re': w_router_expert[l], 'b_re': b_router_expert[l],
            'w_exp_in': w_exp_in_bf, 'w_exp_out': w_exp_out_bf,
        }
        lam_init = 0.8 - 0.6 * math.exp(-0.3 * l)
        mods = _modulation_tables(c_ctx, c, w_ada[l], b_ada[l])
        sh1, sc1, g1, sh2, sc2, g2 = [mods[:, i] for i in range(6)]

        u = _norm_mod(x, w_norm1[l][None, :], sh1, sc1)
        big = _mm(u, w_big, l, F32)
        small = _mm(u, w_small, l, F32, tn=D_SMALL)

        p_ctx = _split_parts(big[:T_CTX], small[:T_CTX], BATCH, SEQ)
        p_lat = _split_parts(big[T_CTX:], small[T_CTX:], DEC_BATCH, DEC_SEQ)
        ys_ctx, ctx = _mixer_context(p_ctx, lw, lam_init)
        ys_lat = _mixer_latent(p_lat, lw, lam_init,
                               cache_a_k[:, l], cache_a_v[:, l], state_b_C[:, l], state_b_n[:, l], state_b_m[:, l],
                               state_c_S[:, l], cache_d_k[:, l], cache_d_v[:, l], rope_a, rope_d)
        for lst, t in zip(ctx_out, ctx):
            lst.append(t)
        ys = jnp.stack([jnp.concatenate([yc.reshape(T_CTX, BRANCH_W), yl.reshape(T_LAT, BRANCH_W)], axis=0)
                        for yc, yl in zip(ys_ctx, ys_lat)], axis=0).astype(BF16)

        merged = _merge(u, ys, w_gate_bf, w_branch_bf, l)
        x = _mm_residual(merged, w_out_bf, l, x, g1)

        u2 = _norm_mod(x, w_norm2[l][None, :], sh2, sc2)
        ya, yb = _hier_moe(u2, lw, l)
        x = _residual_pair(x, g2, ya, yb)

    y = _norm(x, w_norm_f[None, :])
    y_prompt = y[:T_CTX].reshape(BATCH, SEQ, D_MODEL)
    y_sample = y[T_CTX:].reshape(DEC_BATCH, DEC_SEQ, D_MODEL)
    outs = [jnp.stack(t, axis=1) for t in ctx_out]
    return (y_prompt, y_sample) + tuple(outs)
```

```python
import functools
import math

import jax
import jax.numpy as jnp
from jax import lax
from jax.experimental import pallas as pl
from jax.experimental.pallas import tpu as pltpu

D_MODEL = 4096
BATCH = 32
SEQ = 256
DEPTH = 4
DEC_BATCH = 8
DEC_SEQ = 2048
PAST_LEN = 512
GRID_W = 64
ROPE_BASE = 10000.0
EPS = 1e-6
NEG_INF = -1e30
H_A = 8
D_A = 64
H_B = 4
DK_B = 256
DV_B = 256
CHUNK_B = 128
H_C = 4
DK_C = 128
DV_C = 256
GATE_RANK_C = 16
GATE_TAU_C = 16.0
CHUNK_C = 64
SUB_C = 16
H_D = 8
HKV_D = 4
DH_D = 128
WINDOW_D = 128
N_BRANCH = 4
BRANCH_W = 1024
N_GROUPS = 4
EXPERTS_PER_GROUP = 4
N_EXPERTS = N_GROUPS * EXPERTS_PER_GROUP
TOP_K_INNER = 2
D_EXPERT = 1024

F32 = jnp.float32
BF16 = jnp.bfloat16
LANES = 128

T_CTX = BATCH * SEQ
T_LAT = DEC_BATCH * DEC_SEQ
T_ALL = T_CTX + T_LAT
N_SEG = 1 + DEC_BATCH

BIG_PARTS = (
    ('a_q', 1024), ('a_k', 1024), ('a_v', 1024),
    ('b_q', 1024), ('b_k', 1024), ('b_v', 1024), ('b_o', 1024),
    ('c_q', 512), ('c_k', 512), ('c_v', 1024), ('c_r', 1024),
    ('d_q', 1024), ('d_k', 512), ('d_v', 512),
)
D_BIG = sum(w for _, w in BIG_PARTS)
OFF = {}
_o = 0
for _name, _w in BIG_PARTS:
    OFF[_name] = _o
    _o += _w
N_B_IF = 2 * 2 * H_B
N_C_GLR = 2 * GATE_RANK_C
D_SMALL = LANES
OFF_B_IF = 7 * 1024
OFF_C_Q = OFF_B_IF + N_B_IF
OFF_C_GLR = OFF_C_Q + 512 + 512 + 1024 + 1024
OFF_D_Q = OFF_C_GLR + N_C_GLR

VMEM_LIMIT = 56 * 1024 * 1024


def _cparams(sem):
    return pltpu.CompilerParams(dimension_semantics=sem, vmem_limit_bytes=VMEM_LIMIT)


def _seg_of_tile(i, tm):
    n_ctx = T_CTX // tm
    per_lat = DEC_SEQ // tm
    return jnp.where(i < n_ctx, 0, 1 + (i - n_ctx) // per_lat)


def _dot(a, b):
    return jnp.dot(a, b, preferred_element_type=F32)


def _dot_nt(a, b):
    return lax.dot_general(a, b, (((1,), (1,)), ((), ())), preferred_element_type=F32)


def _dot_tn(a, b):
    return lax.dot_general(a, b, (((0,), (0,)), ((), ())), preferred_element_type=F32)


def _log_sigmoid(x):
    return jnp.minimum(x, 0.0) - jnp.log1p(jnp.exp(-jnp.abs(x)))


def _norm_mod_kernel(x_ref, g_ref, sh_ref, sc_ref, o_ref):
    x = x_ref[...]
    y = x * lax.rsqrt(jnp.mean(x * x, axis=-1, keepdims=True) + EPS) * g_ref[...]
    o_ref[...] = (y * (1.0 + sc_ref[...]) + sh_ref[...]).astype(o_ref.dtype)


def _norm_mod(x, g, shift, scale, tm=256):
    T, D = x.shape
    seg = lambda i: (_seg_of_tile(i, tm), 0, 0)
    return pl.pallas_call(
        _norm_mod_kernel,
        grid=(T // tm,),
        in_specs=[pl.BlockSpec((tm, D), lambda i: (i, 0)),
                  pl.BlockSpec((1, D), lambda i: (0, 0)),
                  pl.BlockSpec((None, 1, D), seg),
                  pl.BlockSpec((None, 1, D), seg)],
        out_specs=pl.BlockSpec((tm, D), lambda i: (i, 0)),
        out_shape=jax.ShapeDtypeStruct((T, D), BF16),
        compiler_params=_cparams(("parallel",)),
        name="norm_mod",
    )(x, g, shift, scale)


def _norm_kernel(x_ref, g_ref, o_ref):
    x = x_ref[...]
    o_ref[...] = x * lax.rsqrt(jnp.mean(x * x, axis=-1, keepdims=True) + EPS) * g_ref[...]


def _norm(x, g, tm=256):
    T, D = x.shape
    return pl.pallas_call(
        _norm_kernel,
        grid=(T // tm,),
        in_specs=[pl.BlockSpec((tm, D), lambda i: (i, 0)),
                  pl.BlockSpec((1, D), lambda i: (0, 0))],
        out_specs=pl.BlockSpec((tm, D), lambda i: (i, 0)),
        out_shape=jax.ShapeDtypeStruct((T, D), F32),
        compiler_params=_cparams(("parallel",)),
        name="final_norm",
    )(x, g)


def _mm_kernel(a_ref, w_ref, o_ref):
    o_ref[...] = _dot(a_ref[...], w_ref[...]).astype(o_ref.dtype)


def _mm(a, w, layer, out_dtype, tm=1024, tn=1024):
    T, K = a.shape
    N = w.shape[-1]
    return pl.pallas_call(
        _mm_kernel,
        grid=(T // tm, N // tn),
        in_specs=[pl.BlockSpec((tm, K), lambda i, j: (i, 0)),
                  pl.BlockSpec((None, K, tn), lambda i, j: (layer, 0, j))],
        out_specs=pl.BlockSpec((tm, tn), lambda i, j: (i, j)),
        out_shape=jax.ShapeDtypeStruct((T, N), out_dtype),
        compiler_params=_cparams(("parallel", "parallel")),
        name="proj_mm",
    )(a, w)


def _mm_res_kernel(a_ref, w_ref, x_ref, g_ref, o_ref):
    o_ref[...] = x_ref[...] + g_ref[...] * _dot(a_ref[...], w_ref[...])


def _mm_residual(a, w, layer, x, gate, tm=1024, tn=512):
    T, K = a.shape
    N = w.shape[-1]
    return pl.pallas_call(
        _mm_res_kernel,
        grid=(T // tm, N // tn),
        in_specs=[pl.BlockSpec((tm, K), lambda i, j: (i, 0)),
                  pl.BlockSpec((None, K, tn), lambda i, j: (layer, 0, j)),
                  pl.BlockSpec((tm, tn), lambda i, j: (i, j)),
                  pl.BlockSpec((None, 1, tn), lambda i, j: (_seg_of_tile(i, tm), 0, j))],
        out_specs=pl.BlockSpec((tm, tn), lambda i, j: (i, j)),
        out_shape=jax.ShapeDtypeStruct((T, N), F32),
        compiler_params=_cparams(("parallel", "parallel")),
        name="out_proj_residual",
    )(a, w, x, gate)


def _merge_kernel(u_ref, wg_ref, y_ref, wb_ref, o_ref, acc_ref):
    b = pl.program_id(2)
    gate = jax.nn.sigmoid(_dot(u_ref[...], wg_ref[...]))
    term = gate * _dot(y_ref[...], wb_ref[...])

    @pl.when(b == 0)
    def _():
        acc_ref[...] = term

    @pl.when(b > 0)
    def _():
        acc_ref[...] += term

    @pl.when(b == N_BRANCH - 1)
    def _():
        o_ref[...] = acc_ref[...].astype(o_ref.dtype)


def _merge(u, ys, w_gate, w_branch, layer, tm=1024, tn=512):
    T, D = u.shape
    nj = D // tn
    return pl.pallas_call(
        _merge_kernel,
        grid=(T // tm, nj, N_BRANCH),
        in_specs=[pl.BlockSpec((tm, D), lambda i, j, b: (i, 0)),
                  pl.BlockSpec((None, D, tn), lambda i, j, b: (layer, 0, b * nj + j)),
                  pl.BlockSpec((None, tm, BRANCH_W), lambda i, j, b: (b, i, 0)),
                  pl.BlockSpec((None, None, BRANCH_W, tn), lambda i, j, b: (layer, b, 0, j))],
        out_specs=pl.BlockSpec((tm, tn), lambda i, j, b: (i, j)),
        out_shape=jax.ShapeDtypeStruct((T, D), BF16),
        scratch_shapes=[pltpu.VMEM((tm, tn), F32)],
        compiler_params=_cparams(("parallel", "parallel", "arbitrary")),
        name="branch_merge",
    )(u, w_gate, ys, w_branch)


MOE_TM = 512
MOE_HC = 256
MOE_ROWS = 2 * T_ALL + N_EXPERTS * MOE_TM
MOE_TILES = MOE_ROWS // MOE_TM


def _moe_kernel(te_ref, nt_ref, x_ref, w1_ref, w2_ref, wo_ref, rs_ref, o_ref):
    j = pl.program_id(0)
    h = pl.program_id(1)
    nh = pl.num_programs(1)
    used = j < nt_ref[0]

    @pl.when(used)
    def _():
        x = x_ref[...]
        h1 = _dot(x, w1_ref[...])
        h2 = _dot(x, w2_ref[...])
        act = (h1 * jax.nn.sigmoid(h1) * h2).astype(BF16)
        part = _dot(act, wo_ref[...])

        @pl.when(h == 0)
        def _():
            o_ref[...] = part

        @pl.when(h > 0)
        def _():
            o_ref[...] += part

        @pl.when(h == nh - 1)
        def _():
            o_ref[...] *= rs_ref[...]

    @pl.when(jnp.logical_and(jnp.logical_not(used), h == 0))
    def _():
        o_ref[...] = jnp.zeros_like(o_ref)


def _moe_experts(xs, tile_expert, n_tiles, row_scale, w_in, w_out, layer):
    D = xs.shape[1]
    nh = D_EXPERT // MOE_HC

    def xrow(j, h, te, nt):
        return (jnp.minimum(j, nt[0] - 1), 0)

    grid_spec = pltpu.PrefetchScalarGridSpec(
        num_scalar_prefetch=2,
        grid=(MOE_TILES, nh),
        in_specs=[pl.BlockSpec((MOE_TM, D), xrow),
                  pl.BlockSpec((None, None, D, MOE_HC), lambda j, h, te, nt: (layer, te[j], 0, h)),
                  pl.BlockSpec((None, None, D, MOE_HC), lambda j, h, te, nt: (layer, te[j], 0, h + nh)),
                  pl.BlockSpec((None, None, MOE_HC, D), lambda j, h, te, nt: (layer, te[j], h, 0)),
                  pl.BlockSpec((MOE_TM, 1), xrow)],
        out_specs=pl.BlockSpec((MOE_TM, D), lambda j, h, te, nt: (j, 0)),
    )
    return pl.pallas_call(
        _moe_kernel,
        grid_spec=grid_spec,
        out_shape=jax.ShapeDtypeStruct((MOE_ROWS, D), F32),
        compiler_params=_cparams(("arbitrary", "arbitrary")),
        name="moe_experts",
    )(tile_expert, n_tiles, xs, w_in, w_in, w_out, row_scale)


def _route(u2, w_rg, b_rg, w_re, b_re):
    T = u2.shape[0]
    uf = u2.astype(F32)
    lg = jnp.dot(uf, w_rg, precision=lax.Precision.HIGHEST) + b_rg
    pg = jax.nn.softmax(lg, axis=-1)
    gsel = jnp.argmax(lg, axis=-1)
    pg_sel = jnp.take_along_axis(pg, gsel[:, None], axis=1)
    le = (jnp.dot(uf, w_re, precision=lax.Precision.HIGHEST) + b_re).reshape(T, N_GROUPS, EXPERTS_PER_GROUP)
    le_sel = jnp.take_along_axis(le, gsel[:, None, None], axis=1)[:, 0]
    top_v, top_i = lax.top_k(le_sel, TOP_K_INNER)
    w_top = jax.nn.softmax(top_v, axis=-1) * pg_sel
    eidx = (gsel[:, None] * EXPERTS_PER_GROUP + top_i).astype(jnp.int32)
    return eidx, w_top


def _dispatch(eidx, w_top):
    T = eidx.shape[0]
    flat_e = eidx.reshape(-1)
    onehot = (flat_e[:, None] == jnp.arange(N_EXPERTS, dtype=jnp.int32)[None, :]).astype(jnp.int32)
    csum = jnp.cumsum(onehot, axis=0)
    counts = csum[-1]
    rank = jnp.sum(csum * onehot, axis=1) - 1
    padded = ((counts + MOE_TM - 1) // MOE_TM) * MOE_TM
    pend = jnp.cumsum(padded)
    pstart = pend - padded
    dest = (pstart[flat_e] + rank).astype(jnp.int32)
    tok = jnp.arange(2 * T, dtype=jnp.int32) // TOP_K_INNER
    src_tok = jnp.zeros((MOE_ROWS,), jnp.int32).at[dest].set(tok)
    row_scale = jnp.zeros((MOE_ROWS,), F32).at[dest].set(w_top.reshape(-1))
    n_tiles = (pend[-1] // MOE_TM).astype(jnp.int32)
    tile_start = jnp.arange(MOE_TILES, dtype=jnp.int32) * MOE_TM
    tile_expert = jnp.searchsorted(pend, tile_start, side='right').astype(jnp.int32)
    last_e = jnp.searchsorted(pend, (n_tiles - 1) * MOE_TM, side='right').astype(jnp.int32)
    tile_expert = jnp.where(jnp.arange(MOE_TILES) < n_tiles, tile_expert, last_e)
    return src_tok, row_scale[:, None], tile_expert, n_tiles.reshape(1), dest.reshape(T, TOP_K_INNER)


def _hier_moe(u2, lw, layer):
    eidx, w_top = _route(u2, lw['w_rg'], lw['b_rg'], lw['w_re'], lw['b_re'])
    src_tok, row_scale, tile_expert, n_tiles, dest = _dispatch(eidx, w_top)
    xs = jnp.take(u2, src_tok, axis=0)
    ys = _moe_experts(xs, tile_expert, n_tiles, row_scale, lw['w_exp_in'], lw['w_exp_out'], layer)
    return jnp.take(ys, dest[:, 0], axis=0), jnp.take(ys, dest[:, 1], axis=0)


def _residual_pair_kernel(x_ref, g_ref, a_ref, b_ref, o_ref):
    o_ref[...] = x_ref[...] + g_ref[...] * (a_ref[...] + b_ref[...])


def _residual_pair(x, gate, ya, yb, tm=256):
    T, D = x.shape
    row = pl.BlockSpec((tm, D), lambda i: (i, 0))
    return pl.pallas_call(
        _residual_pair_kernel,
        grid=(T // tm,),
        in_specs=[row, pl.BlockSpec((None, 1, D), lambda i: (_seg_of_tile(i, tm), 0, 0)), row, row],
        out_specs=row,
        out_shape=jax.ShapeDtypeStruct((T, D), F32),
        compiler_params=_cparams(("parallel",)),
        name="moe_residual",
    )(x, gate, ya, yb)


def _axial_rope_tables(rows, dim):
    row = jnp.repeat(jnp.arange(rows, dtype=F32), GRID_W)
    col = jnp.tile(jnp.arange(GRID_W, dtype=F32), rows)
    n_freq = dim // 4
    inv = ROPE_BASE ** (-jnp.arange(n_freq, dtype=F32) / n_freq)
    ang = jnp.concatenate([row[:, None] * inv[None], col[:, None] * inv[None]], axis=-1)
    return jnp.cos(ang), jnp.sin(ang)


def _rope_lane_tables(rows, dim):
    cos, sin = _axial_rope_tables(rows, dim)
    groups = LANES // dim
    return (jnp.tile(cos, (1, 2 * groups)),
            jnp.tile(jnp.concatenate([-sin, sin], axis=-1), (1, groups)))


def _rope_lanes(x, cos_t, sin_t, dim):
    half = dim // 2
    if dim == LANES:
        swapped = pltpu.roll(x, half, 1)
    else:
        lane = lax.broadcasted_iota(jnp.int32, x.shape, 1)
        swapped = jnp.where((lane & half) == 0, pltpu.roll(x, LANES - half, 1), pltpu.roll(x, half, 1))
    return x * cos_t + swapped * sin_t


def _diff_attn_kernel(*refs, n_lat, n_cache, rope, tq, lam_init):
    if rope:
        (q_ref, k_ref, v_ref, lam_ref, g_ref, ck_ref, cv_ref, cosq_ref, sinq_ref, cosk_ref, sink_ref,
         o_ref, kbuf, vbuf) = refs
    else:
        q_ref, k_ref, v_ref, lam_ref, g_ref, o_ref, kbuf, vbuf = refs

    @pl.when(pl.program_id(2) == 0)
    def _():
        k = k_ref[...]
        if rope:
            k = _rope_lanes(k, cosk_ref[...], sink_ref[...], D_A)
        kbuf[0:n_lat, :] = k.astype(BF16)
        vbuf[0:n_lat, :] = v_ref[...].astype(BF16)
        if n_cache:
            kbuf[n_lat:n_lat + n_cache, :] = ck_ref[...].astype(BF16)
            vbuf[n_lat:n_lat + n_cache, :] = cv_ref[...].astype(BF16)

    q = q_ref[...]
    if rope:
        q = _rope_lanes(q, cosq_ref[...], sinq_ref[...], D_A)
    q = q * (D_A ** -0.5)
    lane = lax.broadcasted_iota(jnp.int32, q.shape, 1)
    qs = jnp.concatenate([jnp.where(lane < D_A, q, 0.0), jnp.where(lane >= D_A, q, 0.0)], axis=0).astype(BF16)
    s = _dot_nt(qs, kbuf[...])
    e = jnp.exp(s - jnp.max(s, axis=-1, keepdims=True))
    den = jnp.sum(e, axis=-1, keepdims=True)
    pv = _dot(e.astype(BF16), vbuf[...]) / den
    lf = lam_ref[...]
    lam = (jnp.exp(jnp.sum(lf[0:1] * lf[1:2], keepdims=True))
           - jnp.exp(jnp.sum(lf[2:3] * lf[3:4], keepdims=True)) + lam_init)
    o = pv[0:tq] - lam * pv[tq:2 * tq]
    y = o * lax.rsqrt(jnp.mean(o * o, axis=-1, keepdims=True) + EPS) * g_ref[...]
    o_ref[...] = (y * (1.0 - lam_init)).astype(o_ref.dtype)


def _diff_attention(big, row0, B, N, layer, w_lam, w_subln, lam_init, cache=None, rope_tabs=None, tq=256):
    rope = rope_tabs is not None
    n_cache = cache[0].shape[2] if cache is not None else 0
    nq = N // tq
    hw = 2 * D_A
    qc, kc, vc = OFF['a_q'] // hw, OFF['a_k'] // hw, OFF['a_v'] // hw
    in_specs = [pl.BlockSpec((tq, hw), lambda b, h, i: (row0 // tq + b * nq + i, qc + h)),
                pl.BlockSpec((N, hw), lambda b, h, i: (row0 // N + b, kc + h)),
                pl.BlockSpec((N, hw), lambda b, h, i: (row0 // N + b, vc + h)),
                pl.BlockSpec((None, 4, D_A), lambda b, h, i: (layer, 0, 0)),
                pl.BlockSpec((None, 1, hw), lambda b, h, i: (layer, 0, 0))]
    args = [big, big, big, w_lam, w_subln]
    if rope:
        cspec = pl.BlockSpec((None, None, n_cache, hw), lambda b, h, i: (b, layer, 0, h))
        in_specs += [cspec, cspec,
                     pl.BlockSpec((tq, hw), lambda b, h, i: (i, 0)), pl.BlockSpec((tq, hw), lambda b, h, i: (i, 0)),
                     pl.BlockSpec((N, hw), lambda b, h, i: (0, 0)), pl.BlockSpec((N, hw), lambda b, h, i: (0, 0))]
        args += [cache[0], cache[1], rope_tabs[0], rope_tabs[1], rope_tabs[0], rope_tabs[1]]
    return pl.pallas_call(
        functools.partial(_diff_attn_kernel, n_lat=N, n_cache=n_cache, rope=rope, tq=tq, lam_init=lam_init),
        grid=(B, H_A, nq),
        in_specs=in_specs,
        out_specs=pl.BlockSpec((tq, hw), lambda b, h, i: (b * nq + i, h)),
        out_shape=jax.ShapeDtypeStruct((B * N, BRANCH_W), BF16),
        scratch_shapes=[pltpu.VMEM((N + n_cache, hw), BF16), pltpu.VMEM((N + n_cache, hw), BF16)],
        compiler_params=_cparams(("parallel", "parallel", "arbitrary")),
        name="diff_attention",
    )(*args)


GQA_G = H_D // HKV_D


def _gqa_kernel(*refs, n_lat, n_cache, banded, tq, layer):
    if banded:
        (sink_ref, q_ref, k_ref, v_ref, ck_ref, cv_ref, cosq_ref, sinq_ref, cosk_ref, sink_tab_ref,
         o_ref, kbuf, vbuf) = refs
    else:
        sink_ref, q_ref, k_ref, v_ref, o_ref, kbuf, vbuf = refs
    W = WINDOW_D
    kv = pl.program_id(1)
    qi = pl.program_id(2)

    @pl.when(qi == 0)
    def _():
        k = k_ref[...]
        if banded:
            k = _rope_lanes(k, cosk_ref[...], sink_tab_ref[...], DH_D)
            zeros = jnp.zeros((W, DH_D), BF16)
            for buf, lat, cached in ((kbuf, k, ck_ref), (vbuf, v_ref[...], cv_ref)):
                buf[0:W, :] = zeros
                buf[W:W + n_lat, :] = lat.astype(BF16)
                buf[W + n_lat:2 * W + n_lat, :] = zeros
                buf[2 * W + n_lat:2 * W + n_lat + n_cache, :] = cached[...].astype(BF16)
        else:
            kbuf[...] = k.astype(BF16)
            vbuf[...] = v_ref[...].astype(BF16)

    q2 = q_ref[...]
    heads = [q2[:, g * DH_D:(g + 1) * DH_D] for g in range(GQA_G)]
    if banded:
        heads = [_rope_lanes(qh, cosq_ref[...], sinq_ref[...], DH_D) for qh in heads]
    qs = (jnp.concatenate(heads, axis=0) * (DH_D ** -0.5)).astype(BF16)
    row = lax.broadcasted_iota(jnp.int32, (GQA_G * tq, 1), 0)
    sink = jnp.full((GQA_G * tq, 1), sink_ref[layer, kv * GQA_G], F32)
    for g in range(1, GQA_G):
        sink = jnp.where(row >= g * tq, sink_ref[layer, kv * GQA_G + g], sink)

    if banded:
        start = pl.multiple_of(qi * W, W)
        s_loc = _dot_nt(qs, kbuf[pl.ds(start, 3 * W), :])
        r = lax.broadcasted_iota(jnp.int32, s_loc.shape, 0) & (tq - 1)
        c = lax.broadcasted_iota(jnp.int32, s_loc.shape, 1)
        kpos = qi * W + c - W
        valid = (jnp.abs(c - W - r) <= W) & (kpos >= 0) & (kpos < n_lat)
        s_loc = jnp.where(valid, s_loc, NEG_INF)
        s_ctx = _dot_nt(qs, kbuf[2 * W + n_lat:2 * W + n_lat + n_cache, :])
        m = jnp.maximum(jnp.maximum(jnp.max(s_loc, axis=-1, keepdims=True),
                                    jnp.max(s_ctx, axis=-1, keepdims=True)), sink)
        e_loc = jnp.exp(s_loc - m)
        e_ctx = jnp.exp(s_ctx - m)
        den = (jnp.sum(e_loc, axis=-1, keepdims=True) + jnp.sum(e_ctx, axis=-1, keepdims=True)
               + jnp.exp(sink - m))
        o = (_dot(e_loc.astype(BF16), vbuf[pl.ds(start, 3 * W), :])
             + _dot(e_ctx.astype(BF16), vbuf[2 * W + n_lat:2 * W + n_lat + n_cache, :])) / den
    else:
        s = _dot_nt(qs, kbuf[...])
        m = jnp.maximum(jnp.max(s, axis=-1, keepdims=True), sink)
        e = jnp.exp(s - m)
        den = jnp.sum(e, axis=-1, keepdims=True) + jnp.exp(sink - m)
        o = _dot(e.astype(BF16), vbuf[...]) / den
    o_ref[...] = jnp.concatenate([o[g * tq:(g + 1) * tq] for g in range(GQA_G)], axis=1).astype(o_ref.dtype)


def _gqa(big, row0, B, N, layer, w_sink, cache=None, rope_tabs=None, tq=WINDOW_D):
    banded = cache is not None
    n_cache = cache[0].shape[2] if banded else 0
    nq = N // tq
    qw = GQA_G * DH_D
    qc, kc, vc = OFF['d_q'] // qw, OFF['d_k'] // DH_D, OFF['d_v'] // DH_D
    in_specs = [pl.BlockSpec(memory_space=pltpu.SMEM),
                pl.BlockSpec((tq, qw), lambda b, h, i: (row0 // tq + b * nq + i, qc + h)),
                pl.BlockSpec((N, DH_D), lambda b, h, i: (row0 // N + b, kc + h)),
                pl.BlockSpec((N, DH_D), lambda b, h, i: (row0 // N + b, vc + h))]
    args = [w_sink, big, big, big]
    n_rows = N
    if banded:
        assert tq == WINDOW_D
        cspec = pl.BlockSpec((None, None, n_cache, DH_D), lambda b, h, i: (b, layer, 0, h))
        in_specs += [cspec, cspec,
                     pl.BlockSpec((tq, DH_D), lambda b, h, i: (i, 0)), pl.BlockSpec((tq, DH_D), lambda b, h, i: (i, 0)),
                     pl.BlockSpec((N, DH_D), lambda b, h, i: (0, 0)), pl.BlockSpec((N, DH_D), lambda b, h, i: (0, 0))]
        args += [cache[0], cache[1], rope_tabs[0], rope_tabs[1], rope_tabs[0], rope_tabs[1]]
        n_rows = N + 2 * WINDOW_D + n_cache
    return pl.pallas_call(
        functools.partial(_gqa_kernel, n_lat=N, n_cache=n_cache, banded=banded, tq=tq, layer=layer),
        grid=(B, HKV_D, nq),
        in_specs=in_specs,
        out_specs=pl.BlockSpec((tq, qw), lambda b, h, i: (b * nq + i, h)),
        out_shape=jax.ShapeDtypeStruct((B * N, BRANCH_W), BF16),
        scratch_shapes=[pltpu.VMEM((n_rows, DH_D), BF16), pltpu.VMEM((n_rows, DH_D), BF16)],
        compiler_params=_cparams(("parallel", "parallel", "arbitrary")),
        name="gqa_sink_attention",
    )(*args)


def _mlstm_chunk(q, k, v, i_col, f_col, i_row, f_row, C, n_row, m, rev):
    L = CHUNK_B
    lf_col = _log_sigmoid(f_col)
    lf_row = _log_sigmoid(f_row)
    t_i = lax.broadcasted_iota(jnp.int32, (L, L), 0)
    s_i = lax.broadcasted_iota(jnp.int32, (L, L), 1)
    tri = (s_i >= t_i) if rev else (s_i <= t_i)
    tri_t = (t_i >= s_i) if rev else (t_i <= s_i)
    b_col = jnp.sum(jnp.where(tri, lf_row, 0.0), axis=1, keepdims=True)
    b_row = jnp.sum(jnp.where(tri_t, lf_col, 0.0), axis=0, keepdims=True)
    a_col = b_col + m
    d = jnp.where(tri, b_col - b_row + i_row, NEG_INF)
    m_t = jnp.maximum(a_col, jnp.max(d, axis=1, keepdims=True))
    w_inter = jnp.exp(a_col - m_t)
    w_intra = jnp.exp(d - m_t)
    qb, kb, vb = q.astype(BF16), k.astype(BF16), v.astype(BF16)
    k_scale = DK_B ** -0.5
    qk = _dot_nt(qb, kb) * k_scale * w_intra
    num = w_inter * _dot(qb, C.astype(BF16)) + _dot(qk.astype(BF16), vb)
    den = w_inter * jnp.sum(q * n_row, axis=1, keepdims=True) + jnp.sum(qk, axis=1, keepdims=True)
    h = num / jnp.maximum(jnp.abs(den), jnp.exp(-m_t))
    last = 0 if rev else L - 1
    b_last = b_col[last:last + 1]
    m_last = m_t[last:last + 1]
    kw = k * (k_scale * jnp.exp(b_last - b_col + i_col - m_last))
    decay = w_inter[last:last + 1]
    C_new = decay * C + _dot_tn(kw.astype(BF16), vb)
    n_new = decay * n_row + jnp.sum(kw, axis=0, keepdims=True)
    return h, C_new, n_new, m_last


def _mlstm_kernel(*refs, zero_init, emit_state):
    refs = list(refs)
    qf, kf, vf, qb, kb, vb, gcf, gcb, grf, grb = refs[:10]
    pos = 10
    if not zero_init:
        c0_ref, n0_ref, m0_ref = refs[pos:pos + 3]
        pos += 3
    hf_ref, hb_ref = refs[pos:pos + 2]
    pos += 2
    if emit_state:
        co_ref, no_ref, mo_ref = refs[pos:pos + 3]
        pos += 3
    c_s, n_s, m_s = refs[pos:pos + 3]
    c = pl.program_id(2)

    @pl.when(c == 0)
    def _():
        if zero_init:
            c_s[...] = jnp.zeros_like(c_s)
            n_s[...] = jnp.zeros_like(n_s)
            m_s[...] = jnp.zeros_like(m_s)
        else:
            c_s[...] = c0_ref[...]
            n_s[...] = n0_ref[...]
            m_s[...] = m0_ref[...]

    streams = ((0, qf, kf, vf, gcf, grf, hf_ref), (1, qb, kb, vb, gcb, grb, hb_ref))
    for dr, q_ref, k_ref, v_ref, gc_ref, gr_ref, h_ref in streams:
        gc = gc_ref[...]
        gr = gr_ref[...]
        h, c_new, n_new, m_new = _mlstm_chunk(
            q_ref[...], k_ref[...], v_ref[...],
            gc[:, 2 * dr:2 * dr + 1], gc[:, 2 * dr + 1:2 * dr + 2],
            gr[2 * dr:2 * dr + 1, :], gr[2 * dr + 1:2 * dr + 2, :],
            c_s[dr], n_s[dr], m_s[dr], rev=bool(dr))
        h_ref[...] = h
        c_s[dr] = c_new
        n_s[dr] = n_new
        m_s[dr] = m_new

    if emit_state:
        @pl.when(c == pl.num_programs(2) - 1)
        def _():
            co_ref[...] = c_s[...]
            no_ref[...] = n_s[...]
            mo_ref[...] = m_s[...]


def _mlstm(big, gate_cols, gate_rows, row0, B, N, layer, state=None, emit_state=False):
    L = CHUNK_B
    nc = N // L
    r0 = row0 // L
    qc, kc, vc = OFF['b_q'] // DK_B, OFF['b_k'] // DK_B, OFF['b_v'] // DV_B
    fwd = lambda b, h, c: r0 + b * nc + c
    bwd = lambda b, h, c: r0 + b * nc + (nc - 1 - c)
    in_specs, args = [], []
    for rowf in (fwd, bwd):
        for col, width in ((qc, DK_B), (kc, DK_B), (vc, DV_B)):
            in_specs.append(pl.BlockSpec((L, width), lambda b, h, c, rowf=rowf, col=col: (rowf(b, h, c), col + h)))
            args.append(big)
    for rowf in (fwd, bwd):
        in_specs.append(pl.BlockSpec((None, L, 4), lambda b, h, c, rowf=rowf: (h, rowf(b, h, c), 0)))
        args.append(gate_cols)
    for rowf in (fwd, bwd):
        in_specs.append(pl.BlockSpec((None, 4, L), lambda b, h, c, rowf=rowf: (h, 0, rowf(b, h, c))))
        args.append(gate_rows)
    if state is not None:
        in_specs += [pl.BlockSpec((None, None, 2, None, DK_B, DV_B), lambda b, h, c: (b, layer, 0, h, 0, 0)),
                     pl.BlockSpec((None, None, 2, None, 1, DK_B), lambda b, h, c: (b, layer, 0, h, 0, 0)),
                     pl.BlockSpec((None, None, 2, None, 1, 1), lambda b, h, c: (b, layer, 0, h, 0, 0))]
        args += list(state)
    hspec_f = pl.BlockSpec((L, DV_B), lambda b, h, c: (b * nc + c, h))
    hspec_b = pl.BlockSpec((L, DV_B), lambda b, h, c: (b * nc + (nc - 1 - c), h))
    out_specs = [hspec_f, hspec_b]
    out_shape = [jax.ShapeDtypeStruct((B * N, H_B * DV_B), F32)] * 2
    if emit_state:
        out_specs += [pl.BlockSpec((None, 2, None, DK_B, DV_B), lambda b, h, c: (b, 0, h, 0, 0)),
                      pl.BlockSpec((None, 2, None, 1, DK_B), lambda b, h, c: (b, 0, h, 0, 0)),
                      pl.BlockSpec((None, 2, None, 1, 1), lambda b, h, c: (b, 0, h, 0, 0))]
        out_shape += [jax.ShapeDtypeStruct((B, 2, H_B, DK_B, DV_B), F32),
                      jax.ShapeDtypeStruct((B, 2, H_B, 1, DK_B), F32),
                      jax.ShapeDtypeStruct((B, 2, H_B, 1, 1), F32)]
    return pl.pallas_call(
        functools.partial(_mlstm_kernel, zero_init=state is None, emit_state=emit_state),
        grid=(B, H_B, nc),
        in_specs=in_specs,
        out_specs=out_specs,
        out_shape=out_shape,
        scratch_shapes=[pltpu.VMEM((2, DK_B, DV_B), F32), pltpu.VMEM((2, 1, DK_B), F32), pltpu.VMEM((2, 1, 1), F32)],
        compiler_params=_cparams(("parallel", "parallel", "arbitrary")),
        name="mlstm_scan",
    )(*args)


def _prefix_sum_rows(tri_bf, x):
    x_hi = x.astype(BF16)
    r1 = x - x_hi.astype(F32)
    x_mid = r1.astype(BF16)
    x_lo = (r1 - x_mid.astype(F32)).astype(BF16)
    return _dot(tri_bf, x_hi) + _dot(tri_bf, x_mid) + _dot(tri_bf, x_lo)


def _gla_chunk(q, k, v, z, s_t, rev):
    L, SB = CHUNK_C, SUB_C
    gc = _log_sigmoid(z) * (1.0 / GATE_TAU_C)
    t_i = lax.broadcasted_iota(jnp.int32, (L, L), 0)
    s_i = lax.broadcasted_iota(jnp.int32, (L, L), 1)
    tri = (s_i >= t_i) if rev else (s_i <= t_i)
    bc = _prefix_sum_rows(jnp.where(tri, 1.0, 0.0).astype(BF16), gc)
    qs = q * (DK_C ** -0.5)
    inter = _dot_nt((qs * jnp.exp(bc)).astype(BF16), s_t.astype(BF16))
    col = lax.broadcasted_iota(jnp.int32, (SB, L), 1)
    blocks = []
    for ti in range(L // SB):
        lo, hi = ti * SB, (ti + 1) * SB
        ref_row = bc[hi - 1:hi] if rev else bc[lo:lo + 1]
        q_blk = qs[lo:hi]
        bc_blk = bc[lo:hi]
        q_dec = (q_blk * jnp.exp(bc_blk - ref_row)).astype(BF16)
        k_dec = (k * jnp.exp(jnp.minimum(ref_row - bc, 0.0))).astype(BF16)
        earlier = (col >= hi) if rev else (col < lo)
        a_blk = jnp.where(earlier, _dot_nt(q_dec, k_dec), 0.0)
        t_abs = lax.broadcasted_iota(jnp.int32, (SB, 1), 0) + lo
        for s in range(lo, hi):
            dec = jnp.exp(jnp.minimum(bc_blk - bc[s:s + 1], 0.0))
            a_col = jnp.sum(q_blk * k[s:s + 1] * dec, axis=-1, keepdims=True)
            ok = (t_abs <= s) if rev else (t_abs >= s)
            a_blk = jnp.where((col == s) & ok, a_col, a_blk)
        blocks.append(a_blk)
    a = jnp.concatenate(blocks, axis=0)
    vb = v.astype(BF16)
    o = inter + _dot(a.astype(BF16), vb)
    b_last = bc[0:1] if rev else bc[L - 1:L]
    k_dec = (k * jnp.exp(b_last - bc)).astype(BF16)
    s_new = jnp.exp(b_last) * s_t + _dot_tn(vb, k_dec)
    return o, s_new


def _gla_kernel(*refs, zero_init, emit_state):
    refs = list(refs)
    qf, kf, vf, sf, qb, kb, vb, sb, wup_ref, bup_ref = refs[:10]
    pos = 10
    if not zero_init:
        s0_ref = refs[pos]
        pos += 1
    of_ref, ob_ref = refs[pos:pos + 2]
    pos += 2
    if emit_state:
        so_ref = refs[pos]
        pos += 1
    st_s = refs[pos]
    c = pl.program_id(2)

    @pl.when(c == 0)
    def _():
        for dr in range(2):
            if zero_init:
                st_s[dr] = jnp.zeros((DV_C, DK_C), F32)
            else:
                st_s[dr] = s0_ref[dr].T

    streams = ((0, qf, kf, vf, sf, of_ref), (1, qb, kb, vb, sb, ob_ref))
    for dr, q_ref, k_ref, v_ref, sm_ref, o_ref in streams:
        z = _dot(sm_ref[...].astype(BF16), wup_ref[dr]) + bup_ref[dr]
        o, s_new = _gla_chunk(q_ref[...], k_ref[...], v_ref[...], z, st_s[dr], rev=bool(dr))
        o_ref[...] = o
        st_s[dr] = s_new

    if emit_state:
        @pl.when(c == pl.num_programs(2) - 1)
        def _():
            for dr in range(2):
                so_ref[dr] = st_s[dr].T


def _gla(big, small, w_up_pad, b_up, row0, B, N, layer, state=None, emit_state=False):
    L = CHUNK_C
    nc = N // L
    r0 = row0 // L
    qc, kc, vc = OFF['c_q'] // DK_C, OFF['c_k'] // DK_C, OFF['c_v'] // DV_C
    fwd = lambda b, h, c: r0 + b * nc + c
    bwd = lambda b, h, c: r0 + b * nc + (nc - 1 - c)
    in_specs, args = [], []
    for rowf in (fwd, bwd):
        for col, width in ((qc, DK_C), (kc, DK_C), (vc, DV_C)):
            in_specs.append(pl.BlockSpec((L, width), lambda b, h, c, rowf=rowf, col=col: (rowf(b, h, c), col + h)))
            args.append(big)
        in_specs.append(pl.BlockSpec((L, D_SMALL), lambda b, h, c, rowf=rowf: (rowf(b, h, c), 0)))
        args.append(small)
    in_specs += [pl.BlockSpec((None, 2, D_SMALL, DK_C), lambda b, h, c: (layer, 0, 0, h)),
                 pl.BlockSpec((None, 2, 1, DK_C), lambda b, h, c: (layer, 0, 0, h))]
    args += [w_up_pad, b_up]
    if state is not None:
        in_specs.append(pl.BlockSpec((None, None, 2, None, DK_C, DV_C), lambda b, h, c: (b, layer, 0, h, 0, 0)))
        args.append(state)
    out_specs = [pl.BlockSpec((L, DV_C), lambda b, h, c: (b * nc + c, h)),
                 pl.BlockSpec((L, DV_C), lambda b, h, c: (b * nc + (nc - 1 - c), h))]
    out_shape = [jax.ShapeDtypeStruct((B * N, H_C * DV_C), F32)] * 2
    if emit_state:
        out_specs.append(pl.BlockSpec((None, 2, None, DK_C, DV_C), lambda b, h, c: (b, 0, h, 0, 0)))
        out_shape.append(jax.ShapeDtypeStruct((B, 2, H_C, DK_C, DV_C), F32))
    return pl.pallas_call(
        functools.partial(_gla_kernel, zero_init=state is None, emit_state=emit_state),
        grid=(B, H_C, nc),
        in_specs=in_specs,
        out_specs=out_specs,
        out_shape=out_shape,
        scratch_shapes=[pltpu.VMEM((2, DV_C, DK_C), F32)],
        compiler_params=_cparams(("parallel", "parallel", "arbitrary")),
        name="gla_scan",
    )(*args)


def _gated_norm_kernel(hf_ref, hb_ref, pre_ref, g_ref, o_ref, *, silu):
    h = hf_ref[...] + hb_ref[...]
    y = h * lax.rsqrt(jnp.mean(h * h, axis=-1, keepdims=True) + EPS) * g_ref[...]
    pre = pre_ref[...]
    gate = jax.nn.sigmoid(pre)
    if silu:
        gate = pre * gate
    o_ref[...] = (y * gate).astype(o_ref.dtype)


def _gated_norm(hf, hb, big, pre_off, row0, w_norm, layer, silu, tm=512):
    T = hf.shape[0]
    hw = DV_B
    blk = pl.BlockSpec((tm, hw), lambda i, h: (i, h))
    return pl.pallas_call(
        functools.partial(_gated_norm_kernel, silu=silu),
        grid=(T // tm, BRANCH_W // hw),
        in_specs=[blk, blk,
                  pl.BlockSpec((tm, hw), lambda i, h: (row0 // tm + i, pre_off // hw + h)),
                  pl.BlockSpec((None, 1, hw), lambda i, h: (layer, 0, h))],
        out_specs=blk,
        out_shape=jax.ShapeDtypeStruct((T, BRANCH_W), BF16),
        compiler_params=_cparams(("parallel", "parallel")),
        name="gated_head_norm",
    )(hf, hb, big, w_norm)


def _modulation_tables(c_ctx, c, w_ada_l, b_ada_l):
    cond = jnp.concatenate([c_ctx[None, :], c], axis=0)
    m = jnp.dot(jax.nn.silu(cond), w_ada_l, precision=lax.Precision.HIGHEST) + b_ada_l
    return m.reshape(N_SEG, 6, 1, D_MODEL)


def _mixers(big, small, l, lam_init, shared, caches):
    (w_lam, w_subln, gate_bias, w_b_norm, w_up_pad, b_up, w_c_norm, w_sink, rope_a, rope_d) = shared
    (ca_k, ca_v, sb_c, sb_n, sb_m, sc_s, cd_k, cd_v) = caches
    g = (small[:, :N_B_IF] + gate_bias[l]).reshape(T_ALL, 2, 2, H_B)
    gate_cols = g.transpose(3, 0, 1, 2).reshape(H_B, T_ALL, 4)
    gate_rows = g.transpose(3, 1, 2, 0).reshape(H_B, 4, T_ALL)

    ya = jnp.concatenate([
        _diff_attention(big, 0, BATCH, SEQ, l, w_lam, w_subln, lam_init),
        _diff_attention(big, T_CTX, DEC_BATCH, DEC_SEQ, l, w_lam, w_subln, lam_init, cache=(ca_k, ca_v),
                        rope_tabs=rope_a)], axis=0)

    hf_c, hb_c, c_out, n_out, m_out = _mlstm(big, gate_cols, gate_rows, 0, BATCH, SEQ, l, emit_state=True)
    hf_l, hb_l = _mlstm(big, gate_cols, gate_rows, T_CTX, DEC_BATCH, DEC_SEQ, l, state=(sb_c, sb_n, sb_m))
    yb = jnp.concatenate([_gated_norm(hf_c, hb_c, big, OFF['b_o'], 0, w_b_norm, l, silu=False),
                          _gated_norm(hf_l, hb_l, big, OFF['b_o'], T_CTX, w_b_norm, l, silu=False)], axis=0)

    of_c, ob_c, s_out = _gla(big, small, w_up_pad, b_up, 0, BATCH, SEQ, l, emit_state=True)
    of_l, ob_l = _gla(big, small, w_up_pad, b_up, T_CTX, DEC_BATCH, DEC_SEQ, l, state=sc_s)
    yc = jnp.concatenate([_gated_norm(of_c, ob_c, big, OFF['c_r'], 0, w_c_norm, l, silu=True),
                          _gated_norm(of_l, ob_l, big, OFF['c_r'], T_CTX, w_c_norm, l, silu=True)], axis=0)

    yd = jnp.concatenate([
        _gqa(big, 0, BATCH, SEQ, l, w_sink, tq=SEQ),
        _gqa(big, T_CTX, DEC_BATCH, DEC_SEQ, l, w_sink, cache=(cd_k, cd_v), rope_tabs=rope_d)], axis=0)

    ctx_rows = big[:T_CTX]
    ctx = (ctx_rows[:, OFF['a_k']:OFF['a_k'] + 1024].reshape(BATCH, SEQ, H_A, 2 * D_A),
           ctx_rows[:, OFF['a_v']:OFF['a_v'] + 1024].reshape(BATCH, SEQ, H_A, 2 * D_A),
           c_out, n_out.reshape(BATCH, 2, H_B, DK_B), m_out.reshape(BATCH, 2, H_B), s_out,
           ctx_rows[:, OFF['d_k']:OFF['d_k'] + 512].reshape(BATCH, SEQ, HKV_D, DH_D),
           ctx_rows[:, OFF['d_v']:OFF['d_v'] + 512].reshape(BATCH, SEQ, HKV_D, DH_D))
    return jnp.stack([ya, yb, yc, yd], axis=0), ctx


def kernel(x_prompt, x_sample, c, cache_a_k, cache_a_v, state_b_C, state_b_n, state_b_m, state_c_S, cache_d_k, cache_d_v, c_ctx, w_ada, b_ada, w_norm1, w_norm2, w_in, w_a_lambda, w_a_subln, b_b_if, w_b_norm, w_c_alpha_up, b_c_alpha, w_c_norm, w_d_sink, w_branch, w_merge_gate, w_out, w_router_group, b_router_group, w_router_expert, b_router_expert, w_exp_in, w_exp_out, w_norm_f):
    rows = DEC_SEQ // GRID_W
    rope_a = _rope_lane_tables(rows, D_A)
    rope_d = _rope_lane_tables(rows, DH_D)

    w_big = jnp.concatenate([w_in[:, :, :OFF_B_IF], w_in[:, :, OFF_C_Q:OFF_C_GLR], w_in[:, :, OFF_D_Q:]],
                            axis=-1).astype(BF16)
    w_small = jnp.concatenate([w_in[:, :, OFF_B_IF:OFF_C_Q], w_in[:, :, OFF_C_GLR:OFF_D_Q],
                               jnp.zeros((DEPTH, D_MODEL, D_SMALL - N_B_IF - N_C_GLR), F32)], axis=-1).astype(BF16)
    w_gate_bf = w_merge_gate.astype(BF16)
    w_branch_bf = w_branch.astype(BF16)
    w_out_bf = w_out.astype(BF16)
    w_exp_in_bf = w_exp_in.astype(BF16)
    w_exp_out_bf = w_exp_out.astype(BF16)

    w_up_pad = jnp.zeros((DEPTH, 2, D_SMALL, H_C * DK_C), F32)
    for d in range(2):
        lo = N_B_IF + d * GATE_RANK_C
        w_up_pad = w_up_pad.at[:, d, lo:lo + GATE_RANK_C, :].set(w_c_alpha_up[:, d])
    w_up_pad = w_up_pad.astype(BF16)

    shared = (w_a_lambda, w_a_subln.reshape(DEPTH, 1, 2 * D_A), b_b_if.reshape(DEPTH, N_B_IF),
              w_b_norm.reshape(DEPTH, 1, H_B * DV_B), w_up_pad, b_c_alpha.reshape(DEPTH, 2, 1, H_C * DK_C),
              w_c_norm.reshape(DEPTH, 1, H_C * DV_C), w_d_sink, rope_a, rope_d)
    caches = (cache_a_k.reshape(DEC_BATCH, DEPTH, PAST_LEN, H_A * 2 * D_A),
              cache_a_v.reshape(DEC_BATCH, DEPTH, PAST_LEN, H_A * 2 * D_A),
              state_b_C, state_b_n.reshape(DEC_BATCH, DEPTH, 2, H_B, 1, DK_B),
              state_b_m.reshape(DEC_BATCH, DEPTH, 2, H_B, 1, 1), state_c_S,
              cache_d_k.reshape(DEC_BATCH, DEPTH, PAST_LEN, HKV_D * DH_D),
              cache_d_v.reshape(DEC_BATCH, DEPTH, PAST_LEN, HKV_D * DH_D))

    x = jnp.concatenate([x_prompt.reshape(T_CTX, D_MODEL), x_sample.reshape(T_LAT, D_MODEL)], axis=0)
    ctx_out = [[] for _ in range(8)]
    for l in range(DEPTH):
        lw = {
            'w_rg': w_router_group[l], 'b_rg': b_router_group[l],
            'w_re': w_router_expert[l], 'b_re': b_router_expert[l],
            'w_exp_in': w_exp_in_bf, 'w_exp_out': w_exp_out_bf,
        }
        lam_init = 0.8 - 0.6 * math.exp(-0.3 * l)
        mods = _modulation_tables(c_ctx, c, w_ada[l], b_ada[l])
        sh1, sc1, g1, sh2, sc2, g2 = [mods[:, i] for i in range(6)]

        u = _norm_mod(x, w_norm1[l][None, :], sh1, sc1)
        big = _mm(u, w_big, l, F32)
        small = _mm(u, w_small, l, F32, tn=D_SMALL)

        ys, ctx = _mixers(big, small, l, lam_init, shared, caches)
        for lst, t in zip(ctx_out, ctx):
            lst.append(t)

        merged = _merge(u, ys, w_gate_bf, w_branch_bf, l)
        x = _mm_residual(merged, w_out_bf, l, x, g1)

        u2 = _norm_mod(x, w_norm2[l][None, :], sh2, sc2)
        ya, yb = _hier_moe(u2, lw, l)
        x = _residual_pair(x, g2, ya, yb)

    y = _norm(x, w_norm_f[None, :])
    y_prompt = y[:T_CTX].reshape(BATCH, SEQ, D_MODEL)
    y_sample = y[T_CTX:].reshape(DEC_BATCH, DEC_SEQ, D_MODEL)
    outs = [jnp.stack(t, axis=1) for t in ctx_out]
    return (y_prompt, y_sample) + tuple(outs)
```

```python
import functools
import math

import jax
import jax.numpy as jnp
from jax import lax
from jax.experimental import pallas as pl
from jax.experimental.pallas import tpu as pltpu

D_MODEL = 4096
BATCH = 32
SEQ = 256
DEPTH = 4
DEC_BATCH = 8
DEC_SEQ = 2048
PAST_LEN = 512
GRID_W = 64
ROPE_BASE = 10000.0
EPS = 1e-6
NEG_INF = -1e30
H_A = 8
D_A = 64
H_B = 4
DK_B = 256
DV_B = 256
CHUNK_B = 128
H_C = 4
DK_C = 128
DV_C = 256
GATE_RANK_C = 16
GATE_TAU_C = 16.0
CHUNK_C = 64
SUB_C = 16
H_D = 8
HKV_D = 4
DH_D = 128
WINDOW_D = 128
N_BRANCH = 4
BRANCH_W = 1024
N_GROUPS = 4
EXPERTS_PER_GROUP = 4
N_EXPERTS = N_GROUPS * EXPERTS_PER_GROUP
TOP_K_INNER = 2
D_EXPERT = 1024

F32 = jnp.float32
BF16 = jnp.bfloat16
LANES = 128

T_CTX = BATCH * SEQ
T_LAT = DEC_BATCH * DEC_SEQ
T_ALL = T_CTX + T_LAT
N_SEG = 1 + DEC_BATCH

BIG_PARTS = (
    ('a_q', 1024), ('a_k', 1024), ('a_v', 1024),
    ('b_q', 1024), ('b_k', 1024), ('b_v', 1024), ('b_o', 1024),
    ('c_q', 512), ('c_k', 512), ('c_v', 1024), ('c_r', 1024),
    ('d_q', 1024), ('d_k', 512), ('d_v', 512),
)
D_BIG = sum(w for _, w in BIG_PARTS)
OFF = {}
_o = 0
for _name, _w in BIG_PARTS:
    OFF[_name] = _o
    _o += _w
N_B_IF = 2 * 2 * H_B
N_C_GLR = 2 * GATE_RANK_C
D_SMALL = LANES
OFF_B_IF = 7 * 1024
OFF_C_Q = OFF_B_IF + N_B_IF
OFF_C_GLR = OFF_C_Q + 512 + 512 + 1024 + 1024
OFF_D_Q = OFF_C_GLR + N_C_GLR

VMEM_LIMIT = 56 * 1024 * 1024


def _cparams(sem):
    return pltpu.CompilerParams(dimension_semantics=sem, vmem_limit_bytes=VMEM_LIMIT)


def _seg_of_tile(i, tm):
    n_ctx = T_CTX // tm
    per_lat = DEC_SEQ // tm
    return jnp.where(i < n_ctx, 0, 1 + (i - n_ctx) // per_lat)


def _dot(a, b):
    return jnp.dot(a, b, preferred_element_type=F32)


def _dot_nt(a, b):
    return lax.dot_general(a, b, (((1,), (1,)), ((), ())), preferred_element_type=F32)


def _dot_tn(a, b):
    return lax.dot_general(a, b, (((0,), (0,)), ((), ())), preferred_element_type=F32)


def _log_sigmoid(x):
    return jnp.minimum(x, 0.0) - jnp.log1p(jnp.exp(-jnp.abs(x)))


def _norm_mod_kernel(x_ref, g_ref, sh_ref, sc_ref, o_ref):
    x = x_ref[...]
    y = x * lax.rsqrt(jnp.mean(x * x, axis=-1, keepdims=True) + EPS) * g_ref[...]
    o_ref[...] = (y * (1.0 + sc_ref[...]) + sh_ref[...]).astype(o_ref.dtype)


def _norm_mod(x, g, shift, scale, tm=256):
    T, D = x.shape
    seg = lambda i: (_seg_of_tile(i, tm), 0, 0)
    return pl.pallas_call(
        _norm_mod_kernel,
        grid=(T // tm,),
        in_specs=[pl.BlockSpec((tm, D), lambda i: (i, 0)),
                  pl.BlockSpec((1, D), lambda i: (0, 0)),
                  pl.BlockSpec((None, 1, D), seg),
                  pl.BlockSpec((None, 1, D), seg)],
        out_specs=pl.BlockSpec((tm, D), lambda i: (i, 0)),
        out_shape=jax.ShapeDtypeStruct((T, D), BF16),
        compiler_params=_cparams(("parallel",)),
        name="norm_mod",
    )(x, g, shift, scale)


def _norm_kernel(x_ref, g_ref, o_ref):
    x = x_ref[...]
    o_ref[...] = x * lax.rsqrt(jnp.mean(x * x, axis=-1, keepdims=True) + EPS) * g_ref[...]


def _norm(x, g, tm=256):
    T, D = x.shape
    return pl.pallas_call(
        _norm_kernel,
        grid=(T // tm,),
        in_specs=[pl.BlockSpec((tm, D), lambda i: (i, 0)),
                  pl.BlockSpec((1, D), lambda i: (0, 0))],
        out_specs=pl.BlockSpec((tm, D), lambda i: (i, 0)),
        out_shape=jax.ShapeDtypeStruct((T, D), F32),
        compiler_params=_cparams(("parallel",)),
        name="final_norm",
    )(x, g)


def _mm_kernel(a_ref, w_ref, o_ref):
    o_ref[...] = _dot(a_ref[...], w_ref[...]).astype(o_ref.dtype)


def _mm(a, w, layer, out_dtype, tm=1024, tn=1024):
    T, K = a.shape
    N = w.shape[-1]
    return pl.pallas_call(
        _mm_kernel,
        grid=(T // tm, N // tn),
        in_specs=[pl.BlockSpec((tm, K), lambda i, j: (i, 0)),
                  pl.BlockSpec((None, K, tn), lambda i, j: (layer, 0, j))],
        out_specs=pl.BlockSpec((tm, tn), lambda i, j: (i, j)),
        out_shape=jax.ShapeDtypeStruct((T, N), out_dtype),
        compiler_params=_cparams(("parallel", "parallel")),
        name="proj_mm",
    )(a, w)


def _mm_res_kernel(a_ref, w_ref, x_ref, g_ref, o_ref):
    o_ref[...] = x_ref[...] + g_ref[...] * _dot(a_ref[...], w_ref[...])


def _mm_residual(a, w, layer, x, gate, tm=1024, tn=512):
    T, K = a.shape
    N = w.shape[-1]
    return pl.pallas_call(
        _mm_res_kernel,
        grid=(T // tm, N // tn),
        in_specs=[pl.BlockSpec((tm, K), lambda i, j: (i, 0)),
                  pl.BlockSpec((None, K, tn), lambda i, j: (layer, 0, j)),
                  pl.BlockSpec((tm, tn), lambda i, j: (i, j)),
                  pl.BlockSpec((None, 1, tn), lambda i, j: (_seg_of_tile(i, tm), 0, j))],
        out_specs=pl.BlockSpec((tm, tn), lambda i, j: (i, j)),
        out_shape=jax.ShapeDtypeStruct((T, N), F32),
        compiler_params=_cparams(("parallel", "parallel")),
        name="out_proj_residual",
    )(a, w, x, gate)


def _merge_kernel(u_ref, wg_ref, y_ref, wb_ref, o_ref, acc_ref):
    b = pl.program_id(2)
    gate = jax.nn.sigmoid(_dot(u_ref[...], wg_ref[...]))
    term = gate * _dot(y_ref[...], wb_ref[...])

    @pl.when(b == 0)
    def _():
        acc_ref[...] = term

    @pl.when(b > 0)
    def _():
        acc_ref[...] += term

    @pl.when(b == N_BRANCH - 1)
    def _():
        o_ref[...] = acc_ref[...].astype(o_ref.dtype)


def _merge(u, ys, w_gate, w_branch, layer, tm=1024, tn=512):
    T, D = u.shape
    nj = D // tn
    return pl.pallas_call(
        _merge_kernel,
        grid=(T // tm, nj, N_BRANCH),
        in_specs=[pl.BlockSpec((tm, D), lambda i, j, b: (i, 0)),
                  pl.BlockSpec((None, D, tn), lambda i, j, b: (layer, 0, b * nj + j)),
                  pl.BlockSpec((None, tm, BRANCH_W), lambda i, j, b: (b, i, 0)),
                  pl.BlockSpec((None, None, BRANCH_W, tn), lambda i, j, b: (layer, b, 0, j))],
        out_specs=pl.BlockSpec((tm, tn), lambda i, j, b: (i, j)),
        out_shape=jax.ShapeDtypeStruct((T, D), BF16),
        scratch_shapes=[pltpu.VMEM((tm, tn), F32)],
        compiler_params=_cparams(("parallel", "parallel", "arbitrary")),
        name="branch_merge",
    )(u, w_gate, ys, w_branch)


MOE_TM = 512
MOE_NA = 2
MOE_NB = 2


def _moe_kernel(te_ref, nt_ref, x_ref, w1_ref, w2_ref, wo_ref, rs_ref, o_ref, act_ref):
    j = pl.program_id(0)
    s = pl.program_id(1)
    used = j < nt_ref[0]
    hc = D_EXPERT // MOE_NA

    @pl.when(jnp.logical_and(used, s < MOE_NA))
    def _():
        x = x_ref[...]
        h1 = _dot(x, w1_ref[...])
        h2 = _dot(x, w2_ref[...])
        act_ref[s] = (h1 * jax.nn.sigmoid(h1) * h2).astype(BF16)

    @pl.when(jnp.logical_and(used, s >= MOE_NA))
    def _():
        y = _dot(act_ref[0], wo_ref[0:hc, :])
        for c in range(1, MOE_NA):
            y += _dot(act_ref[c], wo_ref[c * hc:(c + 1) * hc, :])
        o_ref[...] = (y * rs_ref[...]).astype(o_ref.dtype)

    @pl.when(jnp.logical_and(jnp.logical_not(used), s >= MOE_NA))
    def _():
        o_ref[...] = jnp.zeros_like(o_ref)


def _moe_experts(xs, tile_expert, n_tiles, row_scale, w_in, w_out, layer):
    rows, D = xs.shape
    hc = D_EXPERT // MOE_NA
    cb = D // MOE_NB

    def xrow(j, s, te, nt):
        return (jnp.minimum(j, nt[0] - 1), 0)

    up = lambda s: jnp.minimum(s, MOE_NA - 1)
    down = lambda s: jnp.maximum(s - MOE_NA, 0)
    grid_spec = pltpu.PrefetchScalarGridSpec(
        num_scalar_prefetch=2,
        grid=(rows // MOE_TM, MOE_NA + MOE_NB),
        in_specs=[pl.BlockSpec((MOE_TM, D), xrow),
                  pl.BlockSpec((None, None, D, hc), lambda j, s, te, nt: (layer, te[j], 0, up(s))),
                  pl.BlockSpec((None, None, D, hc), lambda j, s, te, nt: (layer, te[j], 0, up(s) + MOE_NA)),
                  pl.BlockSpec((None, None, D_EXPERT, cb), lambda j, s, te, nt: (layer, te[j], 0, down(s))),
                  pl.BlockSpec((MOE_TM, 1), xrow)],
        out_specs=pl.BlockSpec((MOE_TM, cb), lambda j, s, te, nt: (j, down(s))),
        scratch_shapes=[pltpu.VMEM((MOE_NA, MOE_TM, hc), BF16)],
    )
    return pl.pallas_call(
        _moe_kernel,
        grid_spec=grid_spec,
        out_shape=jax.ShapeDtypeStruct((rows, D), BF16),
        compiler_params=_cparams(("arbitrary", "arbitrary")),
        name="moe_experts",
    )(tile_expert, n_tiles, xs, w_in, w_in, w_out, row_scale)


def _route(u2, w_rg, b_rg, w_re, b_re):
    T = u2.shape[0]
    uf = u2.astype(F32)
    lg = jnp.dot(uf, w_rg, precision=lax.Precision.HIGHEST) + b_rg
    pg = jax.nn.softmax(lg, axis=-1)
    gsel = jnp.argmax(lg, axis=-1)
    pg_sel = jnp.take_along_axis(pg, gsel[:, None], axis=1)
    le = (jnp.dot(uf, w_re, precision=lax.Precision.HIGHEST) + b_re).reshape(T, N_GROUPS, EXPERTS_PER_GROUP)
    le_sel = jnp.take_along_axis(le, gsel[:, None, None], axis=1)[:, 0]
    top_v, top_i = lax.top_k(le_sel, TOP_K_INNER)
    w_top = jax.nn.softmax(top_v, axis=-1) * pg_sel
    eidx = (gsel[:, None] * EXPERTS_PER_GROUP + top_i).astype(jnp.int32)
    return eidx, w_top


def _dispatch(eidx, w_top):
    T = eidx.shape[0]
    flat_e = eidx.reshape(-1)
    onehot = (flat_e[:, None] == jnp.arange(N_EXPERTS, dtype=jnp.int32)[None, :]).astype(jnp.int32)
    csum = jnp.cumsum(onehot, axis=0)
    counts = csum[-1]
    rank = jnp.sum(csum * onehot, axis=1) - 1
    padded = ((counts + MOE_TM - 1) // MOE_TM) * MOE_TM
    pend = jnp.cumsum(padded)
    pstart = pend - padded
    dest = (pstart[flat_e] + rank).astype(jnp.int32)
    tok = jnp.arange(2 * T, dtype=jnp.int32) // TOP_K_INNER
    rows = TOP_K_INNER * T + N_EXPERTS * MOE_TM
    tiles = rows // MOE_TM
    src_tok = jnp.zeros((rows,), jnp.int32).at[dest].set(tok)
    row_scale = jnp.zeros((rows,), F32).at[dest].set(w_top.reshape(-1))
    n_tiles = (pend[-1] // MOE_TM).astype(jnp.int32)
    tile_start = jnp.arange(tiles, dtype=jnp.int32) * MOE_TM
    tile_expert = jnp.searchsorted(pend, tile_start, side='right').astype(jnp.int32)
    last_e = jnp.searchsorted(pend, (n_tiles - 1) * MOE_TM, side='right').astype(jnp.int32)
    tile_expert = jnp.where(jnp.arange(tiles) < n_tiles, tile_expert, last_e)
    return src_tok, row_scale[:, None], tile_expert, n_tiles.reshape(1), dest.reshape(T, TOP_K_INNER)


def _hier_moe(u2, lw, layer):
    eidx, w_top = _route(u2, lw['w_rg'], lw['b_rg'], lw['w_re'], lw['b_re'])
    src_tok, row_scale, tile_expert, n_tiles, dest = _dispatch(eidx, w_top)
    rows_of = lambda a, idx: a.at[idx].get(mode='promise_in_bounds')
    xs = rows_of(u2, src_tok)
    ys = _moe_experts(xs, tile_expert, n_tiles, row_scale, lw['w_exp_in'], lw['w_exp_out'], layer)
    return rows_of(ys, dest[:, 0]), rows_of(ys, dest[:, 1])


def _residual_pair_kernel(x_ref, g_ref, a_ref, b_ref, o_ref):
    o_ref[...] = x_ref[...] + g_ref[...] * (a_ref[...].astype(F32) + b_ref[...].astype(F32))


def _residual_pair(x, gate, ya, yb, tm=256):
    T, D = x.shape
    row = pl.BlockSpec((tm, D), lambda i: (i, 0))
    return pl.pallas_call(
        _residual_pair_kernel,
        grid=(T // tm,),
        in_specs=[row, pl.BlockSpec((None, 1, D), lambda i: (_seg_of_tile(i, tm), 0, 0)), row, row],
        out_specs=row,
        out_shape=jax.ShapeDtypeStruct((T, D), F32),
        compiler_params=_cparams(("parallel",)),
        name="moe_residual",
    )(x, gate, ya, yb)


def _axial_rope_tables(rows, dim):
    row = jnp.repeat(jnp.arange(rows, dtype=F32), GRID_W)
    col = jnp.tile(jnp.arange(GRID_W, dtype=F32), rows)
    n_freq = dim // 4
    inv = ROPE_BASE ** (-jnp.arange(n_freq, dtype=F32) / n_freq)
    ang = jnp.concatenate([row[:, None] * inv[None], col[:, None] * inv[None]], axis=-1)
    return jnp.cos(ang), jnp.sin(ang)


def _rope_lane_tables(rows, dim):
    cos, sin = _axial_rope_tables(rows, dim)
    groups = LANES // dim
    return (jnp.tile(cos, (1, 2 * groups)),
            jnp.tile(jnp.concatenate([-sin, sin], axis=-1), (1, groups)))


def _rope_lanes(x, cos_t, sin_t, dim):
    half = dim // 2
    if dim == LANES:
        swapped = pltpu.roll(x, half, 1)
    else:
        lane = lax.broadcasted_iota(jnp.int32, x.shape, 1)
        swapped = jnp.where((lane & half) == 0, pltpu.roll(x, LANES - half, 1), pltpu.roll(x, half, 1))
    return x * cos_t + swapped * sin_t


ATTN_SUB = 64

def _diff_attn_kernel(*refs, n_lat, n_cache, rope, tq, lam_init):
    if rope:
        (q_ref, k_ref, v_ref, lam_ref, g_ref, ck_ref, cv_ref, cosq_ref, sinq_ref, cosk_ref, sink_ref,
         o_ref, kbuf, vbuf) = refs
    else:
        q_ref, k_ref, v_ref, lam_ref, g_ref, o_ref, kbuf, vbuf = refs

    @pl.when(pl.program_id(2) == 0)
    def _():
        k = k_ref[...]
        if rope:
            k = _rope_lanes(k, cosk_ref[...], sink_ref[...], D_A)
        kbuf[0:n_lat, :] = k.astype(BF16)
        vbuf[0:n_lat, :] = v_ref[...].astype(BF16)
        if n_cache:
            kbuf[n_lat:n_lat + n_cache, :] = ck_ref[...].astype(BF16)
            vbuf[n_lat:n_lat + n_cache, :] = cv_ref[...].astype(BF16)

    q = q_ref[...]
    if rope:
        q = _rope_lanes(q, cosq_ref[...], sinq_ref[...], D_A)
    q = q * (D_A ** -0.5)
    lane = lax.broadcasted_iota(jnp.int32, (ATTN_SUB, 2 * D_A), 1)
    lf = lam_ref[...]
    lam = (jnp.exp(jnp.sum(lf[0:1] * lf[1:2], keepdims=True))
           - jnp.exp(jnp.sum(lf[2:3] * lf[3:4], keepdims=True)) + lam_init)
    for r in range(tq // ATTN_SUB):
        qr = q[r * ATTN_SUB:(r + 1) * ATTN_SUB]
        qs = jnp.concatenate([jnp.where(lane < D_A, qr, 0.0), jnp.where(lane >= D_A, qr, 0.0)], axis=0).astype(BF16)
        s = _dot_nt(qs, kbuf[...])
        e = jnp.exp(s - jnp.max(s, axis=-1, keepdims=True))
        den = jnp.sum(e, axis=-1, keepdims=True)
        pv = _dot(e.astype(BF16), vbuf[...]) / den
        o = pv[0:ATTN_SUB] - lam * pv[ATTN_SUB:2 * ATTN_SUB]
        y = o * lax.rsqrt(jnp.mean(o * o, axis=-1, keepdims=True) + EPS) * g_ref[...]
        o_ref[r * ATTN_SUB:(r + 1) * ATTN_SUB, :] = (y * (1.0 - lam_init)).astype(o_ref.dtype)


def _diff_attention(big, row0, B, N, layer, w_lam, w_subln, lam_init, cache=None, rope_tabs=None, tq=256):
    rope = rope_tabs is not None
    n_cache = cache[0].shape[2] if cache is not None else 0
    nq = N // tq
    hw = 2 * D_A
    qc, kc, vc = OFF['a_q'] // hw, OFF['a_k'] // hw, OFF['a_v'] // hw
    in_specs = [pl.BlockSpec((tq, hw), lambda b, h, i: (row0 // tq + b * nq + i, qc + h)),
                pl.BlockSpec((N, hw), lambda b, h, i: (row0 // N + b, kc + h)),
                pl.BlockSpec((N, hw), lambda b, h, i: (row0 // N + b, vc + h)),
                pl.BlockSpec((None, 4, D_A), lambda b, h, i: (layer, 0, 0)),
                pl.BlockSpec((None, 1, hw), lambda b, h, i: (layer, 0, 0))]
    args = [big, big, big, w_lam, w_subln]
    if rope:
        cspec = pl.BlockSpec((None, None, n_cache, hw), lambda b, h, i: (b, layer, 0, h))
        in_specs += [cspec, cspec,
                     pl.BlockSpec((tq, hw), lambda b, h, i: (i, 0)), pl.BlockSpec((tq, hw), lambda b, h, i: (i, 0)),
                     pl.BlockSpec((N, hw), lambda b, h, i: (0, 0)), pl.BlockSpec((N, hw), lambda b, h, i: (0, 0))]
        args += [cache[0], cache[1], rope_tabs[0], rope_tabs[1], rope_tabs[0], rope_tabs[1]]
    return pl.pallas_call(
        functools.partial(_diff_attn_kernel, n_lat=N, n_cache=n_cache, rope=rope, tq=tq, lam_init=lam_init),
        grid=(B, H_A, nq),
        in_specs=in_specs,
        out_specs=pl.BlockSpec((tq, hw), lambda b, h, i: (b * nq + i, h)),
        out_shape=jax.ShapeDtypeStruct((B * N, BRANCH_W), BF16),
        scratch_shapes=[pltpu.VMEM((N + n_cache, hw), BF16), pltpu.VMEM((N + n_cache, hw), BF16)],
        compiler_params=_cparams(("parallel", "parallel", "arbitrary")),
        name="diff_attention",
    )(*args)


GQA_G = H_D // HKV_D


def _gqa_kernel(*refs, n_lat, n_cache, banded, tq, layer):
    if banded:
        (sink_ref, q_ref, k_ref, v_ref, ck_ref, cv_ref, cosq_ref, sinq_ref, cosk_ref, sink_tab_ref,
         o_ref, kbuf, vbuf) = refs
    else:
        sink_ref, q_ref, k_ref, v_ref, o_ref, kbuf, vbuf = refs
    W = WINDOW_D
    kv = pl.program_id(1)
    qi = pl.program_id(2)

    @pl.when(qi == 0)
    def _():
        k = k_ref[...]
        if banded:
            k = _rope_lanes(k, cosk_ref[...], sink_tab_ref[...], DH_D)
            zeros = jnp.zeros((W, DH_D), BF16)
            for buf, lat, cached in ((kbuf, k, ck_ref), (vbuf, v_ref[...], cv_ref)):
                buf[0:W, :] = zeros
                buf[W:W + n_lat, :] = lat.astype(BF16)
                buf[W + n_lat:2 * W + n_lat, :] = zeros
                buf[2 * W + n_lat:2 * W + n_lat + n_cache, :] = cached[...].astype(BF16)
        else:
            kbuf[...] = k.astype(BF16)
            vbuf[...] = v_ref[...].astype(BF16)

    q2 = q_ref[...]
    heads = [q2[:, g * DH_D:(g + 1) * DH_D] for g in range(GQA_G)]
    if banded:
        heads = [_rope_lanes(qh, cosq_ref[...], sinq_ref[...], DH_D) for qh in heads]
    qs = (jnp.concatenate(heads, axis=0) * (DH_D ** -0.5)).astype(BF16)
    row = lax.broadcasted_iota(jnp.int32, (GQA_G * tq, 1), 0)
    sink = jnp.full((GQA_G * tq, 1), sink_ref[layer, kv * GQA_G], F32)
    for g in range(1, GQA_G):
        sink = jnp.where(row >= g * tq, sink_ref[layer, kv * GQA_G + g], sink)

    if banded:
        start = pl.multiple_of(qi * W, W)
        s_loc = _dot_nt(qs, kbuf[pl.ds(start, 3 * W), :])
        r = lax.broadcasted_iota(jnp.int32, s_loc.shape, 0) & (tq - 1)
        c = lax.broadcasted_iota(jnp.int32, s_loc.shape, 1)
        kpos = qi * W + c - W
        valid = (jnp.abs(c - W - r) <= W) & (kpos >= 0) & (kpos < n_lat)
        s_loc = jnp.where(valid, s_loc, NEG_INF)
        s_ctx = _dot_nt(qs, kbuf[2 * W + n_lat:2 * W + n_lat + n_cache, :])
        m = jnp.maximum(jnp.maximum(jnp.max(s_loc, axis=-1, keepdims=True),
                                    jnp.max(s_ctx, axis=-1, keepdims=True)), sink)
        e_loc = jnp.exp(s_loc - m)
        e_ctx = jnp.exp(s_ctx - m)
        den = (jnp.sum(e_loc, axis=-1, keepdims=True) + jnp.sum(e_ctx, axis=-1, keepdims=True)
               + jnp.exp(sink - m))
        o = (_dot(e_loc.astype(BF16), vbuf[pl.ds(start, 3 * W), :])
             + _dot(e_ctx.astype(BF16), vbuf[2 * W + n_lat:2 * W + n_lat + n_cache, :])) / den
    else:
        s = _dot_nt(qs, kbuf[...])
        m = jnp.maximum(jnp.max(s, axis=-1, keepdims=True), sink)
        e = jnp.exp(s - m)
        den = jnp.sum(e, axis=-1, keepdims=True) + jnp.exp(sink - m)
        o = _dot(e.astype(BF16), vbuf[...]) / den
    o_ref[...] = jnp.concatenate([o[g * tq:(g + 1) * tq] for g in range(GQA_G)], axis=1).astype(o_ref.dtype)


def _gqa(big, row0, B, N, layer, w_sink, cache=None, rope_tabs=None, tq=WINDOW_D):
    banded = cache is not None
    n_cache = cache[0].shape[2] if banded else 0
    nq = N // tq
    qw = GQA_G * DH_D
    qc, kc, vc = OFF['d_q'] // qw, OFF['d_k'] // DH_D, OFF['d_v'] // DH_D
    in_specs = [pl.BlockSpec(memory_space=pltpu.SMEM),
                pl.BlockSpec((tq, qw), lambda b, h, i: (row0 // tq + b * nq + i, qc + h)),
                pl.BlockSpec((N, DH_D), lambda b, h, i: (row0 // N + b, kc + h)),
                pl.BlockSpec((N, DH_D), lambda b, h, i: (row0 // N + b, vc + h))]
    args = [w_sink, big, big, big]
    n_rows = N
    if banded:
        assert tq == WINDOW_D
        cspec = pl.BlockSpec((None, None, n_cache, DH_D), lambda b, h, i: (b, layer, 0, h))
        in_specs += [cspec, cspec,
                     pl.BlockSpec((tq, DH_D), lambda b, h, i: (i, 0)), pl.BlockSpec((tq, DH_D), lambda b, h, i: (i, 0)),
                     pl.BlockSpec((N, DH_D), lambda b, h, i: (0, 0)), pl.BlockSpec((N, DH_D), lambda b, h, i: (0, 0))]
        args += [cache[0], cache[1], rope_tabs[0], rope_tabs[1], rope_tabs[0], rope_tabs[1]]
        n_rows = N + 2 * WINDOW_D + n_cache
    return pl.pallas_call(
        functools.partial(_gqa_kernel, n_lat=N, n_cache=n_cache, banded=banded, tq=tq, layer=layer),
        grid=(B, HKV_D, nq),
        in_specs=in_specs,
        out_specs=pl.BlockSpec((tq, qw), lambda b, h, i: (b * nq + i, h)),
        out_shape=jax.ShapeDtypeStruct((B * N, BRANCH_W), BF16),
        scratch_shapes=[pltpu.VMEM((n_rows, DH_D), BF16), pltpu.VMEM((n_rows, DH_D), BF16)],
        compiler_params=_cparams(("parallel", "parallel", "arbitrary")),
        name="gqa_sink_attention",
    )(*args)


def _mlstm_chunk(q, k, v, i_col, lf_col, i_row, lf_row, C, n_row, m, rev):
    L = CHUNK_B
    t_i = lax.broadcasted_iota(jnp.int32, (L, L), 0)
    s_i = lax.broadcasted_iota(jnp.int32, (L, L), 1)
    tri = (s_i >= t_i) if rev else (s_i <= t_i)
    tri_t = (t_i >= s_i) if rev else (t_i <= s_i)
    b_col = jnp.sum(jnp.where(tri, lf_row, 0.0), axis=1, keepdims=True)
    b_row = jnp.sum(jnp.where(tri_t, lf_col, 0.0), axis=0, keepdims=True)
    a_col = b_col + m
    d = jnp.where(tri, b_col - b_row + i_row, NEG_INF)
    m_t = jnp.maximum(a_col, jnp.max(d, axis=1, keepdims=True))
    w_inter = jnp.exp(a_col - m_t)
    w_intra = jnp.exp(d - m_t)
    qb, kb, vb = q.astype(BF16), k.astype(BF16), v.astype(BF16)
    k_scale = DK_B ** -0.5
    qk = _dot_nt(qb, kb) * k_scale * w_intra
    num = w_inter * _dot(qb, C.astype(BF16)) + _dot(qk.astype(BF16), vb)
    den = w_inter * jnp.sum(q * n_row, axis=1, keepdims=True) + jnp.sum(qk, axis=1, keepdims=True)
    h = num / jnp.maximum(jnp.abs(den), jnp.exp(-m_t))
    last = 0 if rev else L - 1
    b_last = b_col[last:last + 1]
    m_last = m_t[last:last + 1]
    kw = k * (k_scale * jnp.exp(b_last - b_col + i_col - m_last))
    decay = w_inter[last:last + 1]
    C_new = decay * C + _dot_tn(kw.astype(BF16), vb)
    n_new = decay * n_row + jnp.sum(kw, axis=0, keepdims=True)
    return h, C_new, n_new, m_last


def _mlstm_kernel(*refs, zero_init, emit_state):
    refs = list(refs)
    qf, kf, vf, qb, kb, vb, gcf, gcb, grf, grb = refs[:10]
    pos = 10
    if not zero_init:
        c0_ref, n0_ref, m0_ref = refs[pos:pos + 3]
        pos += 3
    hf_ref, hb_ref = refs[pos:pos + 2]
    pos += 2
    if emit_state:
        co_ref, no_ref, mo_ref = refs[pos:pos + 3]
        pos += 3
    c_s, n_s, m_s = refs[pos:pos + 3]
    c = pl.program_id(1)

    @pl.when(c == 0)
    def _():
        if zero_init:
            c_s[...] = jnp.zeros_like(c_s)
            n_s[...] = jnp.zeros_like(n_s)
            m_s[...] = jnp.zeros_like(m_s)
        else:
            c_s[...] = c0_ref[...]
            n_s[...] = n0_ref[...]
            m_s[...] = m0_ref[...]

    streams = ((0, qf, kf, vf, gcf, grf, hf_ref), (1, qb, kb, vb, gcb, grb, hb_ref))
    for dr, q_ref, k_ref, v_ref, gc_ref, gr_ref, h_ref in streams:
        gc = gc_ref[...]
        gr = gr_ref[...]
        lf_c = _log_sigmoid(gc)
        lf_r = _log_sigmoid(gr)
        for hd in range(H_B):
            i_at = dr * 2 * H_B + hd
            f_at = i_at + H_B
            h, c_new, n_new, m_new = _mlstm_chunk(
                q_ref[:, hd * DK_B:(hd + 1) * DK_B], k_ref[:, hd * DK_B:(hd + 1) * DK_B],
                v_ref[:, hd * DV_B:(hd + 1) * DV_B],
                gc[:, i_at:i_at + 1], lf_c[:, f_at:f_at + 1], gr[i_at:i_at + 1, :], lf_r[f_at:f_at + 1, :],
                c_s[dr, hd], n_s[dr, hd], m_s[dr, hd], rev=bool(dr))
            h_ref[:, hd * DV_B:(hd + 1) * DV_B] = h
            c_s[dr, hd] = c_new
            n_s[dr, hd] = n_new
            m_s[dr, hd] = m_new

    if emit_state:
        @pl.when(c == pl.num_programs(1) - 1)
        def _():
            co_ref[...] = c_s[...]
            no_ref[...] = n_s[...]
            mo_ref[...] = m_s[...]


def _mlstm(big, gate_cols, gate_rows, row0, B, N, layer, state=None, emit_state=False):
    L = CHUNK_B
    nc = N // L
    r0 = row0 // L
    qw, vw = H_B * DK_B, H_B * DV_B
    qc, kc, vc = OFF['b_q'] // qw, OFF['b_k'] // qw, OFF['b_v'] // vw
    fwd = lambda b, c: r0 + b * nc + c
    bwd = lambda b, c: r0 + b * nc + (nc - 1 - c)
    in_specs, args = [], []
    for rowf in (fwd, bwd):
        for col, width in ((qc, qw), (kc, qw), (vc, vw)):
            in_specs.append(pl.BlockSpec((L, width), lambda b, c, rowf=rowf, col=col: (rowf(b, c), col)))
            args.append(big)
    for rowf in (fwd, bwd):
        in_specs.append(pl.BlockSpec((L, N_B_IF), lambda b, c, rowf=rowf: (rowf(b, c), 0)))
        args.append(gate_cols)
    for rowf in (fwd, bwd):
        in_specs.append(pl.BlockSpec((N_B_IF, L), lambda b, c, rowf=rowf: (0, rowf(b, c))))
        args.append(gate_rows)
    if state is not None:
        in_specs += [pl.BlockSpec((None, None, 2, H_B, DK_B, DV_B), lambda b, c: (b, layer, 0, 0, 0, 0)),
                     pl.BlockSpec((None, None, 2, H_B, 1, DK_B), lambda b, c: (b, layer, 0, 0, 0, 0)),
                     pl.BlockSpec((None, None, 2, H_B, 1, 1), lambda b, c: (b, layer, 0, 0, 0, 0))]
        args += list(state)
    out_specs = [pl.BlockSpec((L, vw), lambda b, c: (b * nc + c, 0)),
                 pl.BlockSpec((L, vw), lambda b, c: (b * nc + (nc - 1 - c), 0))]
    out_shape = [jax.ShapeDtypeStruct((B * N, vw), F32)] * 2
    if emit_state:
        out_specs += [pl.BlockSpec((None, 2, H_B, DK_B, DV_B), lambda b, c: (b, 0, 0, 0, 0)),
                      pl.BlockSpec((None, 2, H_B, 1, DK_B), lambda b, c: (b, 0, 0, 0, 0)),
                      pl.BlockSpec((None, 2, H_B, 1, 1), lambda b, c: (b, 0, 0, 0, 0))]
        out_shape += [jax.ShapeDtypeStruct((B, 2, H_B, DK_B, DV_B), F32),
                      jax.ShapeDtypeStruct((B, 2, H_B, 1, DK_B), F32),
                      jax.ShapeDtypeStruct((B, 2, H_B, 1, 1), F32)]
    return pl.pallas_call(
        functools.partial(_mlstm_kernel, zero_init=state is None, emit_state=emit_state),
        grid=(B, nc),
        in_specs=in_specs,
        out_specs=out_specs,
        out_shape=out_shape,
        scratch_shapes=[pltpu.VMEM((2, H_B, DK_B, DV_B), F32), pltpu.VMEM((2, H_B, 1, DK_B), F32),
                        pltpu.VMEM((2, H_B, 1, 1), F32)],
        compiler_params=_cparams(("parallel", "arbitrary")),
        name="mlstm_scan",
    )(*args)


def _prefix_sum_rows(tri_bf, x):
    x_hi = x.astype(BF16)
    r1 = x - x_hi.astype(F32)
    x_mid = r1.astype(BF16)
    x_lo = (r1 - x_mid.astype(F32)).astype(BF16)
    return _dot(tri_bf, x_hi) + _dot(tri_bf, x_mid) + _dot(tri_bf, x_lo)


def _gla_chunk(q, k, v, gc, s_t, rev):
    L, SB = CHUNK_C, SUB_C
    t_i = lax.broadcasted_iota(jnp.int32, (L, L), 0)
    s_i = lax.broadcasted_iota(jnp.int32, (L, L), 1)
    tri = (s_i >= t_i) if rev else (s_i <= t_i)
    bc = _prefix_sum_rows(jnp.where(tri, 1.0, 0.0).astype(BF16), gc)
    qs = q * (DK_C ** -0.5)
    inter = _dot_nt((qs * jnp.exp(bc)).astype(BF16), s_t.astype(BF16))
    col = lax.broadcasted_iota(jnp.int32, (SB, L), 1)
    blocks = []
    for ti in range(L // SB):
        lo, hi = ti * SB, (ti + 1) * SB
        ref_row = bc[hi - 1:hi] if rev else bc[lo:lo + 1]
        q_blk = qs[lo:hi]
        bc_blk = bc[lo:hi]
        q_dec = (q_blk * jnp.exp(bc_blk - ref_row)).astype(BF16)
        k_dec = (k * jnp.exp(jnp.minimum(ref_row - bc, 0.0))).astype(BF16)
        earlier = (col >= hi) if rev else (col < lo)
        a_blk = jnp.where(earlier, _dot_nt(q_dec, k_dec), 0.0)
        t_abs = lax.broadcasted_iota(jnp.int32, (SB, 1), 0) + lo
        for s in range(lo, hi):
            dec = jnp.exp(jnp.minimum(bc_blk - bc[s:s + 1], 0.0))
            a_col = jnp.sum(q_blk * k[s:s + 1] * dec, axis=-1, keepdims=True)
            ok = (t_abs <= s) if rev else (t_abs >= s)
            a_blk = jnp.where((col == s) & ok, a_col, a_blk)
        blocks.append(a_blk)
    a = jnp.concatenate(blocks, axis=0)
    vb = v.astype(BF16)
    o = inter + _dot(a.astype(BF16), vb)
    b_last = bc[0:1] if rev else bc[L - 1:L]
    k_dec = (k * jnp.exp(b_last - bc)).astype(BF16)
    s_new = jnp.exp(b_last) * s_t + _dot_tn(vb, k_dec)
    return o, s_new


def _gla_kernel(*refs, zero_init, emit_state):
    refs = list(refs)
    qf, kf, vf, sf, qb, kb, vb, sb, wup_ref, bup_ref = refs[:10]
    pos = 10
    if not zero_init:
        s0_ref = refs[pos]
        pos += 1
    of_ref, ob_ref = refs[pos:pos + 2]
    pos += 2
    if emit_state:
        so_ref = refs[pos]
        pos += 1
    st_s = refs[pos]
    c = pl.program_id(1)

    @pl.when(c == 0)
    def _():
        for dr in range(2):
            for hd in range(H_C):
                if zero_init:
                    st_s[dr, hd] = jnp.zeros((DV_C, DK_C), F32)
                else:
                    st_s[dr, hd] = s0_ref[dr, hd].T

    streams = ((0, qf, kf, vf, sf, of_ref), (1, qb, kb, vb, sb, ob_ref))
    for dr, q_ref, k_ref, v_ref, sm_ref, o_ref in streams:
        z = _dot(sm_ref[...].astype(BF16), wup_ref[dr]) + bup_ref[dr]
        gc = _log_sigmoid(z) * (1.0 / GATE_TAU_C)
        for hd in range(H_C):
            ks = slice(hd * DK_C, (hd + 1) * DK_C)
            vs = slice(hd * DV_C, (hd + 1) * DV_C)
            o, s_new = _gla_chunk(q_ref[:, ks], k_ref[:, ks], v_ref[:, vs], gc[:, ks], st_s[dr, hd], rev=bool(dr))
            o_ref[:, vs] = o
            st_s[dr, hd] = s_new

    if emit_state:
        @pl.when(c == pl.num_programs(1) - 1)
        def _():
            for dr in range(2):
                for hd in range(H_C):
                    so_ref[dr, hd] = st_s[dr, hd].T


def _gla(big, small, w_up_pad, b_up, row0, B, N, layer, state=None, emit_state=False):
    L = CHUNK_C
    nc = N // L
    r0 = row0 // L
    qw, vw = H_C * DK_C, H_C * DV_C
    qc, kc, vc = OFF['c_q'] // qw, OFF['c_k'] // qw, OFF['c_v'] // vw
    fwd = lambda b, c: r0 + b * nc + c
    bwd = lambda b, c: r0 + b * nc + (nc - 1 - c)
    in_specs, args = [], []
    for rowf in (fwd, bwd):
        for col, width in ((qc, qw), (kc, qw), (vc, vw)):
            in_specs.append(pl.BlockSpec((L, width), lambda b, c, rowf=rowf, col=col: (rowf(b, c), col)))
            args.append(big)
        in_specs.append(pl.BlockSpec((L, D_SMALL), lambda b, c, rowf=rowf: (rowf(b, c), 0)))
        args.append(small)
    in_specs += [pl.BlockSpec((None, 2, D_SMALL, qw), lambda b, c: (layer, 0, 0, 0)),
                 pl.BlockSpec((None, 2, 1, qw), lambda b, c: (layer, 0, 0, 0))]
    args += [w_up_pad, b_up]
    if state is not None:
        in_specs.append(pl.BlockSpec((None, None, 2, H_C, DK_C, DV_C), lambda b, c: (b, layer, 0, 0, 0, 0)))
        args.append(state)
    out_specs = [pl.BlockSpec((L, vw), lambda b, c: (b * nc + c, 0)),
                 pl.BlockSpec((L, vw), lambda b, c: (b * nc + (nc - 1 - c), 0))]
    out_shape = [jax.ShapeDtypeStruct((B * N, vw), F32)] * 2
    if emit_state:
        out_specs.append(pl.BlockSpec((None, 2, H_C, DK_C, DV_C), lambda b, c: (b, 0, 0, 0, 0)))
        out_shape.append(jax.ShapeDtypeStruct((B, 2, H_C, DK_C, DV_C), F32))
    return pl.pallas_call(
        functools.partial(_gla_kernel, zero_init=state is None, emit_state=emit_state),
        grid=(B, nc),
        in_specs=in_specs,
        out_specs=out_specs,
        out_shape=out_shape,
        scratch_shapes=[pltpu.VMEM((2, H_C, DV_C, DK_C), F32)],
        compiler_params=_cparams(("parallel", "arbitrary")),
        name="gla_scan",
    )(*args)


def _gated_norm_kernel(hf_ref, hb_ref, pre_ref, g_ref, o_ref, *, silu):
    h = hf_ref[...] + hb_ref[...]
    y = h * lax.rsqrt(jnp.mean(h * h, axis=-1, keepdims=True) + EPS) * g_ref[...]
    pre = pre_ref[...]
    gate = jax.nn.sigmoid(pre)
    if silu:
        gate = pre * gate
    o_ref[...] = (y * gate).astype(o_ref.dtype)


def _gated_norm(hf, hb, big, pre_off, row0, w_norm, layer, silu, tm=512):
    T = hf.shape[0]
    hw = DV_B
    blk = pl.BlockSpec((tm, hw), lambda i, h: (i, h))
    return pl.pallas_call(
        functools.partial(_gated_norm_kernel, silu=silu),
        grid=(T // tm, BRANCH_W // hw),
        in_specs=[blk, blk,
                  pl.BlockSpec((tm, hw), lambda i, h: (row0 // tm + i, pre_off // hw + h)),
                  pl.BlockSpec((None, 1, hw), lambda i, h: (layer, 0, h))],
        out_specs=blk,
        out_shape=jax.ShapeDtypeStruct((T, BRANCH_W), BF16),
        compiler_params=_cparams(("parallel", "parallel")),
        name="gated_head_norm",
    )(hf, hb, big, w_norm)


def _modulation_tables(c_ctx, c, w_ada_l, b_ada_l):
    cond = jnp.concatenate([c_ctx[None, :], c], axis=0)
    m = jnp.dot(jax.nn.silu(cond), w_ada_l, precision=lax.Precision.HIGHEST) + b_ada_l
    return m.reshape(N_SEG, 6, 1, D_MODEL)


def _mixers(big, small, l, lam_init, shared, caches):
    (w_lam, w_subln, gate_bias, w_b_norm, w_up_pad, b_up, w_c_norm, w_sink, rope_a, rope_d) = shared
    (ca_k, ca_v, sb_c, sb_n, sb_m, sc_s, cd_k, cd_v) = caches
    gate_cols = small[:, :N_B_IF] + gate_bias[l]
    gate_rows = gate_cols.T

    ya = jnp.concatenate([
        _diff_attention(big, 0, BATCH, SEQ, l, w_lam, w_subln, lam_init),
        _diff_attention(big, T_CTX, DEC_BATCH, DEC_SEQ, l, w_lam, w_subln, lam_init, cache=(ca_k, ca_v),
                        rope_tabs=rope_a)], axis=0)

    hf_c, hb_c, c_out, n_out, m_out = _mlstm(big, gate_cols, gate_rows, 0, BATCH, SEQ, l, emit_state=True)
    hf_l, hb_l = _mlstm(big, gate_cols, gate_rows, T_CTX, DEC_BATCH, DEC_SEQ, l, state=(sb_c, sb_n, sb_m))
    yb = jnp.concatenate([_gated_norm(hf_c, hb_c, big, OFF['b_o'], 0, w_b_norm, l, silu=False),
                          _gated_norm(hf_l, hb_l, big, OFF['b_o'], T_CTX, w_b_norm, l, silu=False)], axis=0)

    of_c, ob_c, s_out = _gla(big, small, w_up_pad, b_up, 0, BATCH, SEQ, l, emit_state=True)
    of_l, ob_l = _gla(big, small, w_up_pad, b_up, T_CTX, DEC_BATCH, DEC_SEQ, l, state=sc_s)
    yc = jnp.concatenate([_gated_norm(of_c, ob_c, big, OFF['c_r'], 0, w_c_norm, l, silu=True),
                          _gated_norm(of_l, ob_l, big, OFF['c_r'], T_CTX, w_c_norm, l, silu=True)], axis=0)

    yd = jnp.concatenate([
        _gqa(big, 0, BATCH, SEQ, l, w_sink, tq=SEQ),
        _gqa(big, T_CTX, DEC_BATCH, DEC_SEQ, l, w_sink, cache=(cd_k, cd_v), rope_tabs=rope_d)], axis=0)

    ctx_rows = big[:T_CTX]
    ctx = (ctx_rows[:, OFF['a_k']:OFF['a_k'] + 1024].reshape(BATCH, SEQ, H_A, 2 * D_A),
           ctx_rows[:, OFF['a_v']:OFF['a_v'] + 1024].reshape(BATCH, SEQ, H_A, 2 * D_A),
           c_out, n_out.reshape(BATCH, 2, H_B, DK_B), m_out.reshape(BATCH, 2, H_B), s_out,
           ctx_rows[:, OFF['d_k']:OFF['d_k'] + 512].reshape(BATCH, SEQ, HKV_D, DH_D),
           ctx_rows[:, OFF['d_v']:OFF['d_v'] + 512].reshape(BATCH, SEQ, HKV_D, DH_D))
    return jnp.stack([ya, yb, yc, yd], axis=0), ctx


def kernel(x_prompt, x_sample, c, cache_a_k, cache_a_v, state_b_C, state_b_n, state_b_m, state_c_S, cache_d_k, cache_d_v, c_ctx, w_ada, b_ada, w_norm1, w_norm2, w_in, w_a_lambda, w_a_subln, b_b_if, w_b_norm, w_c_alpha_up, b_c_alpha, w_c_norm, w_d_sink, w_branch, w_merge_gate, w_out, w_router_group, b_router_group, w_router_expert, b_router_expert, w_exp_in, w_exp_out, w_norm_f):
    rows = DEC_SEQ // GRID_W
    rope_a = _rope_lane_tables(rows, D_A)
    rope_d = _rope_lane_tables(rows, DH_D)

    w_big = jnp.concatenate([w_in[:, :, :OFF_B_IF], w_in[:, :, OFF_C_Q:OFF_C_GLR], w_in[:, :, OFF_D_Q:]],
                            axis=-1).astype(BF16)
    w_small = jnp.concatenate([w_in[:, :, OFF_B_IF:OFF_C_Q], w_in[:, :, OFF_C_GLR:OFF_D_Q],
                               jnp.zeros((DEPTH, D_MODEL, D_SMALL - N_B_IF - N_C_GLR), F32)], axis=-1).astype(BF16)
    w_gate_bf = w_merge_gate.astype(BF16)
    w_branch_bf = w_branch.astype(BF16)
    w_out_bf = w_out.astype(BF16)
    w_exp_in_bf = w_exp_in.astype(BF16)
    w_exp_out_bf = w_exp_out.astype(BF16)

    w_up_pad = jnp.zeros((DEPTH, 2, D_SMALL, H_C * DK_C), F32)
    for d in range(2):
        lo = N_B_IF + d * GATE_RANK_C
        w_up_pad = w_up_pad.at[:, d, lo:lo + GATE_RANK_C, :].set(w_c_alpha_up[:, d])
    w_up_pad = w_up_pad.astype(BF16)

    shared = (w_a_lambda, w_a_subln.reshape(DEPTH, 1, 2 * D_A), b_b_if.reshape(DEPTH, N_B_IF),
              w_b_norm.reshape(DEPTH, 1, H_B * DV_B), w_up_pad, b_c_alpha.reshape(DEPTH, 2, 1, H_C * DK_C),
              w_c_norm.reshape(DEPTH, 1, H_C * DV_C), w_d_sink, rope_a, rope_d)
    caches = (cache_a_k.reshape(DEC_BATCH, DEPTH, PAST_LEN, H_A * 2 * D_A),
              cache_a_v.reshape(DEC_BATCH, DEPTH, PAST_LEN, H_A * 2 * D_A),
              state_b_C, state_b_n.reshape(DEC_BATCH, DEPTH, 2, H_B, 1, DK_B),
              state_b_m.reshape(DEC_BATCH, DEPTH, 2, H_B, 1, 1), state_c_S,
              cache_d_k.reshape(DEC_BATCH, DEPTH, PAST_LEN, HKV_D * DH_D),
              cache_d_v.reshape(DEC_BATCH, DEPTH, PAST_LEN, HKV_D * DH_D))

    x = jnp.concatenate([x_prompt.reshape(T_CTX, D_MODEL), x_sample.reshape(T_LAT, D_MODEL)], axis=0)
    ctx_out = [[] for _ in range(8)]
    for l in range(DEPTH):
        lw = {
            'w_rg': w_router_group[l], 'b_rg': b_router_group[l],
            'w_re': w_router_expert[l], 'b_re': b_router_expert[l],
            'w_exp_in': w_exp_in_bf, 'w_exp_out': w_exp_out_bf,
        }
        lam_init = 0.8 - 0.6 * math.exp(-0.3 * l)
        mods = _modulation_tables(c_ctx, c, w_ada[l], b_ada[l])
        sh1, sc1, g1, sh2, sc2, g2 = [mods[:, i] for i in range(6)]

        u = _norm_mod(x, w_norm1[l][None, :], sh1, sc1)
        big = _mm(u, w_big, l, F32)
        small = _mm(u, w_small, l, F32, tn=D_SMALL)

        ys, ctx = _mixers(big, small, l, lam_init, shared, caches)
        for lst, t in zip(ctx_out, ctx):
            lst.append(t)

        merged = _merge(u, ys, w_gate_bf, w_branch_bf, l)
        x = _mm_residual(merged, w_out_bf, l, x, g1)

        u2 = _norm_mod(x, w_norm2[l][None, :], sh2, sc2)
        ya, yb = _hier_moe(u2, lw, l)
        x = _residual_pair(x, g2, ya, yb)

    y = _norm(x, w_norm_f[None, :])
    y_prompt = y[:T_CTX].reshape(BATCH, SEQ, D_MODEL)
    y_sample = y[T_CTX:].reshape(DEC_BATCH, DEC_SEQ, D_MODEL)
    outs = [jnp.stack(t, axis=1) for t in ctx_out]
    return (y_prompt, y_sample) + tuple(outs)
```

```python
import functools
import math

import jax
import jax.numpy as jnp
from jax import lax
from jax.experimental import pallas as pl
from jax.experimental.pallas import tpu as pltpu

D_MODEL = 4096
BATCH = 32
SEQ = 256
DEPTH = 4
DEC_BATCH = 8
DEC_SEQ = 2048
PAST_LEN = 512
GRID_W = 64
ROPE_BASE = 10000.0
EPS = 1e-6
NEG_INF = -1e30
H_A = 8
D_A = 64
H_B = 4
DK_B = 256
DV_B = 256
CHUNK_B = 128
H_C = 4
DK_C = 128
DV_C = 256
GATE_RANK_C = 16
GATE_TAU_C = 16.0
CHUNK_C = 64
SUB_C = 16
H_D = 8
HKV_D = 4
DH_D = 128
WINDOW_D = 128
N_BRANCH = 4
BRANCH_W = 1024
N_GROUPS = 4
EXPERTS_PER_GROUP = 4
N_EXPERTS = N_GROUPS * EXPERTS_PER_GROUP
TOP_K_INNER = 2
D_EXPERT = 1024

F32 = jnp.float32
BF16 = jnp.bfloat16
LANES = 128

T_CTX = BATCH * SEQ
T_LAT = DEC_BATCH * DEC_SEQ
T_ALL = T_CTX + T_LAT
N_SEG = 1 + DEC_BATCH

BIG_PARTS = (
    ('a_q', 1024), ('a_k', 1024), ('a_v', 1024),
    ('b_q', 1024), ('b_k', 1024), ('b_v', 1024), ('b_o', 1024),
    ('c_q', 512), ('c_k', 512), ('c_v', 1024), ('c_r', 1024),
    ('d_q', 1024), ('d_k', 512), ('d_v', 512),
)
D_BIG = sum(w for _, w in BIG_PARTS)
OFF = {}
_o = 0
for _name, _w in BIG_PARTS:
    OFF[_name] = _o
    _o += _w
N_B_IF = 2 * 2 * H_B
N_C_GLR = 2 * GATE_RANK_C
D_SMALL = LANES
OFF_B_IF = 7 * 1024
OFF_C_Q = OFF_B_IF + N_B_IF
OFF_C_GLR = OFF_C_Q + 512 + 512 + 1024 + 1024
OFF_D_Q = OFF_C_GLR + N_C_GLR

VMEM_LIMIT = 56 * 1024 * 1024


def _cparams(sem):
    return pltpu.CompilerParams(dimension_semantics=sem, vmem_limit_bytes=VMEM_LIMIT)


def _seg_of_tile(i, tm):
    n_ctx = T_CTX // tm
    per_lat = DEC_SEQ // tm
    return jnp.where(i < n_ctx, 0, 1 + (i - n_ctx) // per_lat)


def _dot(a, b):
    return jnp.dot(a, b, preferred_element_type=F32)


def _dot_nt(a, b):
    return lax.dot_general(a, b, (((1,), (1,)), ((), ())), preferred_element_type=F32)


def _dot_tn(a, b):
    return lax.dot_general(a, b, (((0,), (0,)), ((), ())), preferred_element_type=F32)


def _log_sigmoid(x):
    return jnp.minimum(x, 0.0) - jnp.log1p(jnp.exp(-jnp.abs(x)))


def _norm_mod_kernel(x_ref, g_ref, sh_ref, sc_ref, o_ref):
    x = x_ref[...]
    y = x * lax.rsqrt(jnp.mean(x * x, axis=-1, keepdims=True) + EPS) * g_ref[...]
    o_ref[...] = (y * (1.0 + sc_ref[...]) + sh_ref[...]).astype(o_ref.dtype)


def _norm_mod(x, g, shift, scale, tm=256):
    T, D = x.shape
    seg = lambda i: (_seg_of_tile(i, tm), 0, 0)
    return pl.pallas_call(
        _norm_mod_kernel,
        grid=(T // tm,),
        in_specs=[pl.BlockSpec((tm, D), lambda i: (i, 0)),
                  pl.BlockSpec((1, D), lambda i: (0, 0)),
                  pl.BlockSpec((None, 1, D), seg),
                  pl.BlockSpec((None, 1, D), seg)],
        out_specs=pl.BlockSpec((tm, D), lambda i: (i, 0)),
        out_shape=jax.ShapeDtypeStruct((T, D), BF16),
        compiler_params=_cparams(("parallel",)),
        name="norm_mod",
    )(x, g, shift, scale)


def _norm_kernel(x_ref, g_ref, o_ref):
    x = x_ref[...]
    o_ref[...] = x * lax.rsqrt(jnp.mean(x * x, axis=-1, keepdims=True) + EPS) * g_ref[...]


def _norm(x, g, tm=256):
    T, D = x.shape
    return pl.pallas_call(
        _norm_kernel,
        grid=(T // tm,),
        in_specs=[pl.BlockSpec((tm, D), lambda i: (i, 0)),
                  pl.BlockSpec((1, D), lambda i: (0, 0))],
        out_specs=pl.BlockSpec((tm, D), lambda i: (i, 0)),
        out_shape=jax.ShapeDtypeStruct((T, D), F32),
        compiler_params=_cparams(("parallel",)),
        name="final_norm",
    )(x, g)


def _mm_kernel(a_ref, w_ref, o_ref):
    o_ref[...] = _dot(a_ref[...], w_ref[...]).astype(o_ref.dtype)


def _mm(a, w, layer, out_dtype, tm=1024, tn=1024):
    T, K = a.shape
    N = w.shape[-1]
    return pl.pallas_call(
        _mm_kernel,
        grid=(T // tm, N // tn),
        in_specs=[pl.BlockSpec((tm, K), lambda i, j: (i, 0)),
                  pl.BlockSpec((None, K, tn), lambda i, j: (layer, 0, j))],
        out_specs=pl.BlockSpec((tm, tn), lambda i, j: (i, j)),
        out_shape=jax.ShapeDtypeStruct((T, N), out_dtype),
        compiler_params=_cparams(("parallel", "parallel")),
        name="proj_mm",
    )(a, w)


def _mm_res_kernel(a_ref, w_ref, x_ref, g_ref, o_ref):
    o_ref[...] = x_ref[...] + g_ref[...] * _dot(a_ref[...], w_ref[...])


def _mm_residual(a, w, layer, x, gate, tm=1024, tn=512):
    T, K = a.shape
    N = w.shape[-1]
    return pl.pallas_call(
        _mm_res_kernel,
        grid=(T // tm, N // tn),
        in_specs=[pl.BlockSpec((tm, K), lambda i, j: (i, 0)),
                  pl.BlockSpec((None, K, tn), lambda i, j: (layer, 0, j)),
                  pl.BlockSpec((tm, tn), lambda i, j: (i, j)),
                  pl.BlockSpec((None, 1, tn), lambda i, j: (_seg_of_tile(i, tm), 0, j))],
        out_specs=pl.BlockSpec((tm, tn), lambda i, j: (i, j)),
        out_shape=jax.ShapeDtypeStruct((T, N), F32),
        compiler_params=_cparams(("parallel", "parallel")),
        name="out_proj_residual",
    )(a, w, x, gate)


def _merge_kernel(u_ref, wg_ref, y_ref, wb_ref, o_ref, acc_ref):
    b = pl.program_id(2)
    gate = jax.nn.sigmoid(_dot(u_ref[...], wg_ref[...]))
    term = gate * _dot(y_ref[...], wb_ref[...])

    @pl.when(b == 0)
    def _():
        acc_ref[...] = term

    @pl.when(b > 0)
    def _():
        acc_ref[...] += term

    @pl.when(b == N_BRANCH - 1)
    def _():
        o_ref[...] = acc_ref[...].astype(o_ref.dtype)


def _merge(u, ys, w_gate, w_branch, layer, tm=1024, tn=512):
    T, D = u.shape
    nj = D // tn
    return pl.pallas_call(
        _merge_kernel,
        grid=(T // tm, nj, N_BRANCH),
        in_specs=[pl.BlockSpec((tm, D), lambda i, j, b: (i, 0)),
                  pl.BlockSpec((None, D, tn), lambda i, j, b: (layer, 0, b * nj + j)),
                  pl.BlockSpec((None, tm, BRANCH_W), lambda i, j, b: (b, i, 0)),
                  pl.BlockSpec((None, None, BRANCH_W, tn), lambda i, j, b: (layer, b, 0, j))],
        out_specs=pl.BlockSpec((tm, tn), lambda i, j, b: (i, j)),
        out_shape=jax.ShapeDtypeStruct((T, D), BF16),
        scratch_shapes=[pltpu.VMEM((tm, tn), F32)],
        compiler_params=_cparams(("parallel", "parallel", "arbitrary")),
        name="branch_merge",
    )(u, w_gate, ys, w_branch)


MOE_TM = 512
MOE_NA = 2
MOE_NB = 2


def _moe_kernel(te_ref, nt_ref, x_ref, w1_ref, w2_ref, wo_ref, rs_ref, o_ref, act_ref):
    j = pl.program_id(0)
    s = pl.program_id(1)
    used = j < nt_ref[0]
    hc = D_EXPERT // MOE_NA

    @pl.when(jnp.logical_and(used, s < MOE_NA))
    def _():
        x = x_ref[...]
        h1 = _dot(x, w1_ref[...])
        h2 = _dot(x, w2_ref[...])
        act_ref[s] = (h1 * jax.nn.sigmoid(h1) * h2).astype(BF16)

    @pl.when(jnp.logical_and(used, s >= MOE_NA))
    def _():
        y = _dot(act_ref[0], wo_ref[0:hc, :])
        for c in range(1, MOE_NA):
            y += _dot(act_ref[c], wo_ref[c * hc:(c + 1) * hc, :])
        o_ref[...] = (y * rs_ref[...]).astype(o_ref.dtype)

    @pl.when(jnp.logical_and(jnp.logical_not(used), s >= MOE_NA))
    def _():
        o_ref[...] = jnp.zeros_like(o_ref)


def _moe_weight_layout(w_in, w_out):
    D = w_in.shape[-2]
    return (_cast_column_blocks(w_in, D_EXPERT // MOE_NA), _cast_column_blocks(w_out, D // MOE_NB))


def _cast_blocks_kernel(x_ref, o_ref):
    o_ref[...] = x_ref[...].astype(o_ref.dtype)


def _cast_column_blocks(w, width):
    L, E, R, C = w.shape
    return pl.pallas_call(
        _cast_blocks_kernel,
        grid=(L, E, C // width),
        in_specs=[pl.BlockSpec((None, None, R, width), lambda l, e, c: (l, e, 0, c))],
        out_specs=pl.BlockSpec((None, None, None, R, width), lambda l, e, c: (l, e, c, 0, 0)),
        out_shape=jax.ShapeDtypeStruct((L, E, C // width, R, width), BF16),
        compiler_params=_cparams(("parallel", "parallel", "parallel")),
        name="cast_column_blocks",
    )(w)


def _moe_experts(xs, tile_expert, n_tiles, row_scale, w_in, w_out, layer):
    rows, D = xs.shape
    hc = D_EXPERT // MOE_NA
    cb = D // MOE_NB

    def xrow(j, s, te, nt):
        return (jnp.minimum(j, nt[0] - 1), 0)

    up = lambda s: jnp.minimum(s, MOE_NA - 1)
    down = lambda s: jnp.maximum(s - MOE_NA, 0)
    grid_spec = pltpu.PrefetchScalarGridSpec(
        num_scalar_prefetch=2,
        grid=(rows // MOE_TM, MOE_NA + MOE_NB),
        in_specs=[pl.BlockSpec((MOE_TM, D), xrow),
                  pl.BlockSpec((None, None, None, D, hc), lambda j, s, te, nt: (layer, te[j], up(s), 0, 0)),
                  pl.BlockSpec((None, None, None, D, hc), lambda j, s, te, nt: (layer, te[j], up(s) + MOE_NA, 0, 0)),
                  pl.BlockSpec((None, None, None, D_EXPERT, cb), lambda j, s, te, nt: (layer, te[j], down(s), 0, 0)),
                  pl.BlockSpec((MOE_TM, 1), xrow)],
        out_specs=pl.BlockSpec((MOE_TM, cb), lambda j, s, te, nt: (j, down(s))),
        scratch_shapes=[pltpu.VMEM((MOE_NA, MOE_TM, hc), BF16)],
    )
    return pl.pallas_call(
        _moe_kernel,
        grid_spec=grid_spec,
        out_shape=jax.ShapeDtypeStruct((rows, D), BF16),
        compiler_params=_cparams(("arbitrary", "arbitrary")),
        name="moe_experts",
    )(tile_expert, n_tiles, xs, w_in, w_in, w_out, row_scale)


def _route(u2, w_rg, b_rg, w_re, b_re):
    T = u2.shape[0]
    uf = u2.astype(F32)
    lg = jnp.dot(uf, w_rg, precision=lax.Precision.HIGHEST) + b_rg
    pg = jax.nn.softmax(lg, axis=-1)
    gsel = jnp.argmax(lg, axis=-1)
    pg_sel = jnp.take_along_axis(pg, gsel[:, None], axis=1)
    le = (jnp.dot(uf, w_re, precision=lax.Precision.HIGHEST) + b_re).reshape(T, N_GROUPS, EXPERTS_PER_GROUP)
    le_sel = jnp.take_along_axis(le, gsel[:, None, None], axis=1)[:, 0]
    top_v, top_i = lax.top_k(le_sel, TOP_K_INNER)
    w_top = jax.nn.softmax(top_v, axis=-1) * pg_sel
    eidx = (gsel[:, None] * EXPERTS_PER_GROUP + top_i).astype(jnp.int32)
    return eidx, w_top


def _dispatch(eidx, w_top):
    T = eidx.shape[0]
    flat_e = eidx.reshape(-1)
    onehot = (flat_e[:, None] == jnp.arange(N_EXPERTS, dtype=jnp.int32)[None, :]).astype(jnp.int32)
    csum = jnp.cumsum(onehot, axis=0)
    counts = csum[-1]
    rank = jnp.sum(csum * onehot, axis=1) - 1
    padded = ((counts + MOE_TM - 1) // MOE_TM) * MOE_TM
    pend = jnp.cumsum(padded)
    pstart = pend - padded
    dest = (pstart[flat_e] + rank).astype(jnp.int32)
    tok = jnp.arange(2 * T, dtype=jnp.int32) // TOP_K_INNER
    rows = TOP_K_INNER * T + N_EXPERTS * MOE_TM
    tiles = rows // MOE_TM
    src_tok = jnp.zeros((rows,), jnp.int32).at[dest].set(tok)
    row_scale = jnp.zeros((rows,), F32).at[dest].set(w_top.reshape(-1))
    n_tiles = (pend[-1] // MOE_TM).astype(jnp.int32)
    tile_start = jnp.arange(tiles, dtype=jnp.int32) * MOE_TM
    tile_expert = jnp.searchsorted(pend, tile_start, side='right').astype(jnp.int32)
    last_e = jnp.searchsorted(pend, (n_tiles - 1) * MOE_TM, side='right').astype(jnp.int32)
    tile_expert = jnp.where(jnp.arange(tiles) < n_tiles, tile_expert, last_e)
    return src_tok, row_scale[:, None], tile_expert, n_tiles.reshape(1), dest.reshape(T, TOP_K_INNER)


def _hier_moe(u2, lw, layer):
    eidx, w_top = _route(u2, lw['w_rg'], lw['b_rg'], lw['w_re'], lw['b_re'])
    src_tok, row_scale, tile_expert, n_tiles, dest = _dispatch(eidx, w_top)
    rows_of = lambda a, idx: a.at[idx].get(mode='promise_in_bounds')
    xs = rows_of(u2, src_tok)
    ys = _moe_experts(xs, tile_expert, n_tiles, row_scale, lw['w_exp_in'], lw['w_exp_out'], layer)
    return rows_of(ys, dest[:, 0]), rows_of(ys, dest[:, 1])


def _residual_pair_kernel(x_ref, g_ref, a_ref, b_ref, o_ref):
    o_ref[...] = x_ref[...] + g_ref[...] * (a_ref[...].astype(F32) + b_ref[...].astype(F32))


def _residual_pair(x, gate, ya, yb, tm=256):
    T, D = x.shape
    row = pl.BlockSpec((tm, D), lambda i: (i, 0))
    return pl.pallas_call(
        _residual_pair_kernel,
        grid=(T // tm,),
        in_specs=[row, pl.BlockSpec((None, 1, D), lambda i: (_seg_of_tile(i, tm), 0, 0)), row, row],
        out_specs=row,
        out_shape=jax.ShapeDtypeStruct((T, D), F32),
        compiler_params=_cparams(("parallel",)),
        name="moe_residual",
    )(x, gate, ya, yb)


def _axial_rope_tables(rows, dim):
    row = jnp.repeat(jnp.arange(rows, dtype=F32), GRID_W)
    col = jnp.tile(jnp.arange(GRID_W, dtype=F32), rows)
    n_freq = dim // 4
    inv = ROPE_BASE ** (-jnp.arange(n_freq, dtype=F32) / n_freq)
    ang = jnp.concatenate([row[:, None] * inv[None], col[:, None] * inv[None]], axis=-1)
    return jnp.cos(ang), jnp.sin(ang)


def _rope_lane_tables(rows, dim):
    cos, sin = _axial_rope_tables(rows, dim)
    groups = LANES // dim
    return (jnp.tile(cos, (1, 2 * groups)),
            jnp.tile(jnp.concatenate([-sin, sin], axis=-1), (1, groups)))


def _rope_lanes(x, cos_t, sin_t, dim):
    half = dim // 2
    if dim == LANES:
        swapped = pltpu.roll(x, half, 1)
    else:
        lane = lax.broadcasted_iota(jnp.int32, x.shape, 1)
        swapped = jnp.where((lane & half) == 0, pltpu.roll(x, LANES - half, 1), pltpu.roll(x, half, 1))
    return x * cos_t + swapped * sin_t


ATTN_SUB = 64

def _branch_placement(ys, branch, row_block0, rows, block, local_index):
    if ys is None:
        return (pl.BlockSpec(block, local_index), jax.ShapeDtypeStruct((rows, BRANCH_W), BF16), [], [], lambda n: {})

    def index(*g):
        r, c = local_index(*g)
        return (branch, row_block0 + r, c)

    return (pl.BlockSpec((None,) + block, index), jax.ShapeDtypeStruct(ys.shape, ys.dtype),
            [pl.BlockSpec(memory_space=pl.ANY)], [ys], lambda n: {n: 0})


def _drop_aliased(refs, n_in, aliased):
    refs = list(refs)
    return refs[:n_in] + refs[n_in + 1:] if aliased else refs


def _diff_attn_kernel(*refs, n_lat, n_cache, rope, tq, lam_init, aliased):
    refs = _drop_aliased(refs, 11 if rope else 5, aliased)
    if rope:
        (q_ref, k_ref, v_ref, lam_ref, g_ref, ck_ref, cv_ref, cosq_ref, sinq_ref, cosk_ref, sink_ref,
         o_ref, kbuf, vbuf) = refs
    else:
        q_ref, k_ref, v_ref, lam_ref, g_ref, o_ref, kbuf, vbuf = refs

    @pl.when(pl.program_id(2) == 0)
    def _():
        k = k_ref[...]
        if rope:
            k = _rope_lanes(k, cosk_ref[...], sink_ref[...], D_A)
        kbuf[0:n_lat, :] = k.astype(BF16)
        vbuf[0:n_lat, 0:2 * D_A] = v_ref[...].astype(BF16)
        if n_cache:
            kbuf[n_lat:n_lat + n_cache, :] = ck_ref[...].astype(BF16)
            vbuf[n_lat:n_lat + n_cache, 0:2 * D_A] = cv_ref[...].astype(BF16)
        vbuf[:, 2 * D_A:4 * D_A] = jnp.ones((n_lat + n_cache, 2 * D_A), BF16)

    q = q_ref[...]
    if rope:
        q = _rope_lanes(q, cosq_ref[...], sinq_ref[...], D_A)
    q = q * (D_A ** -0.5)
    lane = lax.broadcasted_iota(jnp.int32, (ATTN_SUB, 2 * D_A), 1)
    lf = lam_ref[...]
    lam = (jnp.exp(jnp.sum(lf[0:1] * lf[1:2], keepdims=True))
           - jnp.exp(jnp.sum(lf[2:3] * lf[3:4], keepdims=True)) + lam_init)
    for r in range(tq // ATTN_SUB):
        qr = q[r * ATTN_SUB:(r + 1) * ATTN_SUB]
        qs = jnp.concatenate([jnp.where(lane < D_A, qr, 0.0), jnp.where(lane >= D_A, qr, 0.0)], axis=0).astype(BF16)
        s = _dot_nt(qs, kbuf[...])
        e = jnp.exp(s - jnp.max(s, axis=-1, keepdims=True))
        pv_den = _dot(e.astype(BF16), vbuf[...])
        pv = pv_den[:, 0:2 * D_A] / pv_den[:, 2 * D_A:4 * D_A]
        o = pv[0:ATTN_SUB] - lam * pv[ATTN_SUB:2 * ATTN_SUB]
        y = o * lax.rsqrt(jnp.mean(o * o, axis=-1, keepdims=True) + EPS) * g_ref[...]
        o_ref[r * ATTN_SUB:(r + 1) * ATTN_SUB, :] = (y * (1.0 - lam_init)).astype(o_ref.dtype)


def _diff_attention(big, row0, B, N, layer, w_lam, w_subln, lam_init, cache=None, rope_tabs=None, tq=256,
                    ys=None, branch=0):
    rope = rope_tabs is not None
    n_cache = cache[0].shape[2] if cache is not None else 0
    nq = N // tq
    hw = 2 * D_A
    qc, kc, vc = OFF['a_q'] // hw, OFF['a_k'] // hw, OFF['a_v'] // hw
    in_specs = [pl.BlockSpec((tq, hw), lambda b, h, i: (row0 // tq + b * nq + i, qc + h)),
                pl.BlockSpec((N, hw), lambda b, h, i: (row0 // N + b, kc + h)),
                pl.BlockSpec((N, hw), lambda b, h, i: (row0 // N + b, vc + h)),
                pl.BlockSpec((None, 4, D_A), lambda b, h, i: (layer, 0, 0)),
                pl.BlockSpec((None, 1, hw), lambda b, h, i: (layer, 0, 0))]
    args = [big, big, big, w_lam, w_subln]
    if rope:
        cspec = pl.BlockSpec((None, None, n_cache, hw), lambda b, h, i: (b, layer, 0, h))
        in_specs += [cspec, cspec,
                     pl.BlockSpec((tq, hw), lambda b, h, i: (i, 0)), pl.BlockSpec((tq, hw), lambda b, h, i: (i, 0)),
                     pl.BlockSpec((N, hw), lambda b, h, i: (0, 0)), pl.BlockSpec((N, hw), lambda b, h, i: (0, 0))]
        args += [cache[0], cache[1], rope_tabs[0], rope_tabs[1], rope_tabs[0], rope_tabs[1]]
    out_spec, out_shape, x_specs, x_args, aliases = _branch_placement(
        ys, branch, row0 // tq, B * N, (tq, hw), lambda b, h, i: (b * nq + i, h))
    return pl.pallas_call(
        functools.partial(_diff_attn_kernel, n_lat=N, n_cache=n_cache, rope=rope, tq=tq, lam_init=lam_init,
                          aliased=ys is not None),
        grid=(B, H_A, nq),
        in_specs=in_specs + x_specs,
        out_specs=out_spec,
        out_shape=out_shape,
        input_output_aliases=aliases(len(args)),
        scratch_shapes=[pltpu.VMEM((N + n_cache, hw), BF16), pltpu.VMEM((N + n_cache, 2 * hw), BF16)],
        compiler_params=_cparams(("parallel", "parallel", "arbitrary")),
        name="diff_attention",
    )(*args, *x_args)


GQA_G = H_D // HKV_D


def _gqa_kernel(*refs, n_lat, n_cache, banded, tq, layer, aliased):
    refs = _drop_aliased(refs, 10 if banded else 4, aliased)
    if banded:
        (sink_ref, q_ref, k_ref, v_ref, ck_ref, cv_ref, cosq_ref, sinq_ref, cosk_ref, sink_tab_ref,
         o_ref, kbuf, vbuf) = refs
    else:
        sink_ref, q_ref, k_ref, v_ref, o_ref, kbuf, vbuf = refs
    W = WINDOW_D
    kv = pl.program_id(1)
    qi = pl.program_id(2)

    @pl.when(qi == 0)
    def _():
        k = k_ref[...]
        if banded:
            k = _rope_lanes(k, cosk_ref[...], sink_tab_ref[...], DH_D)
            zeros = jnp.zeros((W, DH_D), BF16)
            for buf, lat, cached in ((kbuf, k, ck_ref), (vbuf, v_ref[...], cv_ref)):
                buf[0:W, :] = zeros
                buf[W:W + n_lat, :] = lat.astype(BF16)
                buf[W + n_lat:2 * W + n_lat, :] = zeros
                buf[2 * W + n_lat:2 * W + n_lat + n_cache, :] = cached[...].astype(BF16)
        else:
            kbuf[...] = k.astype(BF16)
            vbuf[...] = v_ref[...].astype(BF16)

    q2 = q_ref[...]
    heads = [q2[:, g * DH_D:(g + 1) * DH_D] for g in range(GQA_G)]
    if banded:
        heads = [_rope_lanes(qh, cosq_ref[...], sinq_ref[...], DH_D) for qh in heads]
    qs = (jnp.concatenate(heads, axis=0) * (DH_D ** -0.5)).astype(BF16)
    row = lax.broadcasted_iota(jnp.int32, (GQA_G * tq, 1), 0)
    sink = jnp.full((GQA_G * tq, 1), sink_ref[layer, kv * GQA_G], F32)
    for g in range(1, GQA_G):
        sink = jnp.where(row >= g * tq, sink_ref[layer, kv * GQA_G + g], sink)

    if banded:
        start = pl.multiple_of(qi * W, W)
        s_loc = _dot_nt(qs, kbuf[pl.ds(start, 3 * W), :])
        r = lax.broadcasted_iota(jnp.int32, s_loc.shape, 0) & (tq - 1)
        c = lax.broadcasted_iota(jnp.int32, s_loc.shape, 1)
        kpos = qi * W + c - W
        valid = (jnp.abs(c - W - r) <= W) & (kpos >= 0) & (kpos < n_lat)
        s_loc = jnp.where(valid, s_loc, NEG_INF)
        s_ctx = _dot_nt(qs, kbuf[2 * W + n_lat:2 * W + n_lat + n_cache, :])
        m = jnp.maximum(jnp.maximum(jnp.max(s_loc, axis=-1, keepdims=True),
                                    jnp.max(s_ctx, axis=-1, keepdims=True)), sink)
        e_loc = jnp.exp(s_loc - m)
        e_ctx = jnp.exp(s_ctx - m)
        den = (jnp.sum(e_loc, axis=-1, keepdims=True) + jnp.sum(e_ctx, axis=-1, keepdims=True)
               + jnp.exp(sink - m))
        o = (_dot(e_loc.astype(BF16), vbuf[pl.ds(start, 3 * W), :])
             + _dot(e_ctx.astype(BF16), vbuf[2 * W + n_lat:2 * W + n_lat + n_cache, :])) / den
    else:
        s = _dot_nt(qs, kbuf[...])
        m = jnp.maximum(jnp.max(s, axis=-1, keepdims=True), sink)
        e = jnp.exp(s - m)
        den = jnp.sum(e, axis=-1, keepdims=True) + jnp.exp(sink - m)
        o = _dot(e.astype(BF16), vbuf[...]) / den
    o_ref[...] = jnp.concatenate([o[g * tq:(g + 1) * tq] for g in range(GQA_G)], axis=1).astype(o_ref.dtype)


def _gqa(big, row0, B, N, layer, w_sink, cache=None, rope_tabs=None, tq=WINDOW_D, ys=None, branch=0):
    banded = cache is not None
    n_cache = cache[0].shape[2] if banded else 0
    nq = N // tq
    qw = GQA_G * DH_D
    qc, kc, vc = OFF['d_q'] // qw, OFF['d_k'] // DH_D, OFF['d_v'] // DH_D
    in_specs = [pl.BlockSpec(memory_space=pltpu.SMEM),
                pl.BlockSpec((tq, qw), lambda b, h, i: (row0 // tq + b * nq + i, qc + h)),
                pl.BlockSpec((N, DH_D), lambda b, h, i: (row0 // N + b, kc + h)),
                pl.BlockSpec((N, DH_D), lambda b, h, i: (row0 // N + b, vc + h))]
    args = [w_sink, big, big, big]
    n_rows = N
    if banded:
        assert tq == WINDOW_D
        cspec = pl.BlockSpec((None, None, n_cache, DH_D), lambda b, h, i: (b, layer, 0, h))
        in_specs += [cspec, cspec,
                     pl.BlockSpec((tq, DH_D), lambda b, h, i: (i, 0)), pl.BlockSpec((tq, DH_D), lambda b, h, i: (i, 0)),
                     pl.BlockSpec((N, DH_D), lambda b, h, i: (0, 0)), pl.BlockSpec((N, DH_D), lambda b, h, i: (0, 0))]
        args += [cache[0], cache[1], rope_tabs[0], rope_tabs[1], rope_tabs[0], rope_tabs[1]]
        n_rows = N + 2 * WINDOW_D + n_cache
    out_spec, out_shape, x_specs, x_args, aliases = _branch_placement(
        ys, branch, row0 // tq, B * N, (tq, qw), lambda b, h, i: (b * nq + i, h))
    return pl.pallas_call(
        functools.partial(_gqa_kernel, n_lat=N, n_cache=n_cache, banded=banded, tq=tq, layer=layer,
                          aliased=ys is not None),
        grid=(B, HKV_D, nq),
        in_specs=in_specs + x_specs,
        out_specs=out_spec,
        out_shape=out_shape,
        input_output_aliases=aliases(len(args)),
        scratch_shapes=[pltpu.VMEM((n_rows, DH_D), BF16), pltpu.VMEM((n_rows, DH_D), BF16)],
        compiler_params=_cparams(("parallel", "parallel", "arbitrary")),
        name="gqa_sink_attention",
    )(*args, *x_args)


def _mlstm_chunk(q, k, v, i_col, lf_col, i_row, lf_row, C, n_row, m, rev):
    L = CHUNK_B
    t_i = lax.broadcasted_iota(jnp.int32, (L, L), 0)
    s_i = lax.broadcasted_iota(jnp.int32, (L, L), 1)
    tri = (s_i >= t_i) if rev else (s_i <= t_i)
    tri_t = (t_i >= s_i) if rev else (t_i <= s_i)
    b_col = jnp.sum(jnp.where(tri, lf_row, 0.0), axis=1, keepdims=True)
    b_row = jnp.sum(jnp.where(tri_t, lf_col, 0.0), axis=0, keepdims=True)
    a_col = b_col + m
    d = jnp.where(tri, b_col - b_row + i_row, NEG_INF)
    m_t = jnp.maximum(a_col, jnp.max(d, axis=1, keepdims=True))
    w_inter = jnp.exp(a_col - m_t)
    w_intra = jnp.exp(d - m_t)
    qb, kb, vb = q.astype(BF16), k.astype(BF16), v.astype(BF16)
    k_scale = DK_B ** -0.5
    qk = _dot_nt(qb, kb) * k_scale * w_intra
    num = w_inter * _dot(qb, C.astype(BF16)) + _dot(qk.astype(BF16), vb)
    den = w_inter * jnp.sum(q * n_row, axis=1, keepdims=True) + jnp.sum(qk, axis=1, keepdims=True)
    h = num / jnp.maximum(jnp.abs(den), jnp.exp(-m_t))
    last = 0 if rev else L - 1
    b_last = b_col[last:last + 1]
    m_last = m_t[last:last + 1]
    kw = k * (k_scale * jnp.exp(b_last - b_col + i_col - m_last))
    decay = w_inter[last:last + 1]
    C_new = decay * C + _dot_tn(kw.astype(BF16), vb)
    n_new = decay * n_row + jnp.sum(kw, axis=0, keepdims=True)
    return h, C_new, n_new, m_last


def _mlstm_kernel(*refs, zero_init, emit_state):
    refs = list(refs)
    qf, kf, vf, qb, kb, vb, gcf, gcb, grf, grb = refs[:10]
    pos = 10
    if not zero_init:
        c0_ref, n0_ref, m0_ref = refs[pos:pos + 3]
        pos += 3
    hf_ref, hb_ref = refs[pos:pos + 2]
    pos += 2
    if emit_state:
        co_ref, no_ref, mo_ref = refs[pos:pos + 3]
        pos += 3
    c_s, n_s, m_s = refs[pos:pos + 3]
    c = pl.program_id(1)

    @pl.when(c == 0)
    def _():
        if zero_init:
            c_s[...] = jnp.zeros_like(c_s)
            n_s[...] = jnp.zeros_like(n_s)
            m_s[...] = jnp.zeros_like(m_s)
        else:
            c_s[...] = c0_ref[...]
            n_s[...] = n0_ref[...]
            m_s[...] = m0_ref[...]

    streams = ((0, qf, kf, vf, gcf, grf, hf_ref), (1, qb, kb, vb, gcb, grb, hb_ref))
    for dr, q_ref, k_ref, v_ref, gc_ref, gr_ref, h_ref in streams:
        gc = gc_ref[...]
        gr = gr_ref[...]
        lf_c = _log_sigmoid(gc)
        lf_r = _log_sigmoid(gr)
        for hd in range(H_B):
            i_at = dr * 2 * H_B + hd
            f_at = i_at + H_B
            h, c_new, n_new, m_new = _mlstm_chunk(
                q_ref[:, hd * DK_B:(hd + 1) * DK_B], k_ref[:, hd * DK_B:(hd + 1) * DK_B],
                v_ref[:, hd * DV_B:(hd + 1) * DV_B],
                gc[:, i_at:i_at + 1], lf_c[:, f_at:f_at + 1], gr[i_at:i_at + 1, :], lf_r[f_at:f_at + 1, :],
                c_s[dr, hd], n_s[dr, hd], m_s[dr, hd], rev=bool(dr))
            h_ref[:, hd * DV_B:(hd + 1) * DV_B] = h
            c_s[dr, hd] = c_new
            n_s[dr, hd] = n_new
            m_s[dr, hd] = m_new

    if emit_state:
        @pl.when(c == pl.num_programs(1) - 1)
        def _():
            co_ref[...] = c_s[...]
            no_ref[...] = n_s[...]
            mo_ref[...] = m_s[...]


def _mlstm(big, gate_cols, gate_rows, row0, B, N, layer, state=None, emit_state=False):
    L = CHUNK_B
    nc = N // L
    r0 = row0 // L
    qw, vw = H_B * DK_B, H_B * DV_B
    qc, kc, vc = OFF['b_q'] // qw, OFF['b_k'] // qw, OFF['b_v'] // vw
    fwd = lambda b, c: r0 + b * nc + c
    bwd = lambda b, c: r0 + b * nc + (nc - 1 - c)
    in_specs, args = [], []
    for rowf in (fwd, bwd):
        for col, width in ((qc, qw), (kc, qw), (vc, vw)):
            in_specs.append(pl.BlockSpec((L, width), lambda b, c, rowf=rowf, col=col: (rowf(b, c), col)))
            args.append(big)
    for rowf in (fwd, bwd):
        in_specs.append(pl.BlockSpec((L, N_B_IF), lambda b, c, rowf=rowf: (rowf(b, c), 0)))
        args.append(gate_cols)
    for rowf in (fwd, bwd):
        in_specs.append(pl.BlockSpec((N_B_IF, L), lambda b, c, rowf=rowf: (0, rowf(b, c))))
        args.append(gate_rows)
    if state is not None:
        in_specs += [pl.BlockSpec((None, None, 2, H_B, DK_B, DV_B), lambda b, c: (b, layer, 0, 0, 0, 0)),
                     pl.BlockSpec((None, None, 2, H_B, 1, DK_B), lambda b, c: (b, layer, 0, 0, 0, 0)),
                     pl.BlockSpec((None, None, 2, H_B, 1, 1), lambda b, c: (b, layer, 0, 0, 0, 0))]
        args += list(state)
    out_specs = [pl.BlockSpec((L, vw), lambda b, c: (b * nc + c, 0)),
                 pl.BlockSpec((L, vw), lambda b, c: (b * nc + (nc - 1 - c), 0))]
    out_shape = [jax.ShapeDtypeStruct((B * N, vw), F32)] * 2
    if emit_state:
        out_specs += [pl.BlockSpec((None, 2, H_B, DK_B, DV_B), lambda b, c: (b, 0, 0, 0, 0)),
                      pl.BlockSpec((None, 2, H_B, 1, DK_B), lambda b, c: (b, 0, 0, 0, 0)),
                      pl.BlockSpec((None, 2, H_B, 1, 1), lambda b, c: (b, 0, 0, 0, 0))]
        out_shape += [jax.ShapeDtypeStruct((B, 2, H_B, DK_B, DV_B), F32),
                      jax.ShapeDtypeStruct((B, 2, H_B, 1, DK_B), F32),
                      jax.ShapeDtypeStruct((B, 2, H_B, 1, 1), F32)]
    return pl.pallas_call(
        functools.partial(_mlstm_kernel, zero_init=state is None, emit_state=emit_state),
        grid=(B, nc),
        in_specs=in_specs,
        out_specs=out_specs,
        out_shape=out_shape,
        scratch_shapes=[pltpu.VMEM((2, H_B, DK_B, DV_B), F32), pltpu.VMEM((2, H_B, 1, DK_B), F32),
                        pltpu.VMEM((2, H_B, 1, 1), F32)],
        compiler_params=_cparams(("parallel", "arbitrary")),
        name="mlstm_scan",
    )(*args)


def _prefix_sum_rows(tri_bf, x):
    x_hi = x.astype(BF16)
    r1 = x - x_hi.astype(F32)
    x_mid = r1.astype(BF16)
    x_lo = (r1 - x_mid.astype(F32)).astype(BF16)
    return _dot(tri_bf, x_hi) + _dot(tri_bf, x_mid) + _dot(tri_bf, x_lo)


def _gla_chunk(q, k, v, gc, s_t, rev):
    L, SB = CHUNK_C, SUB_C
    t_i = lax.broadcasted_iota(jnp.int32, (L, L), 0)
    s_i = lax.broadcasted_iota(jnp.int32, (L, L), 1)
    tri = (s_i >= t_i) if rev else (s_i <= t_i)
    bc = _prefix_sum_rows(jnp.where(tri, 1.0, 0.0).astype(BF16), gc)
    qs = q * (DK_C ** -0.5)
    inter = _dot_nt((qs * jnp.exp(bc)).astype(BF16), s_t.astype(BF16))
    col = lax.broadcasted_iota(jnp.int32, (SB, L), 1)
    blocks = []
    for ti in range(L // SB):
        lo, hi = ti * SB, (ti + 1) * SB
        ref_row = bc[hi - 1:hi] if rev else bc[lo:lo + 1]
        q_blk = qs[lo:hi]
        bc_blk = bc[lo:hi]
        q_dec = (q_blk * jnp.exp(bc_blk - ref_row)).astype(BF16)
        k_dec = (k * jnp.exp(jnp.minimum(ref_row - bc, 0.0))).astype(BF16)
        earlier = (col >= hi) if rev else (col < lo)
        a_blk = jnp.where(earlier, _dot_nt(q_dec, k_dec), 0.0)
        t_abs = lax.broadcasted_iota(jnp.int32, (SB, 1), 0) + lo
        for s in range(lo, hi):
            dec = jnp.exp(jnp.minimum(bc_blk - bc[s:s + 1], 0.0))
            a_col = jnp.sum(q_blk * k[s:s + 1] * dec, axis=-1, keepdims=True)
            ok = (t_abs <= s) if rev else (t_abs >= s)
            a_blk = jnp.where((col == s) & ok, a_col, a_blk)
        blocks.append(a_blk)
    a = jnp.concatenate(blocks, axis=0)
    vb = v.astype(BF16)
    o = inter + _dot(a.astype(BF16), vb)
    b_last = bc[0:1] if rev else bc[L - 1:L]
    k_dec = (k * jnp.exp(b_last - bc)).astype(BF16)
    s_new = jnp.exp(b_last) * s_t + _dot_tn(vb, k_dec)
    return o, s_new


def _gla_kernel(*refs, zero_init, emit_state):
    refs = list(refs)
    qf, kf, vf, sf, qb, kb, vb, sb, wup_ref, bup_ref = refs[:10]
    pos = 10
    if not zero_init:
        s0_ref = refs[pos]
        pos += 1
    of_ref, ob_ref = refs[pos:pos + 2]
    pos += 2
    if emit_state:
        so_ref = refs[pos]
        pos += 1
    st_s = refs[pos]
    c = pl.program_id(1)

    @pl.when(c == 0)
    def _():
        for dr in range(2):
            for hd in range(H_C):
                if zero_init:
                    st_s[dr, hd] = jnp.zeros((DV_C, DK_C), F32)
                else:
                    st_s[dr, hd] = s0_ref[dr, hd].T

    streams = ((0, qf, kf, vf, sf, of_ref), (1, qb, kb, vb, sb, ob_ref))
    for dr, q_ref, k_ref, v_ref, sm_ref, o_ref in streams:
        z = _dot(sm_ref[...].astype(BF16), wup_ref[dr]) + bup_ref[dr]
        gc = _log_sigmoid(z) * (1.0 / GATE_TAU_C)
        for hd in range(H_C):
            ks = slice(hd * DK_C, (hd + 1) * DK_C)
            vs = slice(hd * DV_C, (hd + 1) * DV_C)
            o, s_new = _gla_chunk(q_ref[:, ks], k_ref[:, ks], v_ref[:, vs], gc[:, ks], st_s[dr, hd], rev=bool(dr))
            o_ref[:, vs] = o
            st_s[dr, hd] = s_new

    if emit_state:
        @pl.when(c == pl.num_programs(1) - 1)
        def _():
            for dr in range(2):
                for hd in range(H_C):
                    so_ref[dr, hd] = st_s[dr, hd].T


def _gla(big, small, w_up_pad, b_up, row0, B, N, layer, state=None, emit_state=False):
    L = CHUNK_C
    nc = N // L
    r0 = row0 // L
    qw, vw = H_C * DK_C, H_C * DV_C
    qc, kc, vc = OFF['c_q'] // qw, OFF['c_k'] // qw, OFF['c_v'] // vw
    fwd = lambda b, c: r0 + b * nc + c
    bwd = lambda b, c: r0 + b * nc + (nc - 1 - c)
    in_specs, args = [], []
    for rowf in (fwd, bwd):
        for col, width in ((qc, qw), (kc, qw), (vc, vw)):
            in_specs.append(pl.BlockSpec((L, width), lambda b, c, rowf=rowf, col=col: (rowf(b, c), col)))
            args.append(big)
        in_specs.append(pl.BlockSpec((L, D_SMALL), lambda b, c, rowf=rowf: (rowf(b, c), 0)))
        args.append(small)
    in_specs += [pl.BlockSpec((None, 2, D_SMALL, qw), lambda b, c: (layer, 0, 0, 0)),
                 pl.BlockSpec((None, 2, 1, qw), lambda b, c: (layer, 0, 0, 0))]
    args += [w_up_pad, b_up]
    if state is not None:
        in_specs.append(pl.BlockSpec((None, None, 2, H_C, DK_C, DV_C), lambda b, c: (b, layer, 0, 0, 0, 0)))
        args.append(state)
    out_specs = [pl.BlockSpec((L, vw), lambda b, c: (b * nc + c, 0)),
                 pl.BlockSpec((L, vw), lambda b, c: (b * nc + (nc - 1 - c), 0))]
    out_shape = [jax.ShapeDtypeStruct((B * N, vw), F32)] * 2
    if emit_state:
        out_specs.append(pl.BlockSpec((None, 2, H_C, DK_C, DV_C), lambda b, c: (b, 0, 0, 0, 0)))
        out_shape.append(jax.ShapeDtypeStruct((B, 2, H_C, DK_C, DV_C), F32))
    return pl.pallas_call(
        functools.partial(_gla_kernel, zero_init=state is None, emit_state=emit_state),
        grid=(B, nc),
        in_specs=in_specs,
        out_specs=out_specs,
        out_shape=out_shape,
        scratch_shapes=[pltpu.VMEM((2, H_C, DV_C, DK_C), F32)],
        compiler_params=_cparams(("parallel", "arbitrary")),
        name="gla_scan",
    )(*args)


def _gated_norm_kernel(*refs, silu, aliased):
    hf_ref, hb_ref, pre_ref, g_ref, o_ref = _drop_aliased(refs, 4, aliased)
    h = hf_ref[...] + hb_ref[...]
    y = h * lax.rsqrt(jnp.mean(h * h, axis=-1, keepdims=True) + EPS) * g_ref[...]
    pre = pre_ref[...]
    gate = jax.nn.sigmoid(pre)
    if silu:
        gate = pre * gate
    o_ref[...] = (y * gate).astype(o_ref.dtype)


def _gated_norm(hf, hb, big, pre_off, row0, w_norm, layer, silu, tm=512, ys=None, branch=0):
    T = hf.shape[0]
    hw = DV_B
    blk = pl.BlockSpec((tm, hw), lambda i, h: (i, h))
    args = [hf, hb, big, w_norm]
    out_spec, out_shape, x_specs, x_args, aliases = _branch_placement(
        ys, branch, row0 // tm, T, (tm, hw), lambda i, h: (i, h))
    return pl.pallas_call(
        functools.partial(_gated_norm_kernel, silu=silu, aliased=ys is not None),
        grid=(T // tm, BRANCH_W // hw),
        in_specs=[blk, blk,
                  pl.BlockSpec((tm, hw), lambda i, h: (row0 // tm + i, pre_off // hw + h)),
                  pl.BlockSpec((None, 1, hw), lambda i, h: (layer, 0, h))] + x_specs,
        out_specs=out_spec,
        out_shape=out_shape,
        input_output_aliases=aliases(len(args)),
        compiler_params=_cparams(("parallel", "parallel")),
        name="gated_head_norm",
    )(*args, *x_args)


def _modulation_tables(c_ctx, c, w_ada_l, b_ada_l):
    cond = jnp.concatenate([c_ctx[None, :], c], axis=0)
    m = jnp.dot(jax.nn.silu(cond), w_ada_l, precision=lax.Precision.HIGHEST) + b_ada_l
    return m.reshape(N_SEG, 6, 1, D_MODEL)


def _mixers(big, small, l, lam_init, shared, caches, ys):
    (w_lam, w_subln, gate_bias, w_b_norm, w_up_pad, b_up, w_c_norm, w_sink, rope_a, rope_d) = shared
    (ca_k, ca_v, sb_c, sb_n, sb_m, sc_s, cd_k, cd_v) = caches
    gate_cols = small[:, :N_B_IF] + gate_bias[l]
    gate_rows = gate_cols.T

    ys = _diff_attention(big, 0, BATCH, SEQ, l, w_lam, w_subln, lam_init, ys=ys, branch=0)
    ys = _diff_attention(big, T_CTX, DEC_BATCH, DEC_SEQ, l, w_lam, w_subln, lam_init, cache=(ca_k, ca_v),
                         rope_tabs=rope_a, ys=ys, branch=0)

    hf_c, hb_c, c_out, n_out, m_out = _mlstm(big, gate_cols, gate_rows, 0, BATCH, SEQ, l, emit_state=True)
    hf_l, hb_l = _mlstm(big, gate_cols, gate_rows, T_CTX, DEC_BATCH, DEC_SEQ, l, state=(sb_c, sb_n, sb_m))
    ys = _gated_norm(hf_c, hb_c, big, OFF['b_o'], 0, w_b_norm, l, silu=False, ys=ys, branch=1)
    ys = _gated_norm(hf_l, hb_l, big, OFF['b_o'], T_CTX, w_b_norm, l, silu=False, ys=ys, branch=1)

    of_c, ob_c, s_out = _gla(big, small, w_up_pad, b_up, 0, BATCH, SEQ, l, emit_state=True)
    of_l, ob_l = _gla(big, small, w_up_pad, b_up, T_CTX, DEC_BATCH, DEC_SEQ, l, state=sc_s)
    ys = _gated_norm(of_c, ob_c, big, OFF['c_r'], 0, w_c_norm, l, silu=True, ys=ys, branch=2)
    ys = _gated_norm(of_l, ob_l, big, OFF['c_r'], T_CTX, w_c_norm, l, silu=True, ys=ys, branch=2)

    ys = _gqa(big, 0, BATCH, SEQ, l, w_sink, tq=SEQ, ys=ys, branch=3)
    ys = _gqa(big, T_CTX, DEC_BATCH, DEC_SEQ, l, w_sink, cache=(cd_k, cd_v), rope_tabs=rope_d, ys=ys, branch=3)

    ctx_rows = big[:T_CTX]
    ctx = (ctx_rows[:, OFF['a_k']:OFF['a_k'] + 1024].reshape(BATCH, SEQ, H_A, 2 * D_A),
           ctx_rows[:, OFF['a_v']:OFF['a_v'] + 1024].reshape(BATCH, SEQ, H_A, 2 * D_A),
           c_out, n_out.reshape(BATCH, 2, H_B, DK_B), m_out.reshape(BATCH, 2, H_B), s_out,
           ctx_rows[:, OFF['d_k']:OFF['d_k'] + 512].reshape(BATCH, SEQ, HKV_D, DH_D),
           ctx_rows[:, OFF['d_v']:OFF['d_v'] + 512].reshape(BATCH, SEQ, HKV_D, DH_D))
    return ys, ctx


def kernel(x_prompt, x_sample, c, cache_a_k, cache_a_v, state_b_C, state_b_n, state_b_m, state_c_S, cache_d_k, cache_d_v, c_ctx, w_ada, b_ada, w_norm1, w_norm2, w_in, w_a_lambda, w_a_subln, b_b_if, w_b_norm, w_c_alpha_up, b_c_alpha, w_c_norm, w_d_sink, w_branch, w_merge_gate, w_out, w_router_group, b_router_group, w_router_expert, b_router_expert, w_exp_in, w_exp_out, w_norm_f):
    rows = DEC_SEQ // GRID_W
    rope_a = _rope_lane_tables(rows, D_A)
    rope_d = _rope_lane_tables(rows, DH_D)

    w_big = jnp.concatenate([w_in[:, :, :OFF_B_IF], w_in[:, :, OFF_C_Q:OFF_C_GLR], w_in[:, :, OFF_D_Q:]],
                            axis=-1).astype(BF16)
    w_small = jnp.concatenate([w_in[:, :, OFF_B_IF:OFF_C_Q], w_in[:, :, OFF_C_GLR:OFF_D_Q],
                               jnp.zeros((DEPTH, D_MODEL, D_SMALL - N_B_IF - N_C_GLR), F32)], axis=-1).astype(BF16)
    w_gate_bf = w_merge_gate.astype(BF16)
    w_branch_bf = w_branch.astype(BF16)
    w_out_bf = w_out.astype(BF16)
    w_exp_in_bf, w_exp_out_bf = _moe_weight_layout(w_exp_in, w_exp_out)

    w_up_pad = jnp.zeros((DEPTH, 2, D_SMALL, H_C * DK_C), F32)
    for d in range(2):
        lo = N_B_IF + d * GATE_RANK_C
        w_up_pad = w_up_pad.at[:, d, lo:lo + GATE_RANK_C, :].set(w_c_alpha_up[:, d])
    w_up_pad = w_up_pad.astype(BF16)

    shared = (w_a_lambda, w_a_subln.reshape(DEPTH, 1, 2 * D_A), b_b_if.reshape(DEPTH, N_B_IF),
              w_b_norm.reshape(DEPTH, 1, H_B * DV_B), w_up_pad, b_c_alpha.reshape(DEPTH, 2, 1, H_C * DK_C),
              w_c_norm.reshape(DEPTH, 1, H_C * DV_C), w_d_sink, rope_a, rope_d)
    caches = (cache_a_k.reshape(DEC_BATCH, DEPTH, PAST_LEN, H_A * 2 * D_A),
              cache_a_v.reshape(DEC_BATCH, DEPTH, PAST_LEN, H_A * 2 * D_A),
              state_b_C, state_b_n.reshape(DEC_BATCH, DEPTH, 2, H_B, 1, DK_B),
              state_b_m.reshape(DEC_BATCH, DEPTH, 2, H_B, 1, 1), state_c_S,
              cache_d_k.reshape(DEC_BATCH, DEPTH, PAST_LEN, HKV_D * DH_D),
              cache_d_v.reshape(DEC_BATCH, DEPTH, PAST_LEN, HKV_D * DH_D))

    x = jnp.concatenate([x_prompt.reshape(T_CTX, D_MODEL), x_sample.reshape(T_LAT, D_MODEL)], axis=0)
    ys = jnp.zeros((N_BRANCH, T_ALL, BRANCH_W), BF16)
    ctx_out = [[] for _ in range(8)]
    for l in range(DEPTH):
        lw = {
            'w_rg': w_router_group[l], 'b_rg': b_router_group[l],
            'w_re': w_router_expert[l], 'b_re': b_router_expert[l],
            'w_exp_in': w_exp_in_bf, 'w_exp_out': w_exp_out_bf,
        }
        lam_init = 0.8 - 0.6 * math.exp(-0.3 * l)
        mods = _modulation_tables(c_ctx, c, w_ada[l], b_ada[l])
        sh1, sc1, g1, sh2, sc2, g2 = [mods[:, i] for i in range(6)]

        u = _norm_mod(x, w_norm1[l][None, :], sh1, sc1)
        big = _mm(u, w_big, l, F32)
        small = _mm(u, w_small, l, F32, tn=D_SMALL)

        ys, ctx = _mixers(big, small, l, lam_init, shared, caches, ys)
        for lst, t in zip(ctx_out, ctx):
            lst.append(t)

        merged = _merge(u, ys, w_gate_bf, w_branch_bf, l)
        x = _mm_residual(merged, w_out_bf, l, x, g1)

        u2 = _norm_mod(x, w_norm2[l][None, :], sh2, sc2)
        ya, yb = _hier_moe(u2, lw, l)
        x = _residual_pair(x, g2, ya, yb)

    y = _norm(x, w_norm_f[None, :])
    y_prompt = y[:T_CTX].reshape(BATCH, SEQ, D_MODEL)
    y_sample = y[T_CTX:].reshape(DEC_BATCH, DEC_SEQ, D_MODEL)
    outs = [jnp.stack(t, axis=1) for t in ctx_out]
    return (y_prompt, y_sample) + tuple(outs)
```

```python
import functools
import math

import jax
import jax.numpy as jnp
from jax import lax
from jax.experimental import pallas as pl
from jax.experimental.pallas import tpu as pltpu

D_MODEL = 4096
BATCH = 32
SEQ = 256
DEPTH = 4
DEC_BATCH = 8
DEC_SEQ = 2048
PAST_LEN = 512
GRID_W = 64
ROPE_BASE = 10000.0
EPS = 1e-6
NEG_INF = -1e30
H_A = 8
D_A = 64
H_B = 4
DK_B = 256
DV_B = 256
CHUNK_B = 128
H_C = 4
DK_C = 128
DV_C = 256
GATE_RANK_C = 16
GATE_TAU_C = 16.0
CHUNK_C = 64
SUB_C = 16
H_D = 8
HKV_D = 4
DH_D = 128
WINDOW_D = 128
N_BRANCH = 4
BRANCH_W = 1024
N_GROUPS = 4
EXPERTS_PER_GROUP = 4
N_EXPERTS = N_GROUPS * EXPERTS_PER_GROUP
TOP_K_INNER = 2
D_EXPERT = 1024

F32 = jnp.float32
BF16 = jnp.bfloat16
LANES = 128

T_CTX = BATCH * SEQ
T_LAT = DEC_BATCH * DEC_SEQ
T_ALL = T_CTX + T_LAT
N_SEG = 1 + DEC_BATCH

BIG_PARTS = (
    ('a_q', 1024), ('a_k', 1024), ('a_v', 1024),
    ('b_q', 1024), ('b_k', 1024), ('b_v', 1024), ('b_o', 1024),
    ('c_q', 512), ('c_k', 512), ('c_v', 1024), ('c_r', 1024),
    ('d_q', 1024), ('d_k', 512), ('d_v', 512),
)
D_BIG = sum(w for _, w in BIG_PARTS)
OFF = {}
_o = 0
for _name, _w in BIG_PARTS:
    OFF[_name] = _o
    _o += _w
N_B_IF = 2 * 2 * H_B
N_C_GLR = 2 * GATE_RANK_C
D_SMALL = LANES
OFF_B_IF = 7 * 1024
OFF_C_Q = OFF_B_IF + N_B_IF
OFF_C_GLR = OFF_C_Q + 512 + 512 + 1024 + 1024
OFF_D_Q = OFF_C_GLR + N_C_GLR

VMEM_LIMIT = 56 * 1024 * 1024


def _cparams(sem):
    return pltpu.CompilerParams(dimension_semantics=sem, vmem_limit_bytes=VMEM_LIMIT)


def _seg_of_tile(i, tm):
    n_ctx = T_CTX // tm
    per_lat = DEC_SEQ // tm
    return jnp.where(i < n_ctx, 0, 1 + (i - n_ctx) // per_lat)


def _dot(a, b):
    return jnp.dot(a, b, preferred_element_type=F32)


def _dot_nt(a, b):
    return lax.dot_general(a, b, (((1,), (1,)), ((), ())), preferred_element_type=F32)


def _dot_tn(a, b):
    return lax.dot_general(a, b, (((0,), (0,)), ((), ())), preferred_element_type=F32)


def _log_sigmoid(x):
    return jnp.minimum(x, 0.0) - jnp.log1p(jnp.exp(-jnp.abs(x)))


def _norm_mod_kernel(x_ref, g_ref, sh_ref, sc_ref, o_ref):
    x = x_ref[...]
    y = x * lax.rsqrt(jnp.mean(x * x, axis=-1, keepdims=True) + EPS) * g_ref[...]
    o_ref[...] = (y * (1.0 + sc_ref[...]) + sh_ref[...]).astype(o_ref.dtype)


def _norm_mod(x, g, shift, scale, tm=256):
    T, D = x.shape
    seg = lambda i: (_seg_of_tile(i, tm), 0, 0)
    return pl.pallas_call(
        _norm_mod_kernel,
        grid=(T // tm,),
        in_specs=[pl.BlockSpec((tm, D), lambda i: (i, 0)),
                  pl.BlockSpec((1, D), lambda i: (0, 0)),
                  pl.BlockSpec((None, 1, D), seg),
                  pl.BlockSpec((None, 1, D), seg)],
        out_specs=pl.BlockSpec((tm, D), lambda i: (i, 0)),
        out_shape=jax.ShapeDtypeStruct((T, D), BF16),
        compiler_params=_cparams(("parallel",)),
        name="norm_mod",
    )(x, g, shift, scale)


def _norm_mod_route_kernel(x_ref, g_ref, sh_ref, sc_ref, wr_ref, br_ref, o_ref, lg_ref):
    x = x_ref[...]
    y = x * lax.rsqrt(jnp.mean(x * x, axis=-1, keepdims=True) + EPS) * g_ref[...]
    u = (y * (1.0 + sc_ref[...]) + sh_ref[...]).astype(o_ref.dtype)
    o_ref[...] = u
    lg_ref[...] = _dot(u, wr_ref[0]) + _dot(u, wr_ref[1]) + _dot(u, wr_ref[2]) + br_ref[...]


def _norm_mod_route(x, g, shift, scale, w_router3, b_router, tm=256):
    T, D = x.shape
    seg = lambda i: (_seg_of_tile(i, tm), 0, 0)
    return pl.pallas_call(
        _norm_mod_route_kernel,
        grid=(T // tm,),
        in_specs=[pl.BlockSpec((tm, D), lambda i: (i, 0)),
                  pl.BlockSpec((1, D), lambda i: (0, 0)),
                  pl.BlockSpec((None, 1, D), seg),
                  pl.BlockSpec((None, 1, D), seg),
                  pl.BlockSpec((3, D, LANES), lambda i: (0, 0, 0)),
                  pl.BlockSpec((1, LANES), lambda i: (0, 0))],
        out_specs=[pl.BlockSpec((tm, D), lambda i: (i, 0)), pl.BlockSpec((tm, LANES), lambda i: (i, 0))],
        out_shape=[jax.ShapeDtypeStruct((T, D), BF16), jax.ShapeDtypeStruct((T, LANES), F32)],
        compiler_params=_cparams(("parallel",)),
        name="norm_mod_route",
    )(x, g, shift, scale, w_router3, b_router)


def _split3_bf16(w):
    hi = w.astype(BF16)
    r1 = w - hi.astype(F32)
    mid = r1.astype(BF16)
    lo = (r1 - mid.astype(F32)).astype(BF16)
    return jnp.stack([hi, mid, lo], axis=0)


def _norm_kernel(x_ref, g_ref, o_ref):
    x = x_ref[...]
    o_ref[...] = x * lax.rsqrt(jnp.mean(x * x, axis=-1, keepdims=True) + EPS) * g_ref[...]


def _norm(x, g, tm=256):
    T, D = x.shape
    return pl.pallas_call(
        _norm_kernel,
        grid=(T // tm,),
        in_specs=[pl.BlockSpec((tm, D), lambda i: (i, 0)),
                  pl.BlockSpec((1, D), lambda i: (0, 0))],
        out_specs=pl.BlockSpec((tm, D), lambda i: (i, 0)),
        out_shape=jax.ShapeDtypeStruct((T, D), F32),
        compiler_params=_cparams(("parallel",)),
        name="final_norm",
    )(x, g)


def _mm_kernel(a_ref, w_ref, o_ref):
    o_ref[...] = _dot(a_ref[...], w_ref[...]).astype(o_ref.dtype)


def _mm(a, w, layer, out_dtype, tm=1024, tn=1024):
    T, K = a.shape
    N = w.shape[-1]
    return pl.pallas_call(
        _mm_kernel,
        grid=(T // tm, N // tn),
        in_specs=[pl.BlockSpec((tm, K), lambda i, j: (i, 0)),
                  pl.BlockSpec((None, K, tn), lambda i, j: (layer, 0, j))],
        out_specs=pl.BlockSpec((tm, tn), lambda i, j: (i, j)),
        out_shape=jax.ShapeDtypeStruct((T, N), out_dtype),
        compiler_params=_cparams(("parallel", "parallel")),
        name="proj_mm",
    )(a, w)


def _mm_res_kernel(a_ref, w_ref, x_ref, g_ref, o_ref):
    o_ref[...] = x_ref[...] + g_ref[...] * _dot(a_ref[...], w_ref[...])


def _mm_residual(a, w, layer, x, gate, tm=1024, tn=512):
    T, K = a.shape
    N = w.shape[-1]
    return pl.pallas_call(
        _mm_res_kernel,
        grid=(T // tm, N // tn),
        in_specs=[pl.BlockSpec((tm, K), lambda i, j: (i, 0)),
                  pl.BlockSpec((None, K, tn), lambda i, j: (layer, 0, j)),
                  pl.BlockSpec((tm, tn), lambda i, j: (i, j)),
                  pl.BlockSpec((None, 1, tn), lambda i, j: (_seg_of_tile(i, tm), 0, j))],
        out_specs=pl.BlockSpec((tm, tn), lambda i, j: (i, j)),
        out_shape=jax.ShapeDtypeStruct((T, N), F32),
        compiler_params=_cparams(("parallel", "parallel")),
        name="out_proj_residual",
    )(a, w, x, gate)


def _merge_kernel(u_ref, wg_ref, y_ref, wb_ref, o_ref, acc_ref):
    b = pl.program_id(2)
    gate = jax.nn.sigmoid(_dot(u_ref[...], wg_ref[...]))
    term = gate * _dot(y_ref[...], wb_ref[...])

    @pl.when(b == 0)
    def _():
        acc_ref[...] = term

    @pl.when(b > 0)
    def _():
        acc_ref[...] += term

    @pl.when(b == N_BRANCH - 1)
    def _():
        o_ref[...] = acc_ref[...].astype(o_ref.dtype)


def _merge(u, ys, w_gate, w_branch, layer, tm=1024, tn=512):
    T, D = u.shape
    nj = D // tn
    return pl.pallas_call(
        _merge_kernel,
        grid=(T // tm, nj, N_BRANCH),
        in_specs=[pl.BlockSpec((tm, D), lambda i, j, b: (i, 0)),
                  pl.BlockSpec((None, D, tn), lambda i, j, b: (layer, 0, b * nj + j)),
                  pl.BlockSpec((None, tm, BRANCH_W), lambda i, j, b: (b, i, 0)),
                  pl.BlockSpec((None, None, BRANCH_W, tn), lambda i, j, b: (layer, b, 0, j))],
        out_specs=pl.BlockSpec((tm, tn), lambda i, j, b: (i, j)),
        out_shape=jax.ShapeDtypeStruct((T, D), BF16),
        scratch_shapes=[pltpu.VMEM((tm, tn), F32)],
        compiler_params=_cparams(("parallel", "parallel", "arbitrary")),
        name="branch_merge",
    )(u, w_gate, ys, w_branch)


MOE_TM = 512
MOE_NA = 2
MOE_NB = 2


def _moe_up_kernel(te_ref, nt_ref, x_ref, w1_ref, w2_ref, act_ref):
    used = pl.program_id(1) < nt_ref[0]

    @pl.when(used)
    def _():
        x = x_ref[...]
        h1 = _dot(x, w1_ref[...])
        h2 = _dot(x, w2_ref[...])
        act_ref[...] = (h1 * jax.nn.sigmoid(h1) * h2).astype(act_ref.dtype)

    @pl.when(jnp.logical_not(used))
    def _():
        act_ref[...] = jnp.zeros_like(act_ref)


def _moe_down_kernel(te_ref, nt_ref, act_ref, wo_ref, rs_ref, o_ref):
    used = pl.program_id(1) < nt_ref[0]

    @pl.when(used)
    def _():
        o_ref[...] = (_dot(act_ref[...], wo_ref[...]) * rs_ref[...]).astype(o_ref.dtype)

    @pl.when(jnp.logical_not(used))
    def _():
        o_ref[...] = jnp.zeros_like(o_ref)


def _moe_experts(xs, tile_expert, n_tiles, row_scale, w_in, w_out, layer):
    rows, D = xs.shape
    tiles = rows // MOE_TM
    hc = D_EXPERT // MOE_NA
    cb = D // MOE_NB
    clamp = lambda j, nt: jnp.minimum(j, nt[0] - 1)

    act = pl.pallas_call(
        _moe_up_kernel,
        grid_spec=pltpu.PrefetchScalarGridSpec(
            num_scalar_prefetch=2,
            grid=(MOE_NA, tiles),
            in_specs=[pl.BlockSpec((MOE_TM, D), lambda c, j, te, nt: (clamp(j, nt), 0)),
                      pl.BlockSpec((None, None, D, hc), lambda c, j, te, nt: (layer, te[j], 0, c)),
                      pl.BlockSpec((None, None, D, hc), lambda c, j, te, nt: (layer, te[j], 0, c + MOE_NA))],
            out_specs=pl.BlockSpec((MOE_TM, hc), lambda c, j, te, nt: (j, c)),
        ),
        out_shape=jax.ShapeDtypeStruct((rows, D_EXPERT), BF16),
        compiler_params=_cparams(("arbitrary", "arbitrary")),
        name="moe_up",
    )(tile_expert, n_tiles, xs, w_in, w_in)

    return pl.pallas_call(
        _moe_down_kernel,
        grid_spec=pltpu.PrefetchScalarGridSpec(
            num_scalar_prefetch=2,
            grid=(MOE_NB, tiles),
            in_specs=[pl.BlockSpec((MOE_TM, D_EXPERT), lambda c, j, te, nt: (clamp(j, nt), 0)),
                      pl.BlockSpec((None, None, D_EXPERT, cb), lambda c, j, te, nt: (layer, te[j], 0, c)),
                      pl.BlockSpec((MOE_TM, 1), lambda c, j, te, nt: (clamp(j, nt), 0))],
            out_specs=pl.BlockSpec((MOE_TM, cb), lambda c, j, te, nt: (j, c)),
        ),
        out_shape=jax.ShapeDtypeStruct((rows, D), BF16),
        compiler_params=_cparams(("arbitrary", "arbitrary")),
        name="moe_down",
    )(tile_expert, n_tiles, act, w_out, row_scale)


def _router_operands(w_rg, b_rg, w_re, b_re):
    D = w_rg.shape[0]
    pad = LANES - N_GROUPS - N_EXPERTS
    w = jnp.concatenate([w_rg, w_re, jnp.zeros((D, pad), F32)], axis=1)
    b = jnp.concatenate([b_rg, b_re, jnp.zeros((pad,), F32)])[None, :]
    return _split3_bf16(w), b


def _route(logits):
    T = logits.shape[0]
    lg = logits[:, :N_GROUPS]
    pg = jax.nn.softmax(lg, axis=-1)
    gsel = jnp.argmax(lg, axis=-1)
    pg_sel = jnp.take_along_axis(pg, gsel[:, None], axis=1)
    le = logits[:, N_GROUPS:N_GROUPS + N_EXPERTS].reshape(T, N_GROUPS, EXPERTS_PER_GROUP)
    le_sel = jnp.take_along_axis(le, gsel[:, None, None], axis=1)[:, 0]
    top_v, top_i = lax.top_k(le_sel, TOP_K_INNER)
    w_top = jax.nn.softmax(top_v, axis=-1) * pg_sel
    eidx = (gsel[:, None] * EXPERTS_PER_GROUP + top_i).astype(jnp.int32)
    return eidx, w_top


def _dispatch(eidx, w_top):
    T = eidx.shape[0]
    flat_e = eidx.reshape(-1)
    onehot = (flat_e[:, None] == jnp.arange(N_EXPERTS, dtype=jnp.int32)[None, :]).astype(jnp.int32)
    csum = jnp.cumsum(onehot, axis=0)
    counts = csum[-1]
    rank = jnp.sum(csum * onehot, axis=1) - 1
    padded = ((counts + MOE_TM - 1) // MOE_TM) * MOE_TM
    pend = jnp.cumsum(padded)
    pstart = pend - padded
    dest = (pstart[flat_e] + rank).astype(jnp.int32)
    tok = jnp.arange(2 * T, dtype=jnp.int32) // TOP_K_INNER
    rows = TOP_K_INNER * T + N_EXPERTS * MOE_TM
    tiles = rows // MOE_TM
    src_tok = jnp.zeros((rows,), jnp.int32).at[dest].set(tok)
    row_scale = jnp.zeros((rows,), F32).at[dest].set(w_top.reshape(-1))
    n_tiles = (pend[-1] // MOE_TM).astype(jnp.int32)
    tile_start = jnp.arange(tiles, dtype=jnp.int32) * MOE_TM
    tile_expert = jnp.searchsorted(pend, tile_start, side='right').astype(jnp.int32)
    last_e = jnp.searchsorted(pend, (n_tiles - 1) * MOE_TM, side='right').astype(jnp.int32)
    tile_expert = jnp.where(jnp.arange(tiles) < n_tiles, tile_expert, last_e)
    return src_tok, row_scale[:, None], tile_expert, n_tiles.reshape(1), dest.reshape(T, TOP_K_INNER)


def _hier_moe(u2, logits, lw, layer):
    eidx, w_top = _route(logits)
    src_tok, row_scale, tile_expert, n_tiles, dest = _dispatch(eidx, w_top)
    rows_of = lambda a, idx: a.at[idx].get(mode='promise_in_bounds')
    xs = rows_of(u2, src_tok)
    ys = _moe_experts(xs, tile_expert, n_tiles, row_scale, lw['w_exp_in'], lw['w_exp_out'], layer)
    return rows_of(ys, dest[:, 0]), rows_of(ys, dest[:, 1])


def _residual_pair_kernel(x_ref, g_ref, a_ref, b_ref, o_ref):
    o_ref[...] = x_ref[...] + g_ref[...] * (a_ref[...].astype(F32) + b_ref[...].astype(F32))


def _residual_pair(x, gate, ya, yb, tm=256):
    T, D = x.shape
    row = pl.BlockSpec((tm, D), lambda i: (i, 0))
    return pl.pallas_call(
        _residual_pair_kernel,
        grid=(T // tm,),
        in_specs=[row, pl.BlockSpec((None, 1, D), lambda i: (_seg_of_tile(i, tm), 0, 0)), row, row],
        out_specs=row,
        out_shape=jax.ShapeDtypeStruct((T, D), F32),
        compiler_params=_cparams(("parallel",)),
        name="moe_residual",
    )(x, gate, ya, yb)


def _axial_rope_tables(rows, dim):
    row = jnp.repeat(jnp.arange(rows, dtype=F32), GRID_W)
    col = jnp.tile(jnp.arange(GRID_W, dtype=F32), rows)
    n_freq = dim // 4
    inv = ROPE_BASE ** (-jnp.arange(n_freq, dtype=F32) / n_freq)
    ang = jnp.concatenate([row[:, None] * inv[None], col[:, None] * inv[None]], axis=-1)
    return jnp.cos(ang), jnp.sin(ang)


def _rope_lane_tables(rows, dim):
    cos, sin = _axial_rope_tables(rows, dim)
    groups = LANES // dim
    return (jnp.tile(cos, (1, 2 * groups)),
            jnp.tile(jnp.concatenate([-sin, sin], axis=-1), (1, groups)))


def _rope_lanes(x, cos_t, sin_t, dim):
    half = dim // 2
    if dim == LANES:
        swapped = pltpu.roll(x, half, 1)
    else:
        lane = lax.broadcasted_iota(jnp.int32, x.shape, 1)
        swapped = jnp.where((lane & half) == 0, pltpu.roll(x, LANES - half, 1), pltpu.roll(x, half, 1))
    return x * cos_t + swapped * sin_t


ATTN_SUB = 64

def _branch_placement(ys, branch, row_block0, rows, block, local_index):
    if ys is None:
        return (pl.BlockSpec(block, local_index), jax.ShapeDtypeStruct((rows, BRANCH_W), BF16), [], [], lambda n: {})

    def index(*g):
        r, c = local_index(*g)
        return (branch, row_block0 + r, c)

    return (pl.BlockSpec((None,) + block, index), jax.ShapeDtypeStruct(ys.shape, ys.dtype),
            [pl.BlockSpec(memory_space=pl.ANY)], [ys], lambda n: {n: 0})


def _drop_aliased(refs, n_in, aliased):
    refs = list(refs)
    return refs[:n_in] + refs[n_in + 1:] if aliased else refs


def _diff_attn_kernel(*refs, n_lat, n_cache, rope, tq, lam_init, aliased):
    refs = _drop_aliased(refs, 11 if rope else 5, aliased)
    if rope:
        (q_ref, k_ref, v_ref, lam_ref, g_ref, ck_ref, cv_ref, cosq_ref, sinq_ref, cosk_ref, sink_ref,
         o_ref, kbuf, vbuf) = refs
    else:
        q_ref, k_ref, v_ref, lam_ref, g_ref, o_ref, kbuf, vbuf = refs

    @pl.when(pl.program_id(2) == 0)
    def _():
        k = k_ref[...]
        if rope:
            k = _rope_lanes(k, cosk_ref[...], sink_ref[...], D_A)
        kbuf[0:n_lat, :] = k.astype(BF16)
        vbuf[0:n_lat, 0:2 * D_A] = v_ref[...].astype(BF16)
        if n_cache:
            kbuf[n_lat:n_lat + n_cache, :] = ck_ref[...].astype(BF16)
            vbuf[n_lat:n_lat + n_cache, 0:2 * D_A] = cv_ref[...].astype(BF16)
        vbuf[:, 2 * D_A:4 * D_A] = jnp.ones((n_lat + n_cache, 2 * D_A), BF16)

    q = q_ref[...]
    if rope:
        q = _rope_lanes(q, cosq_ref[...], sinq_ref[...], D_A)
    q = q * (D_A ** -0.5)
    lane = lax.broadcasted_iota(jnp.int32, (ATTN_SUB, 2 * D_A), 1)
    lf = lam_ref[...]
    lam = (jnp.exp(jnp.sum(lf[0:1] * lf[1:2], keepdims=True))
           - jnp.exp(jnp.sum(lf[2:3] * lf[3:4], keepdims=True)) + lam_init)
    for r in range(tq // ATTN_SUB):
        qr = q[r * ATTN_SUB:(r + 1) * ATTN_SUB]
        qs = jnp.concatenate([jnp.where(lane < D_A, qr, 0.0), jnp.where(lane >= D_A, qr, 0.0)], axis=0).astype(BF16)
        s = _dot_nt(qs, kbuf[...])
        e = jnp.exp(s - jnp.max(s, axis=-1, keepdims=True))
        pv_den = _dot(e.astype(BF16), vbuf[...])
        pv = pv_den[:, 0:2 * D_A] / pv_den[:, 2 * D_A:4 * D_A]
        o = pv[0:ATTN_SUB] - lam * pv[ATTN_SUB:2 * ATTN_SUB]
        y = o * lax.rsqrt(jnp.mean(o * o, axis=-1, keepdims=True) + EPS) * g_ref[...]
        o_ref[r * ATTN_SUB:(r + 1) * ATTN_SUB, :] = (y * (1.0 - lam_init)).astype(o_ref.dtype)


def _diff_attention(big, row0, B, N, layer, w_lam, w_subln, lam_init, cache=None, rope_tabs=None, tq=256,
                    ys=None, branch=0):
    rope = rope_tabs is not None
    n_cache = cache[0].shape[2] if cache is not None else 0
    nq = N // tq
    hw = 2 * D_A
    qc, kc, vc = OFF['a_q'] // hw, OFF['a_k'] // hw, OFF['a_v'] // hw
    in_specs = [pl.BlockSpec((tq, hw), lambda b, h, i: (row0 // tq + b * nq + i, qc + h)),
                pl.BlockSpec((N, hw), lambda b, h, i: (row0 // N + b, kc + h)),
                pl.BlockSpec((N, hw), lambda b, h, i: (row0 // N + b, vc + h)),
                pl.BlockSpec((None, 4, D_A), lambda b, h, i: (layer, 0, 0)),
                pl.BlockSpec((None, 1, hw), lambda b, h, i: (layer, 0, 0))]
    args = [big, big, big, w_lam, w_subln]
    if rope:
        cspec = pl.BlockSpec((None, None, n_cache, hw), lambda b, h, i: (b, layer, 0, h))
        in_specs += [cspec, cspec,
                     pl.BlockSpec((tq, hw), lambda b, h, i: (i, 0)), pl.BlockSpec((tq, hw), lambda b, h, i: (i, 0)),
                     pl.BlockSpec((N, hw), lambda b, h, i: (0, 0)), pl.BlockSpec((N, hw), lambda b, h, i: (0, 0))]
        args += [cache[0], cache[1], rope_tabs[0], rope_tabs[1], rope_tabs[0], rope_tabs[1]]
    out_spec, out_shape, x_specs, x_args, aliases = _branch_placement(
        ys, branch, row0 // tq, B * N, (tq, hw), lambda b, h, i: (b * nq + i, h))
    return pl.pallas_call(
        functools.partial(_diff_attn_kernel, n_lat=N, n_cache=n_cache, rope=rope, tq=tq, lam_init=lam_init,
                          aliased=ys is not None),
        grid=(B, H_A, nq),
        in_specs=in_specs + x_specs,
        out_specs=out_spec,
        out_shape=out_shape,
        input_output_aliases=aliases(len(args)),
        scratch_shapes=[pltpu.VMEM((N + n_cache, hw), BF16), pltpu.VMEM((N + n_cache, 2 * hw), BF16)],
        compiler_params=_cparams(("parallel", "parallel", "arbitrary")),
        name="diff_attention",
    )(*args, *x_args)


GQA_G = H_D // HKV_D


def _gqa_kernel(*refs, n_lat, n_cache, banded, tq, layer, aliased):
    refs = _drop_aliased(refs, 10 if banded else 4, aliased)
    if banded:
        (sink_ref, q_ref, k_ref, v_ref, ck_ref, cv_ref, cosq_ref, sinq_ref, cosk_ref, sink_tab_ref,
         o_ref, kbuf, vbuf) = refs
    else:
        sink_ref, q_ref, k_ref, v_ref, o_ref, kbuf, vbuf = refs
    W = WINDOW_D
    kv = pl.program_id(1)
    qi = pl.program_id(2)

    @pl.when(qi == 0)
    def _():
        k = k_ref[...]
        if banded:
            k = _rope_lanes(k, cosk_ref[...], sink_tab_ref[...], DH_D)
            zeros = jnp.zeros((W, DH_D), BF16)
            for buf, lat, cached in ((kbuf, k, ck_ref), (vbuf, v_ref[...], cv_ref)):
                buf[0:W, :] = zeros
                buf[W:W + n_lat, :] = lat.astype(BF16)
                buf[W + n_lat:2 * W + n_lat, :] = zeros
                buf[2 * W + n_lat:2 * W + n_lat + n_cache, :] = cached[...].astype(BF16)
        else:
            kbuf[...] = k.astype(BF16)
            vbuf[...] = v_ref[...].astype(BF16)

    q2 = q_ref[...]
    heads = [q2[:, g * DH_D:(g + 1) * DH_D] for g in range(GQA_G)]
    if banded:
        heads = [_rope_lanes(qh, cosq_ref[...], sinq_ref[...], DH_D) for qh in heads]
    qs = (jnp.concatenate(heads, axis=0) * (DH_D ** -0.5)).astype(BF16)
    row = lax.broadcasted_iota(jnp.int32, (GQA_G * tq, 1), 0)
    sink = jnp.full((GQA_G * tq, 1), sink_ref[layer, kv * GQA_G], F32)
    for g in range(1, GQA_G):
        sink = jnp.where(row >= g * tq, sink_ref[layer, kv * GQA_G + g], sink)

    if banded:
        start = pl.multiple_of(qi * W, W)
        s_loc = _dot_nt(qs, kbuf[pl.ds(start, 3 * W), :])
        r = lax.broadcasted_iota(jnp.int32, s_loc.shape, 0) & (tq - 1)
        c = lax.broadcasted_iota(jnp.int32, s_loc.shape, 1)
        kpos = qi * W + c - W
        valid = (jnp.abs(c - W - r) <= W) & (kpos >= 0) & (kpos < n_lat)
        s_loc = jnp.where(valid, s_loc, NEG_INF)
        s_ctx = _dot_nt(qs, kbuf[2 * W + n_lat:2 * W + n_lat + n_cache, :])
        m = jnp.maximum(jnp.maximum(jnp.max(s_loc, axis=-1, keepdims=True),
                                    jnp.max(s_ctx, axis=-1, keepdims=True)), sink)
        e_loc = jnp.exp(s_loc - m)
        e_ctx = jnp.exp(s_ctx - m)
        den = (jnp.sum(e_loc, axis=-1, keepdims=True) + jnp.sum(e_ctx, axis=-1, keepdims=True)
               + jnp.exp(sink - m))
        o = (_dot(e_loc.astype(BF16), vbuf[pl.ds(start, 3 * W), :])
             + _dot(e_ctx.astype(BF16), vbuf[2 * W + n_lat:2 * W + n_lat + n_cache, :])) / den
    else:
        s = _dot_nt(qs, kbuf[...])
        m = jnp.maximum(jnp.max(s, axis=-1, keepdims=True), sink)
        e = jnp.exp(s - m)
        den = jnp.sum(e, axis=-1, keepdims=True) + jnp.exp(sink - m)
        o = _dot(e.astype(BF16), vbuf[...]) / den
    o_ref[...] = jnp.concatenate([o[g * tq:(g + 1) * tq] for g in range(GQA_G)], axis=1).astype(o_ref.dtype)


def _gqa(big, row0, B, N, layer, w_sink, cache=None, rope_tabs=None, tq=WINDOW_D, ys=None, branch=0):
    banded = cache is not None
    n_cache = cache[0].shape[2] if banded else 0
    nq = N // tq
    qw = GQA_G * DH_D
    qc, kc, vc = OFF['d_q'] // qw, OFF['d_k'] // DH_D, OFF['d_v'] // DH_D
    in_specs = [pl.BlockSpec(memory_space=pltpu.SMEM),
                pl.BlockSpec((tq, qw), lambda b, h, i: (row0 // tq + b * nq + i, qc + h)),
                pl.BlockSpec((N, DH_D), lambda b, h, i: (row0 // N + b, kc + h)),
                pl.BlockSpec((N, DH_D), lambda b, h, i: (row0 // N + b, vc + h))]
    args = [w_sink, big, big, big]
    n_rows = N
    if banded:
        assert tq == WINDOW_D
        cspec = pl.BlockSpec((None, None, n_cache, DH_D), lambda b, h, i: (b, layer, 0, h))
        in_specs += [cspec, cspec,
                     pl.BlockSpec((tq, DH_D), lambda b, h, i: (i, 0)), pl.BlockSpec((tq, DH_D), lambda b, h, i: (i, 0)),
                     pl.BlockSpec((N, DH_D), lambda b, h, i: (0, 0)), pl.BlockSpec((N, DH_D), lambda b, h, i: (0, 0))]
        args += [cache[0], cache[1], rope_tabs[0], rope_tabs[1], rope_tabs[0], rope_tabs[1]]
        n_rows = N + 2 * WINDOW_D + n_cache
    out_spec, out_shape, x_specs, x_args, aliases = _branch_placement(
        ys, branch, row0 // tq, B * N, (tq, qw), lambda b, h, i: (b * nq + i, h))
    return pl.pallas_call(
        functools.partial(_gqa_kernel, n_lat=N, n_cache=n_cache, banded=banded, tq=tq, layer=layer,
                          aliased=ys is not None),
        grid=(B, HKV_D, nq),
        in_specs=in_specs + x_specs,
        out_specs=out_spec,
        out_shape=out_shape,
        input_output_aliases=aliases(len(args)),
        scratch_shapes=[pltpu.VMEM((n_rows, DH_D), BF16), pltpu.VMEM((n_rows, DH_D), BF16)],
        compiler_params=_cparams(("parallel", "parallel", "arbitrary")),
        name="gqa_sink_attention",
    )(*args, *x_args)


def _mlstm_chunk(q, k, v, i_col, lf_col, i_row, lf_row, C, n_row, m, rev):
    L = CHUNK_B
    t_i = lax.broadcasted_iota(jnp.int32, (L, L), 0)
    s_i = lax.broadcasted_iota(jnp.int32, (L, L), 1)
    tri = (s_i >= t_i) if rev else (s_i <= t_i)
    tri_t = (t_i >= s_i) if rev else (t_i <= s_i)
    b_col = jnp.sum(jnp.where(tri, lf_row, 0.0), axis=1, keepdims=True)
    b_row = jnp.sum(jnp.where(tri_t, lf_col, 0.0), axis=0, keepdims=True)
    a_col = b_col + m
    d = jnp.where(tri, b_col - b_row + i_row, NEG_INF)
    m_t = jnp.maximum(a_col, jnp.max(d, axis=1, keepdims=True))
    w_inter = jnp.exp(a_col - m_t)
    w_intra = jnp.exp(d - m_t)
    qb, kb, vb = q.astype(BF16), k.astype(BF16), v.astype(BF16)
    k_scale = DK_B ** -0.5
    qk = _dot_nt(qb, kb) * k_scale * w_intra
    num = w_inter * _dot(qb, C.astype(BF16)) + _dot(qk.astype(BF16), vb)
    den = w_inter * jnp.sum(q * n_row, axis=1, keepdims=True) + jnp.sum(qk, axis=1, keepdims=True)
    h = num / jnp.maximum(jnp.abs(den), jnp.exp(-m_t))
    last = 0 if rev else L - 1
    b_last = b_col[last:last + 1]
    m_last = m_t[last:last + 1]
    kw = k * (k_scale * jnp.exp(b_last - b_col + i_col - m_last))
    decay = w_inter[last:last + 1]
    C_new = decay * C + _dot_tn(kw.astype(BF16), vb)
    n_new = decay * n_row + jnp.sum(kw, axis=0, keepdims=True)
    return h, C_new, n_new, m_last


def _mlstm_kernel(*refs, zero_init, emit_state):
    refs = list(refs)
    qf, kf, vf, qb, kb, vb, gcf, gcb, grf, grb = refs[:10]
    pos = 10
    if not zero_init:
        c0_ref, n0_ref, m0_ref = refs[pos:pos + 3]
        pos += 3
    hf_ref, hb_ref = refs[pos:pos + 2]
    pos += 2
    if emit_state:
        co_ref, no_ref, mo_ref = refs[pos:pos + 3]
        pos += 3
    c_s, n_s, m_s = refs[pos:pos + 3]
    c = pl.program_id(1)

    @pl.when(c == 0)
    def _():
        if zero_init:
            c_s[...] = jnp.zeros_like(c_s)
            n_s[...] = jnp.zeros_like(n_s)
            m_s[...] = jnp.zeros_like(m_s)
        else:
            c_s[...] = c0_ref[...]
            n_s[...] = n0_ref[...]
            m_s[...] = m0_ref[...]

    streams = ((0, qf, kf, vf, gcf, grf, hf_ref), (1, qb, kb, vb, gcb, grb, hb_ref))
    for dr, q_ref, k_ref, v_ref, gc_ref, gr_ref, h_ref in streams:
        gc = gc_ref[...]
        gr = gr_ref[...]
        lf_c = _log_sigmoid(gc)
        lf_r = _log_sigmoid(gr)
        for hd in range(H_B):
            i_at = dr * 2 * H_B + hd
            f_at = i_at + H_B
            h, c_new, n_new, m_new = _mlstm_chunk(
                q_ref[:, hd * DK_B:(hd + 1) * DK_B], k_ref[:, hd * DK_B:(hd + 1) * DK_B],
                v_ref[:, hd * DV_B:(hd + 1) * DV_B],
                gc[:, i_at:i_at + 1], lf_c[:, f_at:f_at + 1], gr[i_at:i_at + 1, :], lf_r[f_at:f_at + 1, :],
                c_s[dr, hd], n_s[dr, hd], m_s[dr, hd], rev=bool(dr))
            h_ref[:, hd * DV_B:(hd + 1) * DV_B] = h
            c_s[dr, hd] = c_new
            n_s[dr, hd] = n_new
            m_s[dr, hd] = m_new

    if emit_state:
        @pl.when(c == pl.num_programs(1) - 1)
        def _():
            co_ref[...] = c_s[...]
            no_ref[...] = n_s[...]
            mo_ref[...] = m_s[...]


def _mlstm(big, gate_cols, gate_rows, row0, B, N, layer, state=None, emit_state=False):
    L = CHUNK_B
    nc = N // L
    r0 = row0 // L
    qw, vw = H_B * DK_B, H_B * DV_B
    qc, kc, vc = OFF['b_q'] // qw, OFF['b_k'] // qw, OFF['b_v'] // vw
    fwd = lambda b, c: r0 + b * nc + c
    bwd = lambda b, c: r0 + b * nc + (nc - 1 - c)
    in_specs, args = [], []
    for rowf in (fwd, bwd):
        for col, width in ((qc, qw), (kc, qw), (vc, vw)):
            in_specs.append(pl.BlockSpec((L, width), lambda b, c, rowf=rowf, col=col: (rowf(b, c), col)))
            args.append(big)
    for rowf in (fwd, bwd):
        in_specs.append(pl.BlockSpec((L, N_B_IF), lambda b, c, rowf=rowf: (rowf(b, c), 0)))
        args.append(gate_cols)
    for rowf in (fwd, bwd):
        in_specs.append(pl.BlockSpec((N_B_IF, L), lambda b, c, rowf=rowf: (0, rowf(b, c))))
        args.append(gate_rows)
    if state is not None:
        in_specs += [pl.BlockSpec((None, None, 2, H_B, DK_B, DV_B), lambda b, c: (b, layer, 0, 0, 0, 0)),
                     pl.BlockSpec((None, None, 2, H_B, 1, DK_B), lambda b, c: (b, layer, 0, 0, 0, 0)),
                     pl.BlockSpec((None, None, 2, H_B, 1, 1), lambda b, c: (b, layer, 0, 0, 0, 0))]
        args += list(state)
    out_specs = [pl.BlockSpec((L, vw), lambda b, c: (b * nc + c, 0)),
                 pl.BlockSpec((L, vw), lambda b, c: (b * nc + (nc - 1 - c), 0))]
    out_shape = [jax.ShapeDtypeStruct((B * N, vw), F32)] * 2
    if emit_state:
        out_specs += [pl.BlockSpec((None, 2, H_B, DK_B, DV_B), lambda b, c: (b, 0, 0, 0, 0)),
                      pl.BlockSpec((None, 2, H_B, 1, DK_B), lambda b, c: (b, 0, 0, 0, 0)),
                      pl.BlockSpec((None, 2, H_B, 1, 1), lambda b, c: (b, 0, 0, 0, 0))]
        out_shape += [jax.ShapeDtypeStruct((B, 2, H_B, DK_B, DV_B), F32),
                      jax.ShapeDtypeStruct((B, 2, H_B, 1, DK_B), F32),
                      jax.ShapeDtypeStruct((B, 2, H_B, 1, 1), F32)]
    return pl.pallas_call(
        functools.partial(_mlstm_kernel, zero_init=state is None, emit_state=emit_state),
        grid=(B, nc),
        in_specs=in_specs,
        out_specs=out_specs,
        out_shape=out_shape,
        scratch_shapes=[pltpu.VMEM((2, H_B, DK_B, DV_B), F32), pltpu.VMEM((2, H_B, 1, DK_B), F32),
                        pltpu.VMEM((2, H_B, 1, 1), F32)],
        compiler_params=_cparams(("parallel", "arbitrary")),
        name="mlstm_scan",
    )(*args)


def _prefix_sum_rows(tri_bf, x):
    x_hi = x.astype(BF16)
    r1 = x - x_hi.astype(F32)
    x_mid = r1.astype(BF16)
    x_lo = (r1 - x_mid.astype(F32)).astype(BF16)
    return _dot(tri_bf, x_hi) + _dot(tri_bf, x_mid) + _dot(tri_bf, x_lo)


def _gla_chunk(q, k, v, gc, s_t, rev):
    L, SB = CHUNK_C, SUB_C
    t_i = lax.broadcasted_iota(jnp.int32, (L, L), 0)
    s_i = lax.broadcasted_iota(jnp.int32, (L, L), 1)
    tri = (s_i >= t_i) if rev else (s_i <= t_i)
    bc = _prefix_sum_rows(jnp.where(tri, 1.0, 0.0).astype(BF16), gc)
    qs = q * (DK_C ** -0.5)
    inter = _dot_nt((qs * jnp.exp(bc)).astype(BF16), s_t.astype(BF16))
    col = lax.broadcasted_iota(jnp.int32, (SB, L), 1)
    blocks = []
    for ti in range(L // SB):
        lo, hi = ti * SB, (ti + 1) * SB
        ref_row = bc[hi - 1:hi] if rev else bc[lo:lo + 1]
        q_blk = qs[lo:hi]
        bc_blk = bc[lo:hi]
        q_dec = (q_blk * jnp.exp(bc_blk - ref_row)).astype(BF16)
        k_dec = (k * jnp.exp(jnp.minimum(ref_row - bc, 0.0))).astype(BF16)
        earlier = (col >= hi) if rev else (col < lo)
        a_blk = jnp.where(earlier, _dot_nt(q_dec, k_dec), 0.0)
        t_abs = lax.broadcasted_iota(jnp.int32, (SB, 1), 0) + lo
        for s in range(lo, hi):
            dec = jnp.exp(jnp.minimum(bc_blk - bc[s:s + 1], 0.0))
            a_col = jnp.sum(q_blk * k[s:s + 1] * dec, axis=-1, keepdims=True)
            ok = (t_abs <= s) if rev else (t_abs >= s)
            a_blk = jnp.where((col == s) & ok, a_col, a_blk)
        blocks.append(a_blk)
    a = jnp.concatenate(blocks, axis=0)
    vb = v.astype(BF16)
    o = inter + _dot(a.astype(BF16), vb)
    b_last = bc[0:1] if rev else bc[L - 1:L]
    k_dec = (k * jnp.exp(b_last - bc)).astype(BF16)
    s_new = jnp.exp(b_last) * s_t + _dot_tn(vb, k_dec)
    return o, s_new


def _gla_kernel(*refs, zero_init, emit_state):
    refs = list(refs)
    qf, kf, vf, sf, qb, kb, vb, sb, wup_ref, bup_ref = refs[:10]
    pos = 10
    if not zero_init:
        s0_ref = refs[pos]
        pos += 1
    of_ref, ob_ref = refs[pos:pos + 2]
    pos += 2
    if emit_state:
        so_ref = refs[pos]
        pos += 1
    st_s = refs[pos]
    c = pl.program_id(1)

    @pl.when(c == 0)
    def _():
        for dr in range(2):
            for hd in range(H_C):
                if zero_init:
                    st_s[dr, hd] = jnp.zeros((DV_C, DK_C), F32)
                else:
                    st_s[dr, hd] = s0_ref[dr, hd].T

    streams = ((0, qf, kf, vf, sf, of_ref), (1, qb, kb, vb, sb, ob_ref))
    for dr, q_ref, k_ref, v_ref, sm_ref, o_ref in streams:
        z = _dot(sm_ref[...].astype(BF16), wup_ref[dr]) + bup_ref[dr]
        gc = _log_sigmoid(z) * (1.0 / GATE_TAU_C)
        for hd in range(H_C):
            ks = slice(hd * DK_C, (hd + 1) * DK_C)
            vs = slice(hd * DV_C, (hd + 1) * DV_C)
            o, s_new = _gla_chunk(q_ref[:, ks], k_ref[:, ks], v_ref[:, vs], gc[:, ks], st_s[dr, hd], rev=bool(dr))
            o_ref[:, vs] = o
            st_s[dr, hd] = s_new

    if emit_state:
        @pl.when(c == pl.num_programs(1) - 1)
        def _():
            for dr in range(2):
                for hd in range(H_C):
                    so_ref[dr, hd] = st_s[dr, hd].T


def _gla(big, small, w_up_pad, b_up, row0, B, N, layer, state=None, emit_state=False):
    L = CHUNK_C
    nc = N // L
    r0 = row0 // L
    qw, vw = H_C * DK_C, H_C * DV_C
    qc, kc, vc = OFF['c_q'] // qw, OFF['c_k'] // qw, OFF['c_v'] // vw
    fwd = lambda b, c: r0 + b * nc + c
    bwd = lambda b, c: r0 + b * nc + (nc - 1 - c)
    in_specs, args = [], []
    for rowf in (fwd, bwd):
        for col, width in ((qc, qw), (kc, qw), (vc, vw)):
            in_specs.append(pl.BlockSpec((L, width), lambda b, c, rowf=rowf, col=col: (rowf(b, c), col)))
            args.append(big)
        in_specs.append(pl.BlockSpec((L, D_SMALL), lambda b, c, rowf=rowf: (rowf(b, c), 0)))
        args.append(small)
    in_specs += [pl.BlockSpec((None, 2, D_SMALL, qw), lambda b, c: (layer, 0, 0, 0)),
                 pl.BlockSpec((None, 2, 1, qw), lambda b, c: (layer, 0, 0, 0))]
    args += [w_up_pad, b_up]
    if state is not None:
        in_specs.append(pl.BlockSpec((None, None, 2, H_C, DK_C, DV_C), lambda b, c: (b, layer, 0, 0, 0, 0)))
        args.append(state)
    out_specs = [pl.BlockSpec((L, vw), lambda b, c: (b * nc + c, 0)),
                 pl.BlockSpec((L, vw), lambda b, c: (b * nc + (nc - 1 - c), 0))]
    out_shape = [jax.ShapeDtypeStruct((B * N, vw), F32)] * 2
    if emit_state:
        out_specs.append(pl.BlockSpec((None, 2, H_C, DK_C, DV_C), lambda b, c: (b, 0, 0, 0, 0)))
        out_shape.append(jax.ShapeDtypeStruct((B, 2, H_C, DK_C, DV_C), F32))
    return pl.pallas_call(
        functools.partial(_gla_kernel, zero_init=state is None, emit_state=emit_state),
        grid=(B, nc),
        in_specs=in_specs,
        out_specs=out_specs,
        out_shape=out_shape,
        scratch_shapes=[pltpu.VMEM((2, H_C, DV_C, DK_C), F32)],
        compiler_params=_cparams(("parallel", "arbitrary")),
        name="gla_scan",
    )(*args)


def _gated_norm_kernel(*refs, silu, aliased):
    hf_ref, hb_ref, pre_ref, g_ref, o_ref = _drop_aliased(refs, 4, aliased)
    h = hf_ref[...] + hb_ref[...]
    y = h * lax.rsqrt(jnp.mean(h * h, axis=-1, keepdims=True) + EPS) * g_ref[...]
    pre = pre_ref[...]
    gate = jax.nn.sigmoid(pre)
    if silu:
        gate = pre * gate
    o_ref[...] = (y * gate).astype(o_ref.dtype)


def _gated_norm(hf, hb, big, pre_off, row0, w_norm, layer, silu, tm=512, ys=None, branch=0):
    T = hf.shape[0]
    hw = DV_B
    blk = pl.BlockSpec((tm, hw), lambda i, h: (i, h))
    args = [hf, hb, big, w_norm]
    out_spec, out_shape, x_specs, x_args, aliases = _branch_placement(
        ys, branch, row0 // tm, T, (tm, hw), lambda i, h: (i, h))
    return pl.pallas_call(
        functools.partial(_gated_norm_kernel, silu=silu, aliased=ys is not None),
        grid=(T // tm, BRANCH_W // hw),
        in_specs=[blk, blk,
                  pl.BlockSpec((tm, hw), lambda i, h: (row0 // tm + i, pre_off // hw + h)),
                  pl.BlockSpec((None, 1, hw), lambda i, h: (layer, 0, h))] + x_specs,
        out_specs=out_spec,
        out_shape=out_shape,
        input_output_aliases=aliases(len(args)),
        compiler_params=_cparams(("parallel", "parallel")),
        name="gated_head_norm",
    )(*args, *x_args)


def _modulation_tables(c_ctx, c, w_ada_l, b_ada_l):
    cond = jnp.concatenate([c_ctx[None, :], c], axis=0)
    m = jnp.dot(jax.nn.silu(cond), w_ada_l, precision=lax.Precision.HIGHEST) + b_ada_l
    return m.reshape(N_SEG, 6, 1, D_MODEL)


def _mixers(big, small, l, lam_init, shared, caches, ys):
    (w_lam, w_subln, gate_bias, w_b_norm, w_up_pad, b_up, w_c_norm, w_sink, rope_a, rope_d) = shared
    (ca_k, ca_v, sb_c, sb_n, sb_m, sc_s, cd_k, cd_v) = caches
    gate_cols = small[:, :N_B_IF] + gate_bias[l]
    gate_rows = gate_cols.T

    ys = _diff_attention(big, 0, BATCH, SEQ, l, w_lam, w_subln, lam_init, ys=ys, branch=0)
    ys = _diff_attention(big, T_CTX, DEC_BATCH, DEC_SEQ, l, w_lam, w_subln, lam_init, cache=(ca_k, ca_v),
                         rope_tabs=rope_a, ys=ys, branch=0)

    hf_c, hb_c, c_out, n_out, m_out = _mlstm(big, gate_cols, gate_rows, 0, BATCH, SEQ, l, emit_state=True)
    hf_l, hb_l = _mlstm(big, gate_cols, gate_rows, T_CTX, DEC_BATCH, DEC_SEQ, l, state=(sb_c, sb_n, sb_m))
    ys = _gated_norm(hf_c, hb_c, big, OFF['b_o'], 0, w_b_norm, l, silu=False, ys=ys, branch=1)
    ys = _gated_norm(hf_l, hb_l, big, OFF['b_o'], T_CTX, w_b_norm, l, silu=False, ys=ys, branch=1)

    of_c, ob_c, s_out = _gla(big, small, w_up_pad, b_up, 0, BATCH, SEQ, l, emit_state=True)
    of_l, ob_l = _gla(big, small, w_up_pad, b_up, T_CTX, DEC_BATCH, DEC_SEQ, l, state=sc_s)
    ys = _gated_norm(of_c, ob_c, big, OFF['c_r'], 0, w_c_norm, l, silu=True, ys=ys, branch=2)
    ys = _gated_norm(of_l, ob_l, big, OFF['c_r'], T_CTX, w_c_norm, l, silu=True, ys=ys, branch=2)

    ys = _gqa(big, 0, BATCH, SEQ, l, w_sink, tq=SEQ, ys=ys, branch=3)
    ys = _gqa(big, T_CTX, DEC_BATCH, DEC_SEQ, l, w_sink, cache=(cd_k, cd_v), rope_tabs=rope_d, ys=ys, branch=3)

    ctx_rows = big[:T_CTX]
    ctx = (ctx_rows[:, OFF['a_k']:OFF['a_k'] + 1024].reshape(BATCH, SEQ, H_A, 2 * D_A),
           ctx_rows[:, OFF['a_v']:OFF['a_v'] + 1024].reshape(BATCH, SEQ, H_A, 2 * D_A),
           c_out, n_out.reshape(BATCH, 2, H_B, DK_B), m_out.reshape(BATCH, 2, H_B), s_out,
           ctx_rows[:, OFF['d_k']:OFF['d_k'] + 512].reshape(BATCH, SEQ, HKV_D, DH_D),
           ctx_rows[:, OFF['d_v']:OFF['d_v'] + 512].reshape(BATCH, SEQ, HKV_D, DH_D))
    return ys, ctx


def kernel(x_prompt, x_sample, c, cache_a_k, cache_a_v, state_b_C, state_b_n, state_b_m, state_c_S, cache_d_k, cache_d_v, c_ctx, w_ada, b_ada, w_norm1, w_norm2, w_in, w_a_lambda, w_a_subln, b_b_if, w_b_norm, w_c_alpha_up, b_c_alpha, w_c_norm, w_d_sink, w_branch, w_merge_gate, w_out, w_router_group, b_router_group, w_router_expert, b_router_expert, w_exp_in, w_exp_out, w_norm_f):
    rows = DEC_SEQ // GRID_W
    rope_a = _rope_lane_tables(rows, D_A)
    rope_d = _rope_lane_tables(rows, DH_D)

    w_big = jnp.concatenate([w_in[:, :, :OFF_B_IF], w_in[:, :, OFF_C_Q:OFF_C_GLR], w_in[:, :, OFF_D_Q:]],
                            axis=-1).astype(BF16)
    w_small = jnp.concatenate([w_in[:, :, OFF_B_IF:OFF_C_Q], w_in[:, :, OFF_C_GLR:OFF_D_Q],
                               jnp.zeros((DEPTH, D_MODEL, D_SMALL - N_B_IF - N_C_GLR), F32)], axis=-1).astype(BF16)
    w_gate_bf = w_merge_gate.astype(BF16)
    w_branch_bf = w_branch.astype(BF16)
    w_out_bf = w_out.astype(BF16)
    w_exp_in_bf = w_exp_in.astype(BF16)
    w_exp_out_bf = w_exp_out.astype(BF16)

    w_up_pad = jnp.zeros((DEPTH, 2, D_SMALL, H_C * DK_C), F32)
    for d in range(2):
        lo = N_B_IF + d * GATE_RANK_C
        w_up_pad = w_up_pad.at[:, d, lo:lo + GATE_RANK_C, :].set(w_c_alpha_up[:, d])
    w_up_pad = w_up_pad.astype(BF16)

    shared = (w_a_lambda, w_a_subln.reshape(DEPTH, 1, 2 * D_A), b_b_if.reshape(DEPTH, N_B_IF),
              w_b_norm.reshape(DEPTH, 1, H_B * DV_B), w_up_pad, b_c_alpha.reshape(DEPTH, 2, 1, H_C * DK_C),
              w_c_norm.reshape(DEPTH, 1, H_C * DV_C), w_d_sink, rope_a, rope_d)
    caches = (cache_a_k.reshape(DEC_BATCH, DEPTH, PAST_LEN, H_A * 2 * D_A),
              cache_a_v.reshape(DEC_BATCH, DEPTH, PAST_LEN, H_A * 2 * D_A),
              state_b_C, state_b_n.reshape(DEC_BATCH, DEPTH, 2, H_B, 1, DK_B),
              state_b_m.reshape(DEC_BATCH, DEPTH, 2, H_B, 1, 1), state_c_S,
              cache_d_k.reshape(DEC_BATCH, DEPTH, PAST_LEN, HKV_D * DH_D),
              cache_d_v.reshape(DEC_BATCH, DEPTH, PAST_LEN, HKV_D * DH_D))

    x = jnp.concatenate([x_prompt.reshape(T_CTX, D_MODEL), x_sample.reshape(T_LAT, D_MODEL)], axis=0)
    ys = jnp.zeros((N_BRANCH, T_ALL, BRANCH_W), BF16)
    ctx_out = [[] for _ in range(8)]
    for l in range(DEPTH):
        lw = {'w_exp_in': w_exp_in_bf, 'w_exp_out': w_exp_out_bf}
        w_router3, b_router = _router_operands(w_router_group[l], b_router_group[l],
                                               w_router_expert[l], b_router_expert[l])
        lam_init = 0.8 - 0.6 * math.exp(-0.3 * l)
        mods = _modulation_tables(c_ctx, c, w_ada[l], b_ada[l])
        sh1, sc1, g1, sh2, sc2, g2 = [mods[:, i] for i in range(6)]

        u = _norm_mod(x, w_norm1[l][None, :], sh1, sc1)
        big = _mm(u, w_big, l, F32)
        small = _mm(u, w_small, l, F32, tn=D_SMALL)

        ys, ctx = _mixers(big, small, l, lam_init, shared, caches, ys)
        for lst, t in zip(ctx_out, ctx):
            lst.append(t)

        merged = _merge(u, ys, w_gate_bf, w_branch_bf, l)
        x = _mm_residual(merged, w_out_bf, l, x, g1)

        u2, logits = _norm_mod_route(x, w_norm2[l][None, :], sh2, sc2, w_router3, b_router)
        ya, yb = _hier_moe(u2, logits, lw, l)
        x = _residual_pair(x, g2, ya, yb)

    y = _norm(x, w_norm_f[None, :])
    y_prompt = y[:T_CTX].reshape(BATCH, SEQ, D_MODEL)
    y_sample = y[T_CTX:].reshape(DEC_BATCH, DEC_SEQ, D_MODEL)
    outs = [jnp.stack(t, axis=1) for t in ctx_out]
    return (y_prompt, y_sample) + tuple(outs)
```

```python
import functools
import math

import jax
import jax.numpy as jnp
from jax import lax
from jax.experimental import pallas as pl
from jax.experimental.pallas import tpu as pltpu

D_MODEL = 4096
BATCH = 32
SEQ = 256
DEPTH = 4
DEC_BATCH = 8
DEC_SEQ = 2048
PAST_LEN = 512
GRID_W = 64
ROPE_BASE = 10000.0
EPS = 1e-6
NEG_INF = -1e30
H_A = 8
D_A = 64
H_B = 4
DK_B = 256
DV_B = 256
CHUNK_B = 128
H_C = 4
DK_C = 128
DV_C = 256
GATE_RANK_C = 16
GATE_TAU_C = 16.0
CHUNK_C = 64
SUB_C = 16
H_D = 8
HKV_D = 4
DH_D = 128
WINDOW_D = 128
N_BRANCH = 4
BRANCH_W = 1024
N_GROUPS = 4
EXPERTS_PER_GROUP = 4
N_EXPERTS = N_GROUPS * EXPERTS_PER_GROUP
TOP_K_INNER = 2
D_EXPERT = 1024

F32 = jnp.float32
BF16 = jnp.bfloat16
LANES = 128

T_CTX = BATCH * SEQ
T_LAT = DEC_BATCH * DEC_SEQ
T_ALL = T_CTX + T_LAT
N_SEG = 1 + DEC_BATCH

BIG_PARTS = (
    ('a_q', 1024), ('a_k', 1024), ('a_v', 1024),
    ('b_q', 1024), ('b_k', 1024), ('b_v', 1024), ('b_o', 1024),
    ('c_q', 512), ('c_k', 512), ('c_v', 1024), ('c_r', 1024),
    ('d_q', 1024), ('d_k', 512), ('d_v', 512),
)
D_BIG = sum(w for _, w in BIG_PARTS)
OFF = {}
_o = 0
for _name, _w in BIG_PARTS:
    OFF[_name] = _o
    _o += _w
N_B_IF = 2 * 2 * H_B
N_C_GLR = 2 * GATE_RANK_C
D_SMALL = LANES
OFF_B_IF = 7 * 1024
OFF_C_Q = OFF_B_IF + N_B_IF
OFF_C_GLR = OFF_C_Q + 512 + 512 + 1024 + 1024
OFF_D_Q = OFF_C_GLR + N_C_GLR

VMEM_LIMIT = 56 * 1024 * 1024


def _cparams(sem):
    return pltpu.CompilerParams(dimension_semantics=sem, vmem_limit_bytes=VMEM_LIMIT)


def _seg_of_tile(i, tm):
    n_ctx = T_CTX // tm
    per_lat = DEC_SEQ // tm
    return jnp.where(i < n_ctx, 0, 1 + (i - n_ctx) // per_lat)


def _dot(a, b):
    return jnp.dot(a, b, preferred_element_type=F32)


def _dot_nt(a, b):
    return lax.dot_general(a, b, (((1,), (1,)), ((), ())), preferred_element_type=F32)


def _dot_tn(a, b):
    return lax.dot_general(a, b, (((0,), (0,)), ((), ())), preferred_element_type=F32)


def _log_sigmoid(x):
    return jnp.minimum(x, 0.0) - jnp.log1p(jnp.exp(-jnp.abs(x)))


def _norm_mod_kernel(x_ref, g_ref, sh_ref, sc_ref, o_ref):
    x = x_ref[...]
    y = x * lax.rsqrt(jnp.mean(x * x, axis=-1, keepdims=True) + EPS) * g_ref[...]
    o_ref[...] = (y * (1.0 + sc_ref[...]) + sh_ref[...]).astype(o_ref.dtype)


def _norm_mod(x, g, shift, scale, tm=256):
    T, D = x.shape
    seg = lambda i: (_seg_of_tile(i, tm), 0, 0)
    return pl.pallas_call(
        _norm_mod_kernel,
        grid=(T // tm,),
        in_specs=[pl.BlockSpec((tm, D), lambda i: (i, 0)),
                  pl.BlockSpec((1, D), lambda i: (0, 0)),
                  pl.BlockSpec((None, 1, D), seg),
                  pl.BlockSpec((None, 1, D), seg)],
        out_specs=pl.BlockSpec((tm, D), lambda i: (i, 0)),
        out_shape=jax.ShapeDtypeStruct((T, D), BF16),
        compiler_params=_cparams(("parallel",)),
        name="norm_mod",
    )(x, g, shift, scale)


def _norm_mod_route_kernel(x_ref, g_ref, sh_ref, sc_ref, wr_ref, br_ref, o_ref, lg_ref):
    x = x_ref[...]
    y = x * lax.rsqrt(jnp.mean(x * x, axis=-1, keepdims=True) + EPS) * g_ref[...]
    u = (y * (1.0 + sc_ref[...]) + sh_ref[...]).astype(o_ref.dtype)
    o_ref[...] = u
    lg_ref[...] = _dot(u, wr_ref[0]) + _dot(u, wr_ref[1]) + _dot(u, wr_ref[2]) + br_ref[...]


def _norm_mod_route(x, g, shift, scale, w_router3, b_router, tm=256):
    T, D = x.shape
    seg = lambda i: (_seg_of_tile(i, tm), 0, 0)
    return pl.pallas_call(
        _norm_mod_route_kernel,
        grid=(T // tm,),
        in_specs=[pl.BlockSpec((tm, D), lambda i: (i, 0)),
                  pl.BlockSpec((1, D), lambda i: (0, 0)),
                  pl.BlockSpec((None, 1, D), seg),
                  pl.BlockSpec((None, 1, D), seg),
                  pl.BlockSpec((3, D, LANES), lambda i: (0, 0, 0)),
                  pl.BlockSpec((1, LANES), lambda i: (0, 0))],
        out_specs=[pl.BlockSpec((tm, D), lambda i: (i, 0)), pl.BlockSpec((tm, LANES), lambda i: (i, 0))],
        out_shape=[jax.ShapeDtypeStruct((T, D), BF16), jax.ShapeDtypeStruct((T, LANES), F32)],
        compiler_params=_cparams(("parallel",)),
        name="norm_mod_route",
    )(x, g, shift, scale, w_router3, b_router)


def _split3_bf16(w):
    hi = w.astype(BF16)
    r1 = w - hi.astype(F32)
    mid = r1.astype(BF16)
    lo = (r1 - mid.astype(F32)).astype(BF16)
    return jnp.stack([hi, mid, lo], axis=0)


def _norm_kernel(x_ref, g_ref, o_ref):
    x = x_ref[...]
    o_ref[...] = x * lax.rsqrt(jnp.mean(x * x, axis=-1, keepdims=True) + EPS) * g_ref[...]


def _norm(x, g, tm=256):
    T, D = x.shape
    return pl.pallas_call(
        _norm_kernel,
        grid=(T // tm,),
        in_specs=[pl.BlockSpec((tm, D), lambda i: (i, 0)),
                  pl.BlockSpec((1, D), lambda i: (0, 0))],
        out_specs=pl.BlockSpec((tm, D), lambda i: (i, 0)),
        out_shape=jax.ShapeDtypeStruct((T, D), F32),
        compiler_params=_cparams(("parallel",)),
        name="final_norm",
    )(x, g)


def _mm_kernel(a_ref, w_ref, o_ref):
    o_ref[...] = _dot(a_ref[...], w_ref[...]).astype(o_ref.dtype)


def _mm(a, w, layer, out_dtype, tm=1024, tn=1024):
    T, K = a.shape
    N = w.shape[-1]
    return pl.pallas_call(
        _mm_kernel,
        grid=(T // tm, N // tn),
        in_specs=[pl.BlockSpec((tm, K), lambda i, j: (i, 0)),
                  pl.BlockSpec((None, K, tn), lambda i, j: (layer, 0, j))],
        out_specs=pl.BlockSpec((tm, tn), lambda i, j: (i, j)),
        out_shape=jax.ShapeDtypeStruct((T, N), out_dtype),
        compiler_params=_cparams(("parallel", "parallel")),
        name="proj_mm",
    )(a, w)


def _mm_res_kernel(a_ref, w_ref, x_ref, g_ref, o_ref):
    o_ref[...] = x_ref[...] + g_ref[...] * _dot(a_ref[...], w_ref[...])


def _mm_residual(a, w, layer, x, gate, tm=1024, tn=512):
    T, K = a.shape
    N = w.shape[-1]
    return pl.pallas_call(
        _mm_res_kernel,
        grid=(T // tm, N // tn),
        in_specs=[pl.BlockSpec((tm, K), lambda i, j: (i, 0)),
                  pl.BlockSpec((None, K, tn), lambda i, j: (layer, 0, j)),
                  pl.BlockSpec((tm, tn), lambda i, j: (i, j)),
                  pl.BlockSpec((None, 1, tn), lambda i, j: (_seg_of_tile(i, tm), 0, j))],
        out_specs=pl.BlockSpec((tm, tn), lambda i, j: (i, j)),
        out_shape=jax.ShapeDtypeStruct((T, N), F32),
        compiler_params=_cparams(("parallel", "parallel")),
        name="out_proj_residual",
    )(a, w, x, gate)


def _merge_kernel(u_ref, wg_ref, y_ref, wb_ref, o_ref, acc_ref):
    b = pl.program_id(2)
    gate = jax.nn.sigmoid(_dot(u_ref[...], wg_ref[...]))
    term = gate * _dot(y_ref[...], wb_ref[...])

    @pl.when(b == 0)
    def _():
        acc_ref[...] = term

    @pl.when(b > 0)
    def _():
        acc_ref[...] += term

    @pl.when(b == N_BRANCH - 1)
    def _():
        o_ref[...] = acc_ref[...].astype(o_ref.dtype)


def _merge(u, ys, w_gate, w_branch, layer, tm=1024, tn=512):
    T, D = u.shape
    nj = D // tn
    return pl.pallas_call(
        _merge_kernel,
        grid=(T // tm, nj, N_BRANCH),
        in_specs=[pl.BlockSpec((tm, D), lambda i, j, b: (i, 0)),
                  pl.BlockSpec((None, D, tn), lambda i, j, b: (layer, 0, b * nj + j)),
                  pl.BlockSpec((None, tm, BRANCH_W), lambda i, j, b: (b, i, 0)),
                  pl.BlockSpec((None, None, BRANCH_W, tn), lambda i, j, b: (layer, b, 0, j))],
        out_specs=pl.BlockSpec((tm, tn), lambda i, j, b: (i, j)),
        out_shape=jax.ShapeDtypeStruct((T, D), BF16),
        scratch_shapes=[pltpu.VMEM((tm, tn), F32)],
        compiler_params=_cparams(("parallel", "parallel", "arbitrary")),
        name="branch_merge",
    )(u, w_gate, ys, w_branch)


MOE_TM = 512
MOE_NA = 2
MOE_NB = 2


def _moe_up_kernel(te_ref, nt_ref, x_ref, w1_ref, w2_ref, act_ref):
    used = pl.program_id(1) < nt_ref[0]

    @pl.when(used)
    def _():
        x = x_ref[...]
        h1 = _dot(x, w1_ref[...])
        h2 = _dot(x, w2_ref[...])
        act_ref[...] = (h1 * jax.nn.sigmoid(h1) * h2).astype(act_ref.dtype)

    @pl.when(jnp.logical_not(used))
    def _():
        act_ref[...] = jnp.zeros_like(act_ref)


def _moe_down_kernel(te_ref, nt_ref, act_ref, wo_ref, rs_ref, o_ref):
    used = pl.program_id(1) < nt_ref[0]

    @pl.when(used)
    def _():
        o_ref[...] = (_dot(act_ref[...], wo_ref[...]) * rs_ref[...]).astype(o_ref.dtype)

    @pl.when(jnp.logical_not(used))
    def _():
        o_ref[...] = jnp.zeros_like(o_ref)


def _moe_experts(xs, tile_expert, n_tiles, row_scale, w_in, w_out, layer):
    rows, D = xs.shape
    tiles = rows // MOE_TM
    hc = D_EXPERT // MOE_NA
    cb = D // MOE_NB
    clamp = lambda j, nt: jnp.minimum(j, nt[0] - 1)

    act = pl.pallas_call(
        _moe_up_kernel,
        grid_spec=pltpu.PrefetchScalarGridSpec(
            num_scalar_prefetch=2,
            grid=(MOE_NA, tiles),
            in_specs=[pl.BlockSpec((MOE_TM, D), lambda c, j, te, nt: (clamp(j, nt), 0)),
                      pl.BlockSpec((None, None, D, hc), lambda c, j, te, nt: (layer, te[j], 0, c)),
                      pl.BlockSpec((None, None, D, hc), lambda c, j, te, nt: (layer, te[j], 0, c + MOE_NA))],
            out_specs=pl.BlockSpec((MOE_TM, hc), lambda c, j, te, nt: (j, c)),
        ),
        out_shape=jax.ShapeDtypeStruct((rows, D_EXPERT), BF16),
        compiler_params=_cparams(("arbitrary", "arbitrary")),
        name="moe_up",
    )(tile_expert, n_tiles, xs, w_in, w_in)

    return pl.pallas_call(
        _moe_down_kernel,
        grid_spec=pltpu.PrefetchScalarGridSpec(
            num_scalar_prefetch=2,
            grid=(MOE_NB, tiles),
            in_specs=[pl.BlockSpec((MOE_TM, D_EXPERT), lambda c, j, te, nt: (clamp(j, nt), 0)),
                      pl.BlockSpec((None, None, D_EXPERT, cb), lambda c, j, te, nt: (layer, te[j], 0, c)),
                      pl.BlockSpec((MOE_TM, 1), lambda c, j, te, nt: (clamp(j, nt), 0))],
            out_specs=pl.BlockSpec((MOE_TM, cb), lambda c, j, te, nt: (j, c)),
        ),
        out_shape=jax.ShapeDtypeStruct((rows, D), BF16),
        compiler_params=_cparams(("arbitrary", "arbitrary")),
        name="moe_down",
    )(tile_expert, n_tiles, act, w_out, row_scale)


def _router_operands(w_rg, b_rg, w_re, b_re):
    D = w_rg.shape[0]
    pad = LANES - N_GROUPS - N_EXPERTS
    w = jnp.concatenate([w_rg, w_re, jnp.zeros((D, pad), F32)], axis=1)
    b = jnp.concatenate([b_rg, b_re, jnp.zeros((pad,), F32)])[None, :]
    return _split3_bf16(w), b


def _route(logits):
    T = logits.shape[0]
    lg = logits[:, :N_GROUPS]
    pg = jax.nn.softmax(lg, axis=-1)
    gsel = jnp.argmax(lg, axis=-1)
    pg_sel = jnp.take_along_axis(pg, gsel[:, None], axis=1)
    le = logits[:, N_GROUPS:N_GROUPS + N_EXPERTS].reshape(T, N_GROUPS, EXPERTS_PER_GROUP)
    le_sel = jnp.take_along_axis(le, gsel[:, None, None], axis=1)[:, 0]
    top_v, top_i = lax.top_k(le_sel, TOP_K_INNER)
    w_top = jax.nn.softmax(top_v, axis=-1) * pg_sel
    eidx = (gsel[:, None] * EXPERTS_PER_GROUP + top_i).astype(jnp.int32)
    return eidx, w_top


def _dispatch(eidx, w_top):
    T = eidx.shape[0]
    flat_e = eidx.reshape(-1)
    onehot = (flat_e[:, None] == jnp.arange(N_EXPERTS, dtype=jnp.int32)[None, :]).astype(jnp.int32)
    csum = jnp.cumsum(onehot, axis=0)
    counts = csum[-1]
    rank = jnp.sum(csum * onehot, axis=1) - 1
    padded = ((counts + MOE_TM - 1) // MOE_TM) * MOE_TM
    pend = jnp.cumsum(padded)
    pstart = pend - padded
    dest = (pstart[flat_e] + rank).astype(jnp.int32)
    tok = jnp.arange(2 * T, dtype=jnp.int32) // TOP_K_INNER
    rows = TOP_K_INNER * T + N_EXPERTS * MOE_TM
    tiles = rows // MOE_TM
    src_tok = jnp.zeros((rows,), jnp.int32).at[dest].set(tok)
    row_scale = jnp.zeros((rows,), F32).at[dest].set(w_top.reshape(-1))
    n_tiles = (pend[-1] // MOE_TM).astype(jnp.int32)
    tile_start = jnp.arange(tiles, dtype=jnp.int32) * MOE_TM
    tile_expert = jnp.searchsorted(pend, tile_start, side='right').astype(jnp.int32)
    last_e = jnp.searchsorted(pend, (n_tiles - 1) * MOE_TM, side='right').astype(jnp.int32)
    tile_expert = jnp.where(jnp.arange(tiles) < n_tiles, tile_expert, last_e)
    return src_tok, row_scale[:, None], tile_expert, n_tiles.reshape(1), dest.reshape(T, TOP_K_INNER)


def _hier_moe(u2, logits, lw, layer):
    eidx, w_top = _route(logits)
    src_tok, row_scale, tile_expert, n_tiles, dest = _dispatch(eidx, w_top)
    rows_of = lambda a, idx: a.at[idx].get(mode='promise_in_bounds')
    xs = rows_of(u2, src_tok)
    ys = _moe_experts(xs, tile_expert, n_tiles, row_scale, lw['w_exp_in'], lw['w_exp_out'], layer)
    return rows_of(ys, dest[:, 0]), rows_of(ys, dest[:, 1])


def _residual_pair_kernel(x_ref, g_ref, a_ref, b_ref, o_ref):
    o_ref[...] = x_ref[...] + g_ref[...] * (a_ref[...].astype(F32) + b_ref[...].astype(F32))


def _residual_pair(x, gate, ya, yb, tm=256):
    T, D = x.shape
    row = pl.BlockSpec((tm, D), lambda i: (i, 0))
    return pl.pallas_call(
        _residual_pair_kernel,
        grid=(T // tm,),
        in_specs=[row, pl.BlockSpec((None, 1, D), lambda i: (_seg_of_tile(i, tm), 0, 0)), row, row],
        out_specs=row,
        out_shape=jax.ShapeDtypeStruct((T, D), F32),
        compiler_params=_cparams(("parallel",)),
        name="moe_residual",
    )(x, gate, ya, yb)


def _axial_rope_tables(rows, dim):
    row = jnp.repeat(jnp.arange(rows, dtype=F32), GRID_W)
    col = jnp.tile(jnp.arange(GRID_W, dtype=F32), rows)
    n_freq = dim // 4
    inv = ROPE_BASE ** (-jnp.arange(n_freq, dtype=F32) / n_freq)
    ang = jnp.concatenate([row[:, None] * inv[None], col[:, None] * inv[None]], axis=-1)
    return jnp.cos(ang), jnp.sin(ang)


def _rope_lane_tables(rows, dim):
    cos, sin = _axial_rope_tables(rows, dim)
    groups = LANES // dim
    return (jnp.tile(cos, (1, 2 * groups)),
            jnp.tile(jnp.concatenate([-sin, sin], axis=-1), (1, groups)))


def _rope_lanes(x, cos_t, sin_t, dim):
    half = dim // 2
    if dim == LANES:
        swapped = pltpu.roll(x, half, 1)
    else:
        lane = lax.broadcasted_iota(jnp.int32, x.shape, 1)
        swapped = jnp.where((lane & half) == 0, pltpu.roll(x, LANES - half, 1), pltpu.roll(x, half, 1))
    return x * cos_t + swapped * sin_t


ATTN_SUB = 64

def _branch_placement(ys, branch, row_block0, rows, block, local_index):
    if ys is None:
        return (pl.BlockSpec(block, local_index), jax.ShapeDtypeStruct((rows, BRANCH_W), BF16), [], [], lambda n: {})

    def index(*g):
        r, c = local_index(*g)
        return (branch, row_block0 + r, c)

    return (pl.BlockSpec((None,) + block, index), jax.ShapeDtypeStruct(ys.shape, ys.dtype),
            [pl.BlockSpec(memory_space=pl.ANY)], [ys], lambda n: {n: 0})


def _drop_aliased(refs, n_in, aliased):
    refs = list(refs)
    return refs[:n_in] + refs[n_in + 1:] if aliased else refs


def _diff_attn_kernel(*refs, n_lat, n_cache, rope, tq, lam_init, aliased):
    refs = _drop_aliased(refs, 11 if rope else 5, aliased)
    if rope:
        (q_ref, k_ref, v_ref, lam_ref, g_ref, ck_ref, cv_ref, cosq_ref, sinq_ref, cosk_ref, sink_ref,
         o_ref, kbuf, vbuf) = refs
    else:
        q_ref, k_ref, v_ref, lam_ref, g_ref, o_ref, kbuf, vbuf = refs

    @pl.when(pl.program_id(2) == 0)
    def _():
        k = k_ref[...]
        if rope:
            k = _rope_lanes(k, cosk_ref[...], sink_ref[...], D_A)
        kbuf[0:n_lat, :] = k.astype(BF16)
        vbuf[0:n_lat, 0:2 * D_A] = v_ref[...].astype(BF16)
        if n_cache:
            kbuf[n_lat:n_lat + n_cache, :] = ck_ref[...].astype(BF16)
            vbuf[n_lat:n_lat + n_cache, 0:2 * D_A] = cv_ref[...].astype(BF16)
        vbuf[:, 2 * D_A:4 * D_A] = jnp.ones((n_lat + n_cache, 2 * D_A), BF16)

    q = q_ref[...]
    if rope:
        q = _rope_lanes(q, cosq_ref[...], sinq_ref[...], D_A)
    q = q * (D_A ** -0.5)
    lane = lax.broadcasted_iota(jnp.int32, (ATTN_SUB, 2 * D_A), 1)
    lf = lam_ref[...]
    lam = (jnp.exp(jnp.sum(lf[0:1] * lf[1:2], keepdims=True))
           - jnp.exp(jnp.sum(lf[2:3] * lf[3:4], keepdims=True)) + lam_init)
    for r in range(tq // ATTN_SUB):
        qr = q[r * ATTN_SUB:(r + 1) * ATTN_SUB]
        qs = jnp.concatenate([jnp.where(lane < D_A, qr, 0.0), jnp.where(lane >= D_A, qr, 0.0)], axis=0).astype(BF16)
        s = _dot_nt(qs, kbuf[...])
        e = jnp.exp(s - jnp.max(s, axis=-1, keepdims=True))
        pv_den = _dot(e.astype(BF16), vbuf[...])
        pv = pv_den[:, 0:2 * D_A] / pv_den[:, 2 * D_A:4 * D_A]
        o = pv[0:ATTN_SUB] - lam * pv[ATTN_SUB:2 * ATTN_SUB]
        y = o * lax.rsqrt(jnp.mean(o * o, axis=-1, keepdims=True) + EPS) * g_ref[...]
        o_ref[r * ATTN_SUB:(r + 1) * ATTN_SUB, :] = (y * (1.0 - lam_init)).astype(o_ref.dtype)


def _diff_attention(big, row0, B, N, layer, w_lam, w_subln, lam_init, cache=None, rope_tabs=None, tq=256,
                    ys=None, branch=0):
    rope = rope_tabs is not None
    n_cache = cache[0].shape[2] if cache is not None else 0
    nq = N // tq
    hw = 2 * D_A
    qc, kc, vc = OFF['a_q'] // hw, OFF['a_k'] // hw, OFF['a_v'] // hw
    in_specs = [pl.BlockSpec((tq, hw), lambda b, h, i: (row0 // tq + b * nq + i, qc + h)),
                pl.BlockSpec((N, hw), lambda b, h, i: (row0 // N + b, kc + h)),
                pl.BlockSpec((N, hw), lambda b, h, i: (row0 // N + b, vc + h)),
                pl.BlockSpec((None, 4, D_A), lambda b, h, i: (layer, 0, 0)),
                pl.BlockSpec((None, 1, hw), lambda b, h, i: (layer, 0, 0))]
    args = [big, big, big, w_lam, w_subln]
    if rope:
        cspec = pl.BlockSpec((None, None, n_cache, hw), lambda b, h, i: (b, layer, 0, h))
        in_specs += [cspec, cspec,
                     pl.BlockSpec((tq, hw), lambda b, h, i: (i, 0)), pl.BlockSpec((tq, hw), lambda b, h, i: (i, 0)),
                     pl.BlockSpec((N, hw), lambda b, h, i: (0, 0)), pl.BlockSpec((N, hw), lambda b, h, i: (0, 0))]
        args += [cache[0], cache[1], rope_tabs[0], rope_tabs[1], rope_tabs[0], rope_tabs[1]]
    out_spec, out_shape, x_specs, x_args, aliases = _branch_placement(
        ys, branch, row0 // tq, B * N, (tq, hw), lambda b, h, i: (b * nq + i, h))
    return pl.pallas_call(
        functools.partial(_diff_attn_kernel, n_lat=N, n_cache=n_cache, rope=rope, tq=tq, lam_init=lam_init,
                          aliased=ys is not None),
        grid=(B, H_A, nq),
        in_specs=in_specs + x_specs,
        out_specs=out_spec,
        out_shape=out_shape,
        input_output_aliases=aliases(len(args)),
        scratch_shapes=[pltpu.VMEM((N + n_cache, hw), BF16), pltpu.VMEM((N + n_cache, 2 * hw), BF16)],
        compiler_params=_cparams(("parallel", "parallel", "arbitrary")),
        name="diff_attention",
    )(*args, *x_args)


GQA_G = H_D // HKV_D


def _gqa_kernel(*refs, n_lat, n_cache, banded, tq, layer, aliased):
    refs = _drop_aliased(refs, 10 if banded else 4, aliased)
    if banded:
        (sink_ref, q_ref, k_ref, v_ref, ck_ref, cv_ref, cosq_ref, sinq_ref, cosk_ref, sink_tab_ref,
         o_ref, kbuf, vbuf) = refs
    else:
        sink_ref, q_ref, k_ref, v_ref, o_ref, kbuf, vbuf = refs
    W = WINDOW_D
    kv = pl.program_id(1)
    qi = pl.program_id(2)

    @pl.when(qi == 0)
    def _():
        k = k_ref[...]
        if banded:
            k = _rope_lanes(k, cosk_ref[...], sink_tab_ref[...], DH_D)
            zeros = jnp.zeros((W, DH_D), BF16)
            for buf, lat, cached in ((kbuf, k, ck_ref), (vbuf, v_ref[...], cv_ref)):
                buf[0:W, :] = zeros
                buf[W:W + n_lat, :] = lat.astype(BF16)
                buf[W + n_lat:2 * W + n_lat, :] = zeros
                buf[2 * W + n_lat:2 * W + n_lat + n_cache, :] = cached[...].astype(BF16)
        else:
            kbuf[...] = k.astype(BF16)
            vbuf[...] = v_ref[...].astype(BF16)

    q2 = q_ref[...]
    heads = [q2[:, g * DH_D:(g + 1) * DH_D] for g in range(GQA_G)]
    if banded:
        heads = [_rope_lanes(qh, cosq_ref[...], sinq_ref[...], DH_D) for qh in heads]
    qs = (jnp.concatenate(heads, axis=0) * (DH_D ** -0.5)).astype(BF16)
    row = lax.broadcasted_iota(jnp.int32, (GQA_G * tq, 1), 0)
    sink = jnp.full((GQA_G * tq, 1), sink_ref[layer, kv * GQA_G], F32)
    for g in range(1, GQA_G):
        sink = jnp.where(row >= g * tq, sink_ref[layer, kv * GQA_G + g], sink)

    if banded:
        start = pl.multiple_of(qi * W, W)
        s_loc = _dot_nt(qs, kbuf[pl.ds(start, 3 * W), :])
        r = lax.broadcasted_iota(jnp.int32, s_loc.shape, 0) & (tq - 1)
        c = lax.broadcasted_iota(jnp.int32, s_loc.shape, 1)
        kpos = qi * W + c - W
        valid = (jnp.abs(c - W - r) <= W) & (kpos >= 0) & (kpos < n_lat)
        s_loc = jnp.where(valid, s_loc, NEG_INF)
        s_ctx = _dot_nt(qs, kbuf[2 * W + n_lat:2 * W + n_lat + n_cache, :])
        m = jnp.maximum(jnp.maximum(jnp.max(s_loc, axis=-1, keepdims=True),
                                    jnp.max(s_ctx, axis=-1, keepdims=True)), sink)
        e_loc = jnp.exp(s_loc - m)
        e_ctx = jnp.exp(s_ctx - m)
        den = (jnp.sum(e_loc, axis=-1, keepdims=True) + jnp.sum(e_ctx, axis=-1, keepdims=True)
               + jnp.exp(sink - m))
        o = (_dot(e_loc.astype(BF16), vbuf[pl.ds(start, 3 * W), :])
             + _dot(e_ctx.astype(BF16), vbuf[2 * W + n_lat:2 * W + n_lat + n_cache, :])) / den
    else:
        s = _dot_nt(qs, kbuf[...])
        m = jnp.maximum(jnp.max(s, axis=-1, keepdims=True), sink)
        e = jnp.exp(s - m)
        den = jnp.sum(e, axis=-1, keepdims=True) + jnp.exp(sink - m)
        o = _dot(e.astype(BF16), vbuf[...]) / den
    o_ref[...] = jnp.concatenate([o[g * tq:(g + 1) * tq] for g in range(GQA_G)], axis=1).astype(o_ref.dtype)


def _gqa(big, row0, B, N, layer, w_sink, cache=None, rope_tabs=None, tq=WINDOW_D, ys=None, branch=0):
    banded = cache is not None
    n_cache = cache[0].shape[2] if banded else 0
    nq = N // tq
    qw = GQA_G * DH_D
    qc, kc, vc = OFF['d_q'] // qw, OFF['d_k'] // DH_D, OFF['d_v'] // DH_D
    in_specs = [pl.BlockSpec(memory_space=pltpu.SMEM),
                pl.BlockSpec((tq, qw), lambda b, h, i: (row0 // tq + b * nq + i, qc + h)),
                pl.BlockSpec((N, DH_D), lambda b, h, i: (row0 // N + b, kc + h)),
                pl.BlockSpec((N, DH_D), lambda b, h, i: (row0 // N + b, vc + h))]
    args = [w_sink, big, big, big]
    n_rows = N
    if banded:
        assert tq == WINDOW_D
        cspec = pl.BlockSpec((None, None, n_cache, DH_D), lambda b, h, i: (b, layer, 0, h))
        in_specs += [cspec, cspec,
                     pl.BlockSpec((tq, DH_D), lambda b, h, i: (i, 0)), pl.BlockSpec((tq, DH_D), lambda b, h, i: (i, 0)),
                     pl.BlockSpec((N, DH_D), lambda b, h, i: (0, 0)), pl.BlockSpec((N, DH_D), lambda b, h, i: (0, 0))]
        args += [cache[0], cache[1], rope_tabs[0], rope_tabs[1], rope_tabs[0], rope_tabs[1]]
        n_rows = N + 2 * WINDOW_D + n_cache
    out_spec, out_shape, x_specs, x_args, aliases = _branch_placement(
        ys, branch, row0 // tq, B * N, (tq, qw), lambda b, h, i: (b * nq + i, h))
    return pl.pallas_call(
        functools.partial(_gqa_kernel, n_lat=N, n_cache=n_cache, banded=banded, tq=tq, layer=layer,
                          aliased=ys is not None),
        grid=(B, HKV_D, nq),
        in_specs=in_specs + x_specs,
        out_specs=out_spec,
        out_shape=out_shape,
        input_output_aliases=aliases(len(args)),
        scratch_shapes=[pltpu.VMEM((n_rows, DH_D), BF16), pltpu.VMEM((n_rows, DH_D), BF16)],
        compiler_params=_cparams(("parallel", "parallel", "arbitrary")),
        name="gqa_sink_attention",
    )(*args, *x_args)


def _mlstm_chunk(q, k, v, i_col, lf_col, i_row, lf_row, C, n_row, m, rev):
    L = CHUNK_B
    t_i = lax.broadcasted_iota(jnp.int32, (L, L), 0)
    s_i = lax.broadcasted_iota(jnp.int32, (L, L), 1)
    tri = (s_i >= t_i) if rev else (s_i <= t_i)
    tri_t = (t_i >= s_i) if rev else (t_i <= s_i)
    b_col = jnp.sum(jnp.where(tri, lf_row, 0.0), axis=1, keepdims=True)
    b_row = jnp.sum(jnp.where(tri_t, lf_col, 0.0), axis=0, keepdims=True)
    a_col = b_col + m
    d = jnp.where(tri, b_col - b_row + i_row, NEG_INF)
    m_t = jnp.maximum(a_col, jnp.max(d, axis=1, keepdims=True))
    w_inter = jnp.exp(a_col - m_t)
    w_intra = jnp.exp(d - m_t)
    qb, kb, vb = q.astype(BF16), k.astype(BF16), v.astype(BF16)
    k_scale = DK_B ** -0.5
    qk = _dot_nt(qb, kb) * k_scale * w_intra
    num = w_inter * _dot(qb, C.astype(BF16)) + _dot(qk.astype(BF16), vb)
    den = w_inter * jnp.sum(q * n_row, axis=1, keepdims=True) + jnp.sum(qk, axis=1, keepdims=True)
    h = num / jnp.maximum(jnp.abs(den), jnp.exp(-m_t))
    last = 0 if rev else L - 1
    b_last = b_col[last:last + 1]
    m_last = m_t[last:last + 1]
    kw = k * (k_scale * jnp.exp(b_last - b_col + i_col - m_last))
    decay = w_inter[last:last + 1]
    C_new = decay * C + _dot_tn(kw.astype(BF16), vb)
    n_new = decay * n_row + jnp.sum(kw, axis=0, keepdims=True)
    return h, C_new, n_new, m_last


def _mlstm_kernel(*refs, zero_init, emit_state):
    refs = list(refs)
    qf, kf, vf, qb, kb, vb, gcf, gcb, grf, grb = refs[:10]
    pos = 10
    if not zero_init:
        c0_ref, n0_ref, m0_ref = refs[pos:pos + 3]
        pos += 3
    hf_ref, hb_ref = refs[pos:pos + 2]
    pos += 2
    if emit_state:
        co_ref, no_ref, mo_ref = refs[pos:pos + 3]
        pos += 3
    c_s, n_s, m_s = refs[pos:pos + 3]
    c = pl.program_id(1)

    @pl.when(c == 0)
    def _():
        if zero_init:
            c_s[...] = jnp.zeros_like(c_s)
            n_s[...] = jnp.zeros_like(n_s)
            m_s[...] = jnp.zeros_like(m_s)
        else:
            c_s[...] = c0_ref[...]
            n_s[...] = n0_ref[...]
            m_s[...] = m0_ref[...]

    streams = ((0, qf, kf, vf, gcf, grf, hf_ref), (1, qb, kb, vb, gcb, grb, hb_ref))
    for dr, q_ref, k_ref, v_ref, gc_ref, gr_ref, h_ref in streams:
        gc = gc_ref[...]
        gr = gr_ref[...]
        lf_c = _log_sigmoid(gc)
        lf_r = _log_sigmoid(gr)
        for hd in range(H_B):
            i_at = dr * 2 * H_B + hd
            f_at = i_at + H_B
            h, c_new, n_new, m_new = _mlstm_chunk(
                q_ref[:, hd * DK_B:(hd + 1) * DK_B], k_ref[:, hd * DK_B:(hd + 1) * DK_B],
                v_ref[:, hd * DV_B:(hd + 1) * DV_B],
                gc[:, i_at:i_at + 1], lf_c[:, f_at:f_at + 1], gr[i_at:i_at + 1, :], lf_r[f_at:f_at + 1, :],
                c_s[dr, hd], n_s[dr, hd], m_s[dr, hd], rev=bool(dr))
            h_ref[:, hd * DV_B:(hd + 1) * DV_B] = h
            c_s[dr, hd] = c_new
            n_s[dr, hd] = n_new
            m_s[dr, hd] = m_new

    if emit_state:
        @pl.when(c == pl.num_programs(1) - 1)
        def _():
            co_ref[...] = c_s[...]
            no_ref[...] = n_s[...]
            mo_ref[...] = m_s[...]


def _mlstm(big, gate_cols, gate_rows, row0, B, N, layer, state=None, emit_state=False):
    L = CHUNK_B
    nc = N // L
    r0 = row0 // L
    qw, vw = H_B * DK_B, H_B * DV_B
    qc, kc, vc = OFF['b_q'] // qw, OFF['b_k'] // qw, OFF['b_v'] // vw
    fwd = lambda b, c: r0 + b * nc + c
    bwd = lambda b, c: r0 + b * nc + (nc - 1 - c)
    in_specs, args = [], []
    for rowf in (fwd, bwd):
        for col, width in ((qc, qw), (kc, qw), (vc, vw)):
            in_specs.append(pl.BlockSpec((L, width), lambda b, c, rowf=rowf, col=col: (rowf(b, c), col)))
            args.append(big)
    for rowf in (fwd, bwd):
        in_specs.append(pl.BlockSpec((L, N_B_IF), lambda b, c, rowf=rowf: (rowf(b, c), 0)))
        args.append(gate_cols)
    for rowf in (fwd, bwd):
        in_specs.append(pl.BlockSpec((N_B_IF, L), lambda b, c, rowf=rowf: (0, rowf(b, c))))
        args.append(gate_rows)
    if state is not None:
        in_specs += [pl.BlockSpec((None, None, 2, H_B, DK_B, DV_B), lambda b, c: (b, layer, 0, 0, 0, 0)),
                     pl.BlockSpec((None, None, 2, H_B, 1, DK_B), lambda b, c: (b, layer, 0, 0, 0, 0)),
                     pl.BlockSpec((None, None, 2, H_B, 1, 1), lambda b, c: (b, layer, 0, 0, 0, 0))]
        args += list(state)
    out_specs = [pl.BlockSpec((L, vw), lambda b, c: (b * nc + c, 0)),
                 pl.BlockSpec((L, vw), lambda b, c: (b * nc + (nc - 1 - c), 0))]
    out_shape = [jax.ShapeDtypeStruct((B * N, vw), F32)] * 2
    if emit_state:
        out_specs += [pl.BlockSpec((None, 2, H_B, DK_B, DV_B), lambda b, c: (b, 0, 0, 0, 0)),
                      pl.BlockSpec((None, 2, H_B, 1, DK_B), lambda b, c: (b, 0, 0, 0, 0)),
                      pl.BlockSpec((None, 2, H_B, 1, 1), lambda b, c: (b, 0, 0, 0, 0))]
        out_shape += [jax.ShapeDtypeStruct((B, 2, H_B, DK_B, DV_B), F32),
                      jax.ShapeDtypeStruct((B, 2, H_B, 1, DK_B), F32),
                      jax.ShapeDtypeStruct((B, 2, H_B, 1, 1), F32)]
    return pl.pallas_call(
        functools.partial(_mlstm_kernel, zero_init=state is None, emit_state=emit_state),
        grid=(B, nc),
        in_specs=in_specs,
        out_specs=out_specs,
        out_shape=out_shape,
        scratch_shapes=[pltpu.VMEM((2, H_B, DK_B, DV_B), F32), pltpu.VMEM((2, H_B, 1, DK_B), F32),
                        pltpu.VMEM((2, H_B, 1, 1), F32)],
        compiler_params=_cparams(("parallel", "arbitrary")),
        name="mlstm_scan",
    )(*args)


def _prefix_sum_rows(tri_bf, x):
    x_hi = x.astype(BF16)
    r1 = x - x_hi.astype(F32)
    x_mid = r1.astype(BF16)
    x_lo = (r1 - x_mid.astype(F32)).astype(BF16)
    return _dot(tri_bf, x_hi) + _dot(tri_bf, x_mid) + _dot(tri_bf, x_lo)


def _gla_chunk(q, k, v, gc, s_t, rev):
    L, SB = CHUNK_C, SUB_C
    t_i = lax.broadcasted_iota(jnp.int32, (L, L), 0)
    s_i = lax.broadcasted_iota(jnp.int32, (L, L), 1)
    tri = (s_i >= t_i) if rev else (s_i <= t_i)
    bc = _prefix_sum_rows(jnp.where(tri, 1.0, 0.0).astype(BF16), gc)
    qs = q * (DK_C ** -0.5)
    inter = _dot_nt((qs * jnp.exp(bc)).astype(BF16), s_t.astype(BF16))
    col = lax.broadcasted_iota(jnp.int32, (SB, L), 1)
    blocks = []
    for ti in range(L // SB):
        lo, hi = ti * SB, (ti + 1) * SB
        ref_row = bc[hi - 1:hi] if rev else bc[lo:lo + 1]
        q_blk = qs[lo:hi]
        bc_blk = bc[lo:hi]
        q_dec = (q_blk * jnp.exp(bc_blk - ref_row)).astype(BF16)
        k_dec = (k * jnp.exp(jnp.minimum(ref_row - bc, 0.0))).astype(BF16)
        earlier = (col >= hi) if rev else (col < lo)
        a_blk = jnp.where(earlier, _dot_nt(q_dec, k_dec), 0.0)
        t_abs = lax.broadcasted_iota(jnp.int32, (SB, 1), 0) + lo
        for s in range(lo, hi):
            dec = jnp.exp(jnp.minimum(bc_blk - bc[s:s + 1], 0.0))
            a_col = jnp.sum(q_blk * k[s:s + 1] * dec, axis=-1, keepdims=True)
            ok = (t_abs <= s) if rev else (t_abs >= s)
            a_blk = jnp.where((col == s) & ok, a_col, a_blk)
        blocks.append(a_blk)
    a = jnp.concatenate(blocks, axis=0)
    vb = v.astype(BF16)
    o = inter + _dot(a.astype(BF16), vb)
    b_last = bc[0:1] if rev else bc[L - 1:L]
    k_dec = (k * jnp.exp(b_last - bc)).astype(BF16)
    s_new = jnp.exp(b_last) * s_t + _dot_tn(vb, k_dec)
    return o, s_new


def _gla_kernel(*refs, zero_init, emit_state):
    refs = list(refs)
    qf, kf, vf, sf, qb, kb, vb, sb, wup_ref, bup_ref = refs[:10]
    pos = 10
    if not zero_init:
        s0_ref = refs[pos]
        pos += 1
    of_ref, ob_ref = refs[pos:pos + 2]
    pos += 2
    if emit_state:
        so_ref = refs[pos]
        pos += 1
    st_s = refs[pos]
    c = pl.program_id(1)

    @pl.when(c == 0)
    def _():
        for dr in range(2):
            for hd in range(H_C):
                if zero_init:
                    st_s[dr, hd] = jnp.zeros((DV_C, DK_C), F32)
                else:
                    st_s[dr, hd] = s0_ref[dr, hd].T

    streams = ((0, qf, kf, vf, sf, of_ref), (1, qb, kb, vb, sb, ob_ref))
    for dr, q_ref, k_ref, v_ref, sm_ref, o_ref in streams:
        z = _dot(sm_ref[...].astype(BF16), wup_ref[dr]) + bup_ref[dr]
        gc = _log_sigmoid(z) * (1.0 / GATE_TAU_C)
        for hd in range(H_C):
            ks = slice(hd * DK_C, (hd + 1) * DK_C)
            vs = slice(hd * DV_C, (hd + 1) * DV_C)
            o, s_new = _gla_chunk(q_ref[:, ks], k_ref[:, ks], v_ref[:, vs], gc[:, ks], st_s[dr, hd], rev=bool(dr))
            o_ref[:, vs] = o
            st_s[dr, hd] = s_new

    if emit_state:
        @pl.when(c == pl.num_programs(1) - 1)
        def _():
            for dr in range(2):
                for hd in range(H_C):
                    so_ref[dr, hd] = st_s[dr, hd].T


def _gla(big, small, w_up_pad, b_up, row0, B, N, layer, state=None, emit_state=False):
    L = CHUNK_C
    nc = N // L
    r0 = row0 // L
    qw, vw = H_C * DK_C, H_C * DV_C
    qc, kc, vc = OFF['c_q'] // qw, OFF['c_k'] // qw, OFF['c_v'] // vw
    fwd = lambda b, c: r0 + b * nc + c
    bwd = lambda b, c: r0 + b * nc + (nc - 1 - c)
    in_specs, args = [], []
    for rowf in (fwd, bwd):
        for col, width in ((qc, qw), (kc, qw), (vc, vw)):
            in_specs.append(pl.BlockSpec((L, width), lambda b, c, rowf=rowf, col=col: (rowf(b, c), col)))
            args.append(big)
        in_specs.append(pl.BlockSpec((L, D_SMALL), lambda b, c, rowf=rowf: (rowf(b, c), 0)))
        args.append(small)
    in_specs += [pl.BlockSpec((None, 2, D_SMALL, qw), lambda b, c: (layer, 0, 0, 0)),
                 pl.BlockSpec((None, 2, 1, qw), lambda b, c: (layer, 0, 0, 0))]
    args += [w_up_pad, b_up]
    if state is not None:
        in_specs.append(pl.BlockSpec((None, None, 2, H_C, DK_C, DV_C), lambda b, c: (b, layer, 0, 0, 0, 0)))
        args.append(state)
    out_specs = [pl.BlockSpec((L, vw), lambda b, c: (b * nc + c, 0)),
                 pl.BlockSpec((L, vw), lambda b, c: (b * nc + (nc - 1 - c), 0))]
    out_shape = [jax.ShapeDtypeStruct((B * N, vw), F32)] * 2
    if emit_state:
        out_specs.append(pl.BlockSpec((None, 2, H_C, DK_C, DV_C), lambda b, c: (b, 0, 0, 0, 0)))
        out_shape.append(jax.ShapeDtypeStruct((B, 2, H_C, DK_C, DV_C), F32))
    return pl.pallas_call(
        functools.partial(_gla_kernel, zero_init=state is None, emit_state=emit_state),
        grid=(B, nc),
        in_specs=in_specs,
        out_specs=out_specs,
        out_shape=out_shape,
        scratch_shapes=[pltpu.VMEM((2, H_C, DV_C, DK_C), F32)],
        compiler_params=_cparams(("parallel", "arbitrary")),
        name="gla_scan",
    )(*args)


def _gated_norm_kernel(*refs, silu, aliased):
    hf_ref, hb_ref, pre_ref, g_ref, o_ref = _drop_aliased(refs, 4, aliased)
    hw = DV_B
    for hd in range(BRANCH_W // hw):
        cs = slice(hd * hw, (hd + 1) * hw)
        h = hf_ref[:, cs] + hb_ref[:, cs]
        y = h * lax.rsqrt(jnp.mean(h * h, axis=-1, keepdims=True) + EPS) * g_ref[:, cs]
        pre = pre_ref[:, cs]
        gate = jax.nn.sigmoid(pre)
        if silu:
            gate = pre * gate
        o_ref[:, cs] = (y * gate).astype(o_ref.dtype)


def _gated_norm(hf, hb, big, pre_off, row0, w_norm, layer, silu, tm=256, ys=None, branch=0):
    T = hf.shape[0]
    blk = pl.BlockSpec((tm, BRANCH_W), lambda i: (i, 0))
    args = [hf, hb, big, w_norm]
    out_spec, out_shape, x_specs, x_args, aliases = _branch_placement(
        ys, branch, row0 // tm, T, (tm, BRANCH_W), lambda i: (i, 0))
    return pl.pallas_call(
        functools.partial(_gated_norm_kernel, silu=silu, aliased=ys is not None),
        grid=(T // tm,),
        in_specs=[blk, blk,
                  pl.BlockSpec((tm, BRANCH_W), lambda i: (row0 // tm + i, pre_off // BRANCH_W)),
                  pl.BlockSpec((None, 1, BRANCH_W), lambda i: (layer, 0, 0))] + x_specs,
        out_specs=out_spec,
        out_shape=out_shape,
        input_output_aliases=aliases(len(args)),
        compiler_params=_cparams(("parallel",)),
        name="gated_head_norm",
    )(*args, *x_args)


MOD_ROWS = 16


def _modulation_kernel(c_ref, w_ref, b_ref, o_ref):
    c = c_ref[...]
    x = c * jax.nn.sigmoid(c)
    x_hi = x.astype(BF16)
    x_mid = (x - x_hi.astype(F32)).astype(BF16)
    w = w_ref[...]
    w_hi = w.astype(BF16)
    w_mid = (w - w_hi.astype(F32)).astype(BF16)
    o_ref[...] = _dot(x_hi, w_hi) + _dot(x_hi, w_mid) + _dot(x_mid, w_hi) + b_ref[...]


def _modulation_tables(cond, w_ada, b_ada, layer, tn=512):
    D = cond.shape[1]
    N = w_ada.shape[-1]
    m = pl.pallas_call(
        _modulation_kernel,
        grid=(N // tn,),
        in_specs=[pl.BlockSpec((MOD_ROWS, D), lambda j: (0, 0)),
                  pl.BlockSpec((None, D, tn), lambda j: (layer, 0, j)),
                  pl.BlockSpec((None, 1, tn), lambda j: (layer, 0, j))],
        out_specs=pl.BlockSpec((MOD_ROWS, tn), lambda j: (0, j)),
        out_shape=jax.ShapeDtypeStruct((MOD_ROWS, N), F32),
        compiler_params=_cparams(("parallel",)),
        name="adaln_modulation",
    )(cond, w_ada, b_ada)
    return m[:N_SEG].reshape(N_SEG, 6, 1, D)


def _mixers(big, small, l, lam_init, shared, caches, ys):
    (w_lam, w_subln, gate_bias, w_b_norm, w_up_pad, b_up, w_c_norm, w_sink, rope_a, rope_d) = shared
    (ca_k, ca_v, sb_c, sb_n, sb_m, sc_s, cd_k, cd_v) = caches
    gate_cols = small[:, :N_B_IF] + gate_bias[l]
    gate_rows = gate_cols.T

    ys = _diff_attention(big, 0, BATCH, SEQ, l, w_lam, w_subln, lam_init, ys=ys, branch=0)
    ys = _diff_attention(big, T_CTX, DEC_BATCH, DEC_SEQ, l, w_lam, w_subln, lam_init, cache=(ca_k, ca_v),
                         rope_tabs=rope_a, ys=ys, branch=0)

    hf_c, hb_c, c_out, n_out, m_out = _mlstm(big, gate_cols, gate_rows, 0, BATCH, SEQ, l, emit_state=True)
    hf_l, hb_l = _mlstm(big, gate_cols, gate_rows, T_CTX, DEC_BATCH, DEC_SEQ, l, state=(sb_c, sb_n, sb_m))
    ys = _gated_norm(hf_c, hb_c, big, OFF['b_o'], 0, w_b_norm, l, silu=False, ys=ys, branch=1)
    ys = _gated_norm(hf_l, hb_l, big, OFF['b_o'], T_CTX, w_b_norm, l, silu=False, ys=ys, branch=1)

    of_c, ob_c, s_out = _gla(big, small, w_up_pad, b_up, 0, BATCH, SEQ, l, emit_state=True)
    of_l, ob_l = _gla(big, small, w_up_pad, b_up, T_CTX, DEC_BATCH, DEC_SEQ, l, state=sc_s)
    ys = _gated_norm(of_c, ob_c, big, OFF['c_r'], 0, w_c_norm, l, silu=True, ys=ys, branch=2)
    ys = _gated_norm(of_l, ob_l, big, OFF['c_r'], T_CTX, w_c_norm, l, silu=True, ys=ys, branch=2)

    ys = _gqa(big, 0, BATCH, SEQ, l, w_sink, tq=SEQ, ys=ys, branch=3)
    ys = _gqa(big, T_CTX, DEC_BATCH, DEC_SEQ, l, w_sink, cache=(cd_k, cd_v), rope_tabs=rope_d, ys=ys, branch=3)

    ctx_rows = big[:T_CTX]
    ctx = (ctx_rows[:, OFF['a_k']:OFF['a_k'] + 1024].reshape(BATCH, SEQ, H_A, 2 * D_A),
           ctx_rows[:, OFF['a_v']:OFF['a_v'] + 1024].reshape(BATCH, SEQ, H_A, 2 * D_A),
           c_out, n_out.reshape(BATCH, 2, H_B, DK_B), m_out.reshape(BATCH, 2, H_B), s_out,
           ctx_rows[:, OFF['d_k']:OFF['d_k'] + 512].reshape(BATCH, SEQ, HKV_D, DH_D),
           ctx_rows[:, OFF['d_v']:OFF['d_v'] + 512].reshape(BATCH, SEQ, HKV_D, DH_D))
    return ys, ctx


def kernel(x_prompt, x_sample, c, cache_a_k, cache_a_v, state_b_C, state_b_n, state_b_m, state_c_S, cache_d_k, cache_d_v, c_ctx, w_ada, b_ada, w_norm1, w_norm2, w_in, w_a_lambda, w_a_subln, b_b_if, w_b_norm, w_c_alpha_up, b_c_alpha, w_c_norm, w_d_sink, w_branch, w_merge_gate, w_out, w_router_group, b_router_group, w_router_expert, b_router_expert, w_exp_in, w_exp_out, w_norm_f):
    rows = DEC_SEQ // GRID_W
    rope_a = _rope_lane_tables(rows, D_A)
    rope_d = _rope_lane_tables(rows, DH_D)

    w_big = jnp.concatenate([w_in[:, :, :OFF_B_IF], w_in[:, :, OFF_C_Q:OFF_C_GLR], w_in[:, :, OFF_D_Q:]],
                            axis=-1).astype(BF16)
    w_small = jnp.concatenate([w_in[:, :, OFF_B_IF:OFF_C_Q], w_in[:, :, OFF_C_GLR:OFF_D_Q],
                               jnp.zeros((DEPTH, D_MODEL, D_SMALL - N_B_IF - N_C_GLR), F32)], axis=-1).astype(BF16)
    w_gate_bf = w_merge_gate.astype(BF16)
    w_branch_bf = w_branch.astype(BF16)
    w_out_bf = w_out.astype(BF16)
    w_exp_in_bf = w_exp_in.astype(BF16)
    w_exp_out_bf = w_exp_out.astype(BF16)

    w_up_pad = jnp.zeros((DEPTH, 2, D_SMALL, H_C * DK_C), F32)
    for d in range(2):
        lo = N_B_IF + d * GATE_RANK_C
        w_up_pad = w_up_pad.at[:, d, lo:lo + GATE_RANK_C, :].set(w_c_alpha_up[:, d])
    w_up_pad = w_up_pad.astype(BF16)

    shared = (w_a_lambda, w_a_subln.reshape(DEPTH, 1, 2 * D_A), b_b_if.reshape(DEPTH, N_B_IF),
              w_b_norm.reshape(DEPTH, 1, H_B * DV_B), w_up_pad, b_c_alpha.reshape(DEPTH, 2, 1, H_C * DK_C),
              w_c_norm.reshape(DEPTH, 1, H_C * DV_C), w_d_sink, rope_a, rope_d)
    caches = (cache_a_k.reshape(DEC_BATCH, DEPTH, PAST_LEN, H_A * 2 * D_A),
              cache_a_v.reshape(DEC_BATCH, DEPTH, PAST_LEN, H_A * 2 * D_A),
              state_b_C, state_b_n.reshape(DEC_BATCH, DEPTH, 2, H_B, 1, DK_B),
              state_b_m.reshape(DEC_BATCH, DEPTH, 2, H_B, 1, 1), state_c_S,
              cache_d_k.reshape(DEC_BATCH, DEPTH, PAST_LEN, HKV_D * DH_D),
              cache_d_v.reshape(DEC_BATCH, DEPTH, PAST_LEN, HKV_D * DH_D))

    cond = jnp.concatenate([c_ctx[None, :], c, jnp.zeros((MOD_ROWS - N_SEG, D_MODEL), F32)], axis=0)
    b_ada3 = b_ada.reshape(DEPTH, 1, 6 * D_MODEL)
    x = jnp.concatenate([x_prompt.reshape(T_CTX, D_MODEL), x_sample.reshape(T_LAT, D_MODEL)], axis=0)
    ys = jnp.zeros((N_BRANCH, T_ALL, BRANCH_W), BF16)
    ctx_out = [[] for _ in range(8)]
    for l in range(DEPTH):
        lw = {'w_exp_in': w_exp_in_bf, 'w_exp_out': w_exp_out_bf}
        w_router3, b_router = _router_operands(w_router_group[l], b_router_group[l],
                                               w_router_expert[l], b_router_expert[l])
        lam_init = 0.8 - 0.6 * math.exp(-0.3 * l)
        mods = _modulation_tables(cond, w_ada, b_ada3, l)
        sh1, sc1, g1, sh2, sc2, g2 = [mods[:, i] for i in range(6)]

        u = _norm_mod(x, w_norm1[l][None, :], sh1, sc1)
        big = _mm(u, w_big, l, F32)
        small = _mm(u, w_small, l, F32, tn=D_SMALL)

        ys, ctx = _mixers(big, small, l, lam_init, shared, caches, ys)
        for lst, t in zip(ctx_out, ctx):
            lst.append(t)

        merged = _merge(u, ys, w_gate_bf, w_branch_bf, l)
        x = _mm_residual(merged, w_out_bf, l, x, g1)

        u2, logits = _norm_mod_route(x, w_norm2[l][None, :], sh2, sc2, w_router3, b_router)
        ya, yb = _hier_moe(u2, logits, lw, l)
        x = _residual_pair(x, g2, ya, yb)

    y = _norm(x, w_norm_f[None, :])
    y_prompt = y[:T_CTX].reshape(BATCH, SEQ, D_MODEL)
    y_sample = y[T_CTX:].reshape(DEC_BATCH, DEC_SEQ, D_MODEL)
    outs = [jnp.stack(t, axis=1) for t in ctx_out]
    return (y_prompt, y_sample) + tuple(outs)
```

```python
import functools
import math

import jax
import jax.numpy as jnp
from jax import lax
from jax.experimental import pallas as pl
from jax.experimental.pallas import tpu as pltpu

D_MODEL = 4096
BATCH = 32
SEQ = 256
DEPTH = 4
DEC_BATCH = 8
DEC_SEQ = 2048
PAST_LEN = 512
GRID_W = 64
ROPE_BASE = 10000.0
EPS = 1e-6
NEG_INF = -1e30
H_A = 8
D_A = 64
H_B = 4
DK_B = 256
DV_B = 256
CHUNK_B = 128
H_C = 4
DK_C = 128
DV_C = 256
GATE_RANK_C = 16
GATE_TAU_C = 16.0
CHUNK_C = 64
SUB_C = 16
H_D = 8
HKV_D = 4
DH_D = 128
WINDOW_D = 128
N_BRANCH = 4
BRANCH_W = 1024
N_GROUPS = 4
EXPERTS_PER_GROUP = 4
N_EXPERTS = N_GROUPS * EXPERTS_PER_GROUP
TOP_K_INNER = 2
D_EXPERT = 1024

F32 = jnp.float32
BF16 = jnp.bfloat16
LANES = 128

T_CTX = BATCH * SEQ
T_LAT = DEC_BATCH * DEC_SEQ
T_ALL = T_CTX + T_LAT
N_SEG = 1 + DEC_BATCH

BIG_PARTS = (
    ('a_q', 1024), ('a_k', 1024), ('a_v', 1024),
    ('b_q', 1024), ('b_k', 1024), ('b_v', 1024), ('b_o', 1024),
    ('c_q', 512), ('c_k', 512), ('c_v', 1024), ('c_r', 1024),
    ('d_q', 1024), ('d_k', 512), ('d_v', 512),
)
D_BIG = sum(w for _, w in BIG_PARTS)
OFF = {}
_o = 0
for _name, _w in BIG_PARTS:
    OFF[_name] = _o
    _o += _w
N_B_IF = 2 * 2 * H_B
N_C_GLR = 2 * GATE_RANK_C
D_SMALL = LANES
OFF_B_IF = 7 * 1024
OFF_C_Q = OFF_B_IF + N_B_IF
OFF_C_GLR = OFF_C_Q + 512 + 512 + 1024 + 1024
OFF_D_Q = OFF_C_GLR + N_C_GLR

VMEM_LIMIT = 56 * 1024 * 1024


def _cparams(sem):
    return pltpu.CompilerParams(dimension_semantics=sem, vmem_limit_bytes=VMEM_LIMIT)


def _seg_of_tile(i, tm):
    n_ctx = T_CTX // tm
    per_lat = DEC_SEQ // tm
    return jnp.where(i < n_ctx, 0, 1 + (i - n_ctx) // per_lat)


def _dot(a, b):
    return jnp.dot(a, b, preferred_element_type=F32)


def _dot_nt(a, b):
    return lax.dot_general(a, b, (((1,), (1,)), ((), ())), preferred_element_type=F32)


def _dot_tn(a, b):
    return lax.dot_general(a, b, (((0,), (0,)), ((), ())), preferred_element_type=F32)


def _log_sigmoid(x):
    return jnp.minimum(x, 0.0) - jnp.log1p(jnp.exp(-jnp.abs(x)))


def _norm_mod_kernel(x_ref, g_ref, sh_ref, sc_ref, o_ref):
    x = x_ref[...]
    y = x * lax.rsqrt(jnp.mean(x * x, axis=-1, keepdims=True) + EPS) * g_ref[...]
    o_ref[...] = (y * (1.0 + sc_ref[...]) + sh_ref[...]).astype(o_ref.dtype)


def _norm_mod(x, g, shift, scale, tm=256):
    T, D = x.shape
    seg = lambda i: (_seg_of_tile(i, tm), 0, 0)
    return pl.pallas_call(
        _norm_mod_kernel,
        grid=(T // tm,),
        in_specs=[pl.BlockSpec((tm, D), lambda i: (i, 0)),
                  pl.BlockSpec((1, D), lambda i: (0, 0)),
                  pl.BlockSpec((None, 1, D), seg),
                  pl.BlockSpec((None, 1, D), seg)],
        out_specs=pl.BlockSpec((tm, D), lambda i: (i, 0)),
        out_shape=jax.ShapeDtypeStruct((T, D), BF16),
        compiler_params=_cparams(("parallel",)),
        name="norm_mod",
    )(x, g, shift, scale)


def _norm_mod_route_kernel(x_ref, g_ref, sh_ref, sc_ref, wr_ref, br_ref, o_ref, lg_ref):
    x = x_ref[...]
    y = x * lax.rsqrt(jnp.mean(x * x, axis=-1, keepdims=True) + EPS) * g_ref[...]
    u = (y * (1.0 + sc_ref[...]) + sh_ref[...]).astype(o_ref.dtype)
    o_ref[...] = u
    lg_ref[...] = _dot(u, wr_ref[0]) + _dot(u, wr_ref[1]) + _dot(u, wr_ref[2]) + br_ref[...]


def _norm_mod_route(x, g, shift, scale, w_router3, b_router, tm=256):
    T, D = x.shape
    seg = lambda i: (_seg_of_tile(i, tm), 0, 0)
    return pl.pallas_call(
        _norm_mod_route_kernel,
        grid=(T // tm,),
        in_specs=[pl.BlockSpec((tm, D), lambda i: (i, 0)),
                  pl.BlockSpec((1, D), lambda i: (0, 0)),
                  pl.BlockSpec((None, 1, D), seg),
                  pl.BlockSpec((None, 1, D), seg),
                  pl.BlockSpec((3, D, LANES), lambda i: (0, 0, 0)),
                  pl.BlockSpec((1, LANES), lambda i: (0, 0))],
        out_specs=[pl.BlockSpec((tm, D), lambda i: (i, 0)), pl.BlockSpec((tm, LANES), lambda i: (i, 0))],
        out_shape=[jax.ShapeDtypeStruct((T, D), BF16), jax.ShapeDtypeStruct((T, LANES), F32)],
        compiler_params=_cparams(("parallel",)),
        name="norm_mod_route",
    )(x, g, shift, scale, w_router3, b_router)


def _split3_bf16(w):
    hi = w.astype(BF16)
    r1 = w - hi.astype(F32)
    mid = r1.astype(BF16)
    lo = (r1 - mid.astype(F32)).astype(BF16)
    return jnp.stack([hi, mid, lo], axis=0)


def _norm_kernel(x_ref, g_ref, o_ref):
    x = x_ref[...]
    o_ref[...] = x * lax.rsqrt(jnp.mean(x * x, axis=-1, keepdims=True) + EPS) * g_ref[...]


def _norm(x, g, tm=256):
    T, D = x.shape
    return pl.pallas_call(
        _norm_kernel,
        grid=(T // tm,),
        in_specs=[pl.BlockSpec((tm, D), lambda i: (i, 0)),
                  pl.BlockSpec((1, D), lambda i: (0, 0))],
        out_specs=pl.BlockSpec((tm, D), lambda i: (i, 0)),
        out_shape=jax.ShapeDtypeStruct((T, D), F32),
        compiler_params=_cparams(("parallel",)),
        name="final_norm",
    )(x, g)


def _mm_kernel(a_ref, w_ref, o_ref):
    o_ref[...] = _dot(a_ref[...], w_ref[...]).astype(o_ref.dtype)


def _mm(a, w, layer, out_dtype, tm=1024, tn=1024):
    T, K = a.shape
    N = w.shape[-1]
    return pl.pallas_call(
        _mm_kernel,
        grid=(T // tm, N // tn),
        in_specs=[pl.BlockSpec((tm, K), lambda i, j: (i, 0)),
                  pl.BlockSpec((None, K, tn), lambda i, j: (layer, 0, j))],
        out_specs=pl.BlockSpec((tm, tn), lambda i, j: (i, j)),
        out_shape=jax.ShapeDtypeStruct((T, N), out_dtype),
        compiler_params=_cparams(("parallel", "parallel")),
        name="proj_mm",
    )(a, w)


def _mm_res_kernel(a_ref, w_ref, x_ref, g_ref, o_ref):
    o_ref[...] = x_ref[...] + g_ref[...] * _dot(a_ref[...], w_ref[...])


def _mm_residual(a, w, layer, x, gate, tm=1024, tn=512):
    T, K = a.shape
    N = w.shape[-1]
    return pl.pallas_call(
        _mm_res_kernel,
        grid=(T // tm, N // tn),
        in_specs=[pl.BlockSpec((tm, K), lambda i, j: (i, 0)),
                  pl.BlockSpec((None, K, tn), lambda i, j: (layer, 0, j)),
                  pl.BlockSpec((tm, tn), lambda i, j: (i, j)),
                  pl.BlockSpec((None, 1, tn), lambda i, j: (_seg_of_tile(i, tm), 0, j))],
        out_specs=pl.BlockSpec((tm, tn), lambda i, j: (i, j)),
        out_shape=jax.ShapeDtypeStruct((T, N), F32),
        compiler_params=_cparams(("parallel", "parallel")),
        name="out_proj_residual",
    )(a, w, x, gate)


def _merge_kernel(u_ref, wg_ref, y_ref, wb_ref, o_ref, acc_ref):
    b = pl.program_id(2)
    gate = jax.nn.sigmoid(_dot(u_ref[...], wg_ref[...]))
    term = gate * _dot(y_ref[...], wb_ref[...])

    @pl.when(b == 0)
    def _():
        acc_ref[...] = term

    @pl.when(b > 0)
    def _():
        acc_ref[...] += term

    @pl.when(b == N_BRANCH - 1)
    def _():
        o_ref[...] = acc_ref[...].astype(o_ref.dtype)


def _merge(u, ys, w_gate, w_branch, layer, tm=1024, tn=512):
    T, D = u.shape
    nj = D // tn
    return pl.pallas_call(
        _merge_kernel,
        grid=(T // tm, nj, N_BRANCH),
        in_specs=[pl.BlockSpec((tm, D), lambda i, j, b: (i, 0)),
                  pl.BlockSpec((None, D, tn), lambda i, j, b: (layer, 0, b * nj + j)),
                  pl.BlockSpec((None, tm, BRANCH_W), lambda i, j, b: (b, i, 0)),
                  pl.BlockSpec((None, None, BRANCH_W, tn), lambda i, j, b: (layer, b, 0, j))],
        out_specs=pl.BlockSpec((tm, tn), lambda i, j, b: (i, j)),
        out_shape=jax.ShapeDtypeStruct((T, D), BF16),
        scratch_shapes=[pltpu.VMEM((tm, tn), F32)],
        compiler_params=_cparams(("parallel", "parallel", "arbitrary")),
        name="branch_merge",
    )(u, w_gate, ys, w_branch)


MOE_TM = 512
MOE_NA = 2
MOE_NB = 2


def _moe_up_kernel(te_ref, nt_ref, x_ref, w1_ref, w2_ref, act_ref):
    used = pl.program_id(1) < nt_ref[0]

    @pl.when(used)
    def _():
        x = x_ref[...]
        h1 = _dot(x, w1_ref[...])
        h2 = _dot(x, w2_ref[...])
        act_ref[...] = (h1 * jax.nn.sigmoid(h1) * h2).astype(act_ref.dtype)

    @pl.when(jnp.logical_not(used))
    def _():
        act_ref[...] = jnp.zeros_like(act_ref)


def _moe_down_kernel(te_ref, nt_ref, act_ref, wo_ref, rs_ref, o_ref):
    used = pl.program_id(1) < nt_ref[0]

    @pl.when(used)
    def _():
        o_ref[...] = (_dot(act_ref[...], wo_ref[...]) * rs_ref[...]).astype(o_ref.dtype)

    @pl.when(jnp.logical_not(used))
    def _():
        o_ref[...] = jnp.zeros_like(o_ref)


def _moe_experts(xs, tile_expert, n_tiles, row_scale, w_in, w_out, layer):
    rows, D = xs.shape
    tiles = rows // MOE_TM
    hc = D_EXPERT // MOE_NA
    cb = D // MOE_NB
    clamp = lambda j, nt: jnp.minimum(j, nt[0] - 1)

    act = pl.pallas_call(
        _moe_up_kernel,
        grid_spec=pltpu.PrefetchScalarGridSpec(
            num_scalar_prefetch=2,
            grid=(MOE_NA, tiles),
            in_specs=[pl.BlockSpec((MOE_TM, D), lambda c, j, te, nt: (clamp(j, nt), 0)),
                      pl.BlockSpec((None, None, D, hc), lambda c, j, te, nt: (layer, te[j], 0, c)),
                      pl.BlockSpec((None, None, D, hc), lambda c, j, te, nt: (layer, te[j], 0, c + MOE_NA))],
            out_specs=pl.BlockSpec((MOE_TM, hc), lambda c, j, te, nt: (j, c)),
        ),
        out_shape=jax.ShapeDtypeStruct((rows, D_EXPERT), BF16),
        compiler_params=_cparams(("arbitrary", "arbitrary")),
        name="moe_up",
    )(tile_expert, n_tiles, xs, w_in, w_in)

    return pl.pallas_call(
        _moe_down_kernel,
        grid_spec=pltpu.PrefetchScalarGridSpec(
            num_scalar_prefetch=2,
            grid=(MOE_NB, tiles),
            in_specs=[pl.BlockSpec((MOE_TM, D_EXPERT), lambda c, j, te, nt: (clamp(j, nt), 0)),
                      pl.BlockSpec((None, None, D_EXPERT, cb), lambda c, j, te, nt: (layer, te[j], 0, c)),
                      pl.BlockSpec((MOE_TM, 1), lambda c, j, te, nt: (clamp(j, nt), 0))],
            out_specs=pl.BlockSpec((MOE_TM, cb), lambda c, j, te, nt: (j, c)),
        ),
        out_shape=jax.ShapeDtypeStruct((rows, D), BF16),
        compiler_params=_cparams(("arbitrary", "arbitrary")),
        name="moe_down",
    )(tile_expert, n_tiles, act, w_out, row_scale)


def _router_operands(w_rg, b_rg, w_re, b_re):
    D = w_rg.shape[0]
    pad = LANES - N_GROUPS - N_EXPERTS
    w = jnp.concatenate([w_rg, w_re, jnp.zeros((D, pad), F32)], axis=1)
    b = jnp.concatenate([b_rg, b_re, jnp.zeros((pad,), F32)])[None, :]
    return _split3_bf16(w), b


def _route(logits):
    T = logits.shape[0]
    lg = logits[:, :N_GROUPS]
    pg = jax.nn.softmax(lg, axis=-1)
    gsel = jnp.argmax(lg, axis=-1)
    pg_sel = jnp.take_along_axis(pg, gsel[:, None], axis=1)
    le = logits[:, N_GROUPS:N_GROUPS + N_EXPERTS].reshape(T, N_GROUPS, EXPERTS_PER_GROUP)
    le_sel = jnp.take_along_axis(le, gsel[:, None, None], axis=1)[:, 0]
    top_v, top_i = lax.top_k(le_sel, TOP_K_INNER)
    w_top = jax.nn.softmax(top_v, axis=-1) * pg_sel
    eidx = (gsel[:, None] * EXPERTS_PER_GROUP + top_i).astype(jnp.int32)
    return eidx, w_top


def _dispatch(eidx, w_top):
    T = eidx.shape[0]
    flat_e = eidx.reshape(-1)
    onehot = (flat_e[:, None] == jnp.arange(N_EXPERTS, dtype=jnp.int32)[None, :]).astype(jnp.int32)
    csum = jnp.cumsum(onehot, axis=0)
    counts = csum[-1]
    rank = jnp.sum(csum * onehot, axis=1) - 1
    padded = ((counts + MOE_TM - 1) // MOE_TM) * MOE_TM
    pend = jnp.cumsum(padded)
    pstart = pend - padded
    dest = (pstart[flat_e] + rank).astype(jnp.int32)
    tok = jnp.arange(2 * T, dtype=jnp.int32) // TOP_K_INNER
    rows = TOP_K_INNER * T + N_EXPERTS * MOE_TM
    tiles = rows // MOE_TM
    src_tok = jnp.zeros((rows,), jnp.int32).at[dest].set(tok)
    row_scale = jnp.zeros((rows,), F32).at[dest].set(w_top.reshape(-1))
    n_tiles = (pend[-1] // MOE_TM).astype(jnp.int32)
    tile_start = jnp.arange(tiles, dtype=jnp.int32) * MOE_TM
    tile_expert = jnp.searchsorted(pend, tile_start, side='right').astype(jnp.int32)
    last_e = jnp.searchsorted(pend, (n_tiles - 1) * MOE_TM, side='right').astype(jnp.int32)
    tile_expert = jnp.where(jnp.arange(tiles) < n_tiles, tile_expert, last_e)
    return src_tok, row_scale[:, None], tile_expert, n_tiles.reshape(1), dest.reshape(T, TOP_K_INNER)


def _hier_moe(u2, logits, lw, layer):
    eidx, w_top = _route(logits)
    src_tok, row_scale, tile_expert, n_tiles, dest = _dispatch(eidx, w_top)
    rows_of = lambda a, idx: a.at[idx].get(mode='promise_in_bounds')
    xs = rows_of(u2, src_tok)
    ys = _moe_experts(xs, tile_expert, n_tiles, row_scale, lw['w_exp_in'], lw['w_exp_out'], layer)
    return rows_of(ys, dest[:, 0]), rows_of(ys, dest[:, 1])


def _residual_pair_kernel(x_ref, g_ref, a_ref, b_ref, o_ref):
    o_ref[...] = x_ref[...] + g_ref[...] * (a_ref[...].astype(F32) + b_ref[...].astype(F32))


def _residual_pair(x, gate, ya, yb, tm=256):
    T, D = x.shape
    row = pl.BlockSpec((tm, D), lambda i: (i, 0))
    return pl.pallas_call(
        _residual_pair_kernel,
        grid=(T // tm,),
        in_specs=[row, pl.BlockSpec((None, 1, D), lambda i: (_seg_of_tile(i, tm), 0, 0)), row, row],
        out_specs=row,
        out_shape=jax.ShapeDtypeStruct((T, D), F32),
        compiler_params=_cparams(("parallel",)),
        name="moe_residual",
    )(x, gate, ya, yb)


def _axial_rope_tables(rows, dim):
    row = jnp.repeat(jnp.arange(rows, dtype=F32), GRID_W)
    col = jnp.tile(jnp.arange(GRID_W, dtype=F32), rows)
    n_freq = dim // 4
    inv = ROPE_BASE ** (-jnp.arange(n_freq, dtype=F32) / n_freq)
    ang = jnp.concatenate([row[:, None] * inv[None], col[:, None] * inv[None]], axis=-1)
    return jnp.cos(ang), jnp.sin(ang)


def _rope_lane_tables(rows, dim):
    cos, sin = _axial_rope_tables(rows, dim)
    groups = LANES // dim
    return (jnp.tile(cos, (1, 2 * groups)),
            jnp.tile(jnp.concatenate([-sin, sin], axis=-1), (1, groups)))


def _rope_lanes(x, cos_t, sin_t, dim):
    half = dim // 2
    if dim == LANES:
        swapped = pltpu.roll(x, half, 1)
    else:
        lane = lax.broadcasted_iota(jnp.int32, x.shape, 1)
        swapped = jnp.where((lane & half) == 0, pltpu.roll(x, LANES - half, 1), pltpu.roll(x, half, 1))
    return x * cos_t + swapped * sin_t


ATTN_SUB = 64

def _branch_placement(ys, branch, row_block0, rows, block, local_index):
    if ys is None:
        return (pl.BlockSpec(block, local_index), jax.ShapeDtypeStruct((rows, BRANCH_W), BF16), [], [], lambda n: {})

    def index(*g):
        r, c = local_index(*g)
        return (branch, row_block0 + r, c)

    return (pl.BlockSpec((None,) + block, index), jax.ShapeDtypeStruct(ys.shape, ys.dtype),
            [pl.BlockSpec(memory_space=pl.ANY)], [ys], lambda n: {n: 0})


def _drop_aliased(refs, n_in, aliased):
    refs = list(refs)
    return refs[:n_in] + refs[n_in + 1:] if aliased else refs


def _diff_attn_kernel(*refs, n_lat, n_cache, rope, tq, lam_init, aliased):
    refs = _drop_aliased(refs, 11 if rope else 5, aliased)
    if rope:
        (q_ref, k_ref, v_ref, lam_ref, g_ref, ck_ref, cv_ref, cosq_ref, sinq_ref, cosk_ref, sink_ref,
         o_ref, kbuf, vbuf) = refs
    else:
        q_ref, k_ref, v_ref, lam_ref, g_ref, o_ref, kbuf, vbuf = refs

    @pl.when(pl.program_id(2) == 0)
    def _():
        k = k_ref[...]
        if rope:
            k = _rope_lanes(k, cosk_ref[...], sink_ref[...], D_A)
        kbuf[0:n_lat, :] = k.astype(BF16)
        vbuf[0:n_lat, 0:2 * D_A] = v_ref[...].astype(BF16)
        if n_cache:
            kbuf[n_lat:n_lat + n_cache, :] = ck_ref[...].astype(BF16)
            vbuf[n_lat:n_lat + n_cache, 0:2 * D_A] = cv_ref[...].astype(BF16)
        vbuf[:, 2 * D_A:4 * D_A] = jnp.ones((n_lat + n_cache, 2 * D_A), BF16)

    q = q_ref[...]
    if rope:
        q = _rope_lanes(q, cosq_ref[...], sinq_ref[...], D_A)
    q = q * (D_A ** -0.5)
    lane = lax.broadcasted_iota(jnp.int32, (ATTN_SUB, 2 * D_A), 1)
    lf = lam_ref[...]
    lam = (jnp.exp(jnp.sum(lf[0:1] * lf[1:2], keepdims=True))
           - jnp.exp(jnp.sum(lf[2:3] * lf[3:4], keepdims=True)) + lam_init)
    for r in range(tq // ATTN_SUB):
        qr = q[r * ATTN_SUB:(r + 1) * ATTN_SUB]
        qs = jnp.concatenate([jnp.where(lane < D_A, qr, 0.0), jnp.where(lane >= D_A, qr, 0.0)], axis=0).astype(BF16)
        s = _dot_nt(qs, kbuf[...])
        e = jnp.exp(s - jnp.max(s, axis=-1, keepdims=True))
        pv_den = _dot(e.astype(BF16), vbuf[...])
        pv = pv_den[:, 0:2 * D_A] / pv_den[:, 2 * D_A:4 * D_A]
        o = pv[0:ATTN_SUB] - lam * pv[ATTN_SUB:2 * ATTN_SUB]
        y = o * lax.rsqrt(jnp.mean(o * o, axis=-1, keepdims=True) + EPS) * g_ref[...]
        o_ref[r * ATTN_SUB:(r + 1) * ATTN_SUB, :] = (y * (1.0 - lam_init)).astype(o_ref.dtype)


def _diff_attention(big, row0, B, N, layer, w_lam, w_subln, lam_init, cache=None, rope_tabs=None, tq=256,
                    ys=None, branch=0):
    rope = rope_tabs is not None
    n_cache = cache[0].shape[2] if cache is not None else 0
    nq = N // tq
    hw = 2 * D_A
    qc, kc, vc = OFF['a_q'] // hw, OFF['a_k'] // hw, OFF['a_v'] // hw
    in_specs = [pl.BlockSpec((tq, hw), lambda b, h, i: (row0 // tq + b * nq + i, qc + h)),
                pl.BlockSpec((N, hw), lambda b, h, i: (row0 // N + b, kc + h)),
                pl.BlockSpec((N, hw), lambda b, h, i: (row0 // N + b, vc + h)),
                pl.BlockSpec((None, 4, D_A), lambda b, h, i: (layer, 0, 0)),
                pl.BlockSpec((None, 1, hw), lambda b, h, i: (layer, 0, 0))]
    args = [big, big, big, w_lam, w_subln]
    if rope:
        cspec = pl.BlockSpec((None, None, n_cache, hw), lambda b, h, i: (b, layer, 0, h))
        in_specs += [cspec, cspec,
                     pl.BlockSpec((tq, hw), lambda b, h, i: (i, 0)), pl.BlockSpec((tq, hw), lambda b, h, i: (i, 0)),
                     pl.BlockSpec((N, hw), lambda b, h, i: (0, 0)), pl.BlockSpec((N, hw), lambda b, h, i: (0, 0))]
        args += [cache[0], cache[1], rope_tabs[0], rope_tabs[1], rope_tabs[0], rope_tabs[1]]
    out_spec, out_shape, x_specs, x_args, aliases = _branch_placement(
        ys, branch, row0 // tq, B * N, (tq, hw), lambda b, h, i: (b * nq + i, h))
    return pl.pallas_call(
        functools.partial(_diff_attn_kernel, n_lat=N, n_cache=n_cache, rope=rope, tq=tq, lam_init=lam_init,
                          aliased=ys is not None),
        grid=(B, H_A, nq),
        in_specs=in_specs + x_specs,
        out_specs=out_spec,
        out_shape=out_shape,
        input_output_aliases=aliases(len(args)),
        scratch_shapes=[pltpu.VMEM((N + n_cache, hw), BF16), pltpu.VMEM((N + n_cache, 2 * hw), BF16)],
        compiler_params=_cparams(("parallel", "parallel", "arbitrary")),
        name="diff_attention",
    )(*args, *x_args)


GQA_G = H_D // HKV_D


def _gqa_kernel(*refs, n_lat, n_cache, banded, tq, layer, aliased):
    refs = _drop_aliased(refs, 10 if banded else 4, aliased)
    if banded:
        (sink_ref, q_ref, k_ref, v_ref, ck_ref, cv_ref, cosq_ref, sinq_ref, cosk_ref, sink_tab_ref,
         o_ref, kbuf, vbuf) = refs
    else:
        sink_ref, q_ref, k_ref, v_ref, o_ref, kbuf, vbuf = refs
    W = WINDOW_D
    kv = pl.program_id(1)
    qi = pl.program_id(2)

    @pl.when(qi == 0)
    def _():
        k = k_ref[...]
        if banded:
            k = _rope_lanes(k, cosk_ref[...], sink_tab_ref[...], DH_D)
            zeros = jnp.zeros((W, DH_D), BF16)
            for buf, lat, cached in ((kbuf, k, ck_ref), (vbuf, v_ref[...], cv_ref)):
                buf[0:W, :] = zeros
                buf[W:W + n_lat, :] = lat.astype(BF16)
                buf[W + n_lat:2 * W + n_lat, :] = zeros
                buf[2 * W + n_lat:2 * W + n_lat + n_cache, :] = cached[...].astype(BF16)
        else:
            kbuf[...] = k.astype(BF16)
            vbuf[...] = v_ref[...].astype(BF16)

    q2 = q_ref[...]
    heads = [q2[:, g * DH_D:(g + 1) * DH_D] for g in range(GQA_G)]
    if banded:
        heads = [_rope_lanes(qh, cosq_ref[...], sinq_ref[...], DH_D) for qh in heads]
    qs = (jnp.concatenate(heads, axis=0) * (DH_D ** -0.5)).astype(BF16)
    row = lax.broadcasted_iota(jnp.int32, (GQA_G * tq, 1), 0)
    sink = jnp.full((GQA_G * tq, 1), sink_ref[layer, kv * GQA_G], F32)
    for g in range(1, GQA_G):
        sink = jnp.where(row >= g * tq, sink_ref[layer, kv * GQA_G + g], sink)

    if banded:
        start = pl.multiple_of(qi * W, W)
        s_loc = _dot_nt(qs, kbuf[pl.ds(start, 3 * W), :])
        r = lax.broadcasted_iota(jnp.int32, s_loc.shape, 0) & (tq - 1)
        c = lax.broadcasted_iota(jnp.int32, s_loc.shape, 1)
        kpos = qi * W + c - W
        valid = (jnp.abs(c - W - r) <= W) & (kpos >= 0) & (kpos < n_lat)
        s_loc = jnp.where(valid, s_loc, NEG_INF)
        s_ctx = _dot_nt(qs, kbuf[2 * W + n_lat:2 * W + n_lat + n_cache, :])
        m = jnp.maximum(jnp.maximum(jnp.max(s_loc, axis=-1, keepdims=True),
                                    jnp.max(s_ctx, axis=-1, keepdims=True)), sink)
        e_loc = jnp.exp(s_loc - m)
        e_ctx = jnp.exp(s_ctx - m)
        den = (jnp.sum(e_loc, axis=-1, keepdims=True) + jnp.sum(e_ctx, axis=-1, keepdims=True)
               + jnp.exp(sink - m))
        o = (_dot(e_loc.astype(BF16), vbuf[pl.ds(start, 3 * W), :])
             + _dot(e_ctx.astype(BF16), vbuf[2 * W + n_lat:2 * W + n_lat + n_cache, :])) / den
    else:
        s = _dot_nt(qs, kbuf[...])
        m = jnp.maximum(jnp.max(s, axis=-1, keepdims=True), sink)
        e = jnp.exp(s - m)
        den = jnp.sum(e, axis=-1, keepdims=True) + jnp.exp(sink - m)
        o = _dot(e.astype(BF16), vbuf[...]) / den
    o_ref[...] = jnp.concatenate([o[g * tq:(g + 1) * tq] for g in range(GQA_G)], axis=1).astype(o_ref.dtype)


def _gqa(big, row0, B, N, layer, w_sink, cache=None, rope_tabs=None, tq=WINDOW_D, ys=None, branch=0):
    banded = cache is not None
    n_cache = cache[0].shape[2] if banded else 0
    nq = N // tq
    qw = GQA_G * DH_D
    qc, kc, vc = OFF['d_q'] // qw, OFF['d_k'] // DH_D, OFF['d_v'] // DH_D
    in_specs = [pl.BlockSpec(memory_space=pltpu.SMEM),
                pl.BlockSpec((tq, qw), lambda b, h, i: (row0 // tq + b * nq + i, qc + h)),
                pl.BlockSpec((N, DH_D), lambda b, h, i: (row0 // N + b, kc + h)),
                pl.BlockSpec((N, DH_D), lambda b, h, i: (row0 // N + b, vc + h))]
    args = [w_sink, big, big, big]
    n_rows = N
    if banded:
        assert tq == WINDOW_D
        cspec = pl.BlockSpec((None, None, n_cache, DH_D), lambda b, h, i: (b, layer, 0, h))
        in_specs += [cspec, cspec,
                     pl.BlockSpec((tq, DH_D), lambda b, h, i: (i, 0)), pl.BlockSpec((tq, DH_D), lambda b, h, i: (i, 0)),
                     pl.BlockSpec((N, DH_D), lambda b, h, i: (0, 0)), pl.BlockSpec((N, DH_D), lambda b, h, i: (0, 0))]
        args += [cache[0], cache[1], rope_tabs[0], rope_tabs[1], rope_tabs[0], rope_tabs[1]]
        n_rows = N + 2 * WINDOW_D + n_cache
    out_spec, out_shape, x_specs, x_args, aliases = _branch_placement(
        ys, branch, row0 // tq, B * N, (tq, qw), lambda b, h, i: (b * nq + i, h))
    return pl.pallas_call(
        functools.partial(_gqa_kernel, n_lat=N, n_cache=n_cache, banded=banded, tq=tq, layer=layer,
                          aliased=ys is not None),
        grid=(B, HKV_D, nq),
        in_specs=in_specs + x_specs,
        out_specs=out_spec,
        out_shape=out_shape,
        input_output_aliases=aliases(len(args)),
        scratch_shapes=[pltpu.VMEM((n_rows, DH_D), BF16), pltpu.VMEM((n_rows, DH_D), BF16)],
        compiler_params=_cparams(("parallel", "parallel", "arbitrary")),
        name="gqa_sink_attention",
    )(*args, *x_args)


def _mlstm_chunk(q, k, v, i_col, lf_col, i_row, lf_row, C, n_row, m, rev):
    L = CHUNK_B
    t_i = lax.broadcasted_iota(jnp.int32, (L, L), 0)
    s_i = lax.broadcasted_iota(jnp.int32, (L, L), 1)
    tri = (s_i >= t_i) if rev else (s_i <= t_i)
    tri_t = (t_i >= s_i) if rev else (t_i <= s_i)
    b_col = jnp.sum(jnp.where(tri, lf_row, 0.0), axis=1, keepdims=True)
    b_row = jnp.sum(jnp.where(tri_t, lf_col, 0.0), axis=0, keepdims=True)
    a_col = b_col + m
    d = jnp.where(tri, b_col - b_row + i_row, NEG_INF)
    m_t = jnp.maximum(a_col, jnp.max(d, axis=1, keepdims=True))
    w_inter = jnp.exp(a_col - m_t)
    w_intra = jnp.exp(d - m_t)
    qb, kb, vb = q.astype(BF16), k.astype(BF16), v.astype(BF16)
    k_scale = DK_B ** -0.5
    qk = _dot_nt(qb, kb) * k_scale * w_intra
    num = w_inter * _dot(qb, C.astype(BF16)) + _dot(qk.astype(BF16), vb)
    den = w_inter * jnp.sum(q * n_row, axis=1, keepdims=True) + jnp.sum(qk, axis=1, keepdims=True)
    h = num / jnp.maximum(jnp.abs(den), jnp.exp(-m_t))
    last = 0 if rev else L - 1
    b_last = b_col[last:last + 1]
    m_last = m_t[last:last + 1]
    kw = k * (k_scale * jnp.exp(b_last - b_col + i_col - m_last))
    decay = w_inter[last:last + 1]
    C_new = decay * C + _dot_tn(kw.astype(BF16), vb)
    n_new = decay * n_row + jnp.sum(kw, axis=0, keepdims=True)
    return h, C_new, n_new, m_last


def _mlstm_kernel(*refs, zero_init, emit_state):
    refs = list(refs)
    qf, kf, vf, qb, kb, vb, gcf, gcb, grf, grb = refs[:10]
    pos = 10
    if not zero_init:
        c0_ref, n0_ref, m0_ref = refs[pos:pos + 3]
        pos += 3
    hf_ref, hb_ref = refs[pos:pos + 2]
    pos += 2
    if emit_state:
        co_ref, no_ref, mo_ref = refs[pos:pos + 3]
        pos += 3
    c_s, n_s, m_s = refs[pos:pos + 3]
    c = pl.program_id(1)

    @pl.when(c == 0)
    def _():
        if zero_init:
            c_s[...] = jnp.zeros_like(c_s)
            n_s[...] = jnp.zeros_like(n_s)
            m_s[...] = jnp.zeros_like(m_s)
        else:
            c_s[...] = c0_ref[...]
            n_s[...] = n0_ref[...]
            m_s[...] = m0_ref[...]

    streams = ((0, qf, kf, vf, gcf, grf, hf_ref), (1, qb, kb, vb, gcb, grb, hb_ref))
    for dr, q_ref, k_ref, v_ref, gc_ref, gr_ref, h_ref in streams:
        gc = gc_ref[...]
        gr = gr_ref[...]
        lf_c = _log_sigmoid(gc)
        lf_r = _log_sigmoid(gr)
        for hd in range(H_B):
            i_at = dr * 2 * H_B + hd
            f_at = i_at + H_B
            h, c_new, n_new, m_new = _mlstm_chunk(
                q_ref[:, hd * DK_B:(hd + 1) * DK_B], k_ref[:, hd * DK_B:(hd + 1) * DK_B],
                v_ref[:, hd * DV_B:(hd + 1) * DV_B],
                gc[:, i_at:i_at + 1], lf_c[:, f_at:f_at + 1], gr[i_at:i_at + 1, :], lf_r[f_at:f_at + 1, :],
                c_s[dr, hd], n_s[dr, hd], m_s[dr, hd], rev=bool(dr))
            h_ref[:, hd * DV_B:(hd + 1) * DV_B] = h
            c_s[dr, hd] = c_new
            n_s[dr, hd] = n_new
            m_s[dr, hd] = m_new

    if emit_state:
        @pl.when(c == pl.num_programs(1) - 1)
        def _():
            co_ref[...] = c_s[...]
            no_ref[...] = n_s[...]
            mo_ref[...] = m_s[...]


def _mlstm(big, gate_cols, gate_rows, row0, B, N, layer, state=None, emit_state=False):
    L = CHUNK_B
    nc = N // L
    r0 = row0 // L
    qw, vw = H_B * DK_B, H_B * DV_B
    qc, kc, vc = OFF['b_q'] // qw, OFF['b_k'] // qw, OFF['b_v'] // vw
    fwd = lambda b, c: r0 + b * nc + c
    bwd = lambda b, c: r0 + b * nc + (nc - 1 - c)
    in_specs, args = [], []
    for rowf in (fwd, bwd):
        for col, width in ((qc, qw), (kc, qw), (vc, vw)):
            in_specs.append(pl.BlockSpec((L, width), lambda b, c, rowf=rowf, col=col: (rowf(b, c), col)))
            args.append(big)
    for rowf in (fwd, bwd):
        in_specs.append(pl.BlockSpec((L, N_B_IF), lambda b, c, rowf=rowf: (rowf(b, c), 0)))
        args.append(gate_cols)
    for rowf in (fwd, bwd):
        in_specs.append(pl.BlockSpec((N_B_IF, L), lambda b, c, rowf=rowf: (0, rowf(b, c))))
        args.append(gate_rows)
    if state is not None:
        in_specs += [pl.BlockSpec((None, None, 2, H_B, DK_B, DV_B), lambda b, c: (b, layer, 0, 0, 0, 0)),
                     pl.BlockSpec((None, None, 2, H_B, 1, DK_B), lambda b, c: (b, layer, 0, 0, 0, 0)),
                     pl.BlockSpec((None, None, 2, H_B, 1, 1), lambda b, c: (b, layer, 0, 0, 0, 0))]
        args += list(state)
    out_specs = [pl.BlockSpec((L, vw), lambda b, c: (b * nc + c, 0)),
                 pl.BlockSpec((L, vw), lambda b, c: (b * nc + (nc - 1 - c), 0))]
    out_shape = [jax.ShapeDtypeStruct((B * N, vw), F32)] * 2
    if emit_state:
        out_specs += [pl.BlockSpec((None, 2, H_B, DK_B, DV_B), lambda b, c: (b, 0, 0, 0, 0)),
                      pl.BlockSpec((None, 2, H_B, 1, DK_B), lambda b, c: (b, 0, 0, 0, 0)),
                      pl.BlockSpec((None, 2, H_B, 1, 1), lambda b, c: (b, 0, 0, 0, 0))]
        out_shape += [jax.ShapeDtypeStruct((B, 2, H_B, DK_B, DV_B), F32),
                      jax.ShapeDtypeStruct((B, 2, H_B, 1, DK_B), F32),
                      jax.ShapeDtypeStruct((B, 2, H_B, 1, 1), F32)]
    return pl.pallas_call(
        functools.partial(_mlstm_kernel, zero_init=state is None, emit_state=emit_state),
        grid=(B, nc),
        in_specs=in_specs,
        out_specs=out_specs,
        out_shape=out_shape,
        scratch_shapes=[pltpu.VMEM((2, H_B, DK_B, DV_B), F32), pltpu.VMEM((2, H_B, 1, DK_B), F32),
                        pltpu.VMEM((2, H_B, 1, 1), F32)],
        compiler_params=_cparams(("parallel", "arbitrary")),
        name="mlstm_scan",
    )(*args)


def _prefix_sum_rows(tri_bf, x):
    x_hi = x.astype(BF16)
    r1 = x - x_hi.astype(F32)
    x_mid = r1.astype(BF16)
    x_lo = (r1 - x_mid.astype(F32)).astype(BF16)
    return _dot(tri_bf, x_hi) + _dot(tri_bf, x_mid) + _dot(tri_bf, x_lo)


def _gla_chunk(q, k, v, gc, s_t, rev):
    L, SB = CHUNK_C, SUB_C
    t_i = lax.broadcasted_iota(jnp.int32, (L, L), 0)
    s_i = lax.broadcasted_iota(jnp.int32, (L, L), 1)
    tri = (s_i >= t_i) if rev else (s_i <= t_i)
    bc = _prefix_sum_rows(jnp.where(tri, 1.0, 0.0).astype(BF16), gc)
    qs = q * (DK_C ** -0.5)
    inter = _dot_nt((qs * jnp.exp(bc)).astype(BF16), s_t.astype(BF16))
    col = lax.broadcasted_iota(jnp.int32, (SB, L), 1)
    blocks = []
    for ti in range(L // SB):
        lo, hi = ti * SB, (ti + 1) * SB
        ref_row = bc[hi - 1:hi] if rev else bc[lo:lo + 1]
        q_blk = qs[lo:hi]
        bc_blk = bc[lo:hi]
        q_dec = (q_blk * jnp.exp(bc_blk - ref_row)).astype(BF16)
        k_dec = (k * jnp.exp(jnp.minimum(ref_row - bc, 0.0))).astype(BF16)
        earlier = (col >= hi) if rev else (col < lo)
        a_blk = jnp.where(earlier, _dot_nt(q_dec, k_dec), 0.0)
        t_abs = lax.broadcasted_iota(jnp.int32, (SB, 1), 0) + lo
        for s in range(lo, hi):
            dec = jnp.exp(jnp.minimum(bc_blk - bc[s:s + 1], 0.0))
            a_col = jnp.sum(q_blk * k[s:s + 1] * dec, axis=-1, keepdims=True)
            ok = (t_abs <= s) if rev else (t_abs >= s)
            a_blk = jnp.where((col == s) & ok, a_col, a_blk)
        blocks.append(a_blk)
    a = jnp.concatenate(blocks, axis=0)
    vb = v.astype(BF16)
    o = inter + _dot(a.astype(BF16), vb)
    b_last = bc[0:1] if rev else bc[L - 1:L]
    k_dec = (k * jnp.exp(b_last - bc)).astype(BF16)
    s_new = jnp.exp(b_last) * s_t + _dot_tn(vb, k_dec)
    return o, s_new


def _gla_kernel(*refs, zero_init, emit_state):
    refs = list(refs)
    qf, kf, vf, sf, qb, kb, vb, sb, wup_ref, bup_ref = refs[:10]
    pos = 10
    if not zero_init:
        s0_ref = refs[pos]
        pos += 1
    of_ref, ob_ref = refs[pos:pos + 2]
    pos += 2
    if emit_state:
        so_ref = refs[pos]
        pos += 1
    st_s = refs[pos]
    c = pl.program_id(1)

    @pl.when(c == 0)
    def _():
        for dr in range(2):
            for hd in range(H_C):
                if zero_init:
                    st_s[dr, hd] = jnp.zeros((DV_C, DK_C), F32)
                else:
                    st_s[dr, hd] = s0_ref[dr, hd].T

    streams = ((0, qf, kf, vf, sf, of_ref), (1, qb, kb, vb, sb, ob_ref))
    for dr, q_ref, k_ref, v_ref, sm_ref, o_ref in streams:
        z = _dot(sm_ref[...].astype(BF16), wup_ref[dr]) + bup_ref[dr]
        gc = _log_sigmoid(z) * (1.0 / GATE_TAU_C)
        for hd in range(H_C):
            ks = slice(hd * DK_C, (hd + 1) * DK_C)
            vs = slice(hd * DV_C, (hd + 1) * DV_C)
            o, s_new = _gla_chunk(q_ref[:, ks], k_ref[:, ks], v_ref[:, vs], gc[:, ks], st_s[dr, hd], rev=bool(dr))
            o_ref[:, vs] = o
            st_s[dr, hd] = s_new

    if emit_state:
        @pl.when(c == pl.num_programs(1) - 1)
        def _():
            for dr in range(2):
                for hd in range(H_C):
                    so_ref[dr, hd] = st_s[dr, hd].T


def _gla(big, small, w_up_pad, b_up, row0, B, N, layer, state=None, emit_state=False):
    L = CHUNK_C
    nc = N // L
    r0 = row0 // L
    qw, vw = H_C * DK_C, H_C * DV_C
    qc, kc, vc = OFF['c_q'] // qw, OFF['c_k'] // qw, OFF['c_v'] // vw
    fwd = lambda b, c: r0 + b * nc + c
    bwd = lambda b, c: r0 + b * nc + (nc - 1 - c)
    in_specs, args = [], []
    for rowf in (fwd, bwd):
        for col, width in ((qc, qw), (kc, qw), (vc, vw)):
            in_specs.append(pl.BlockSpec((L, width), lambda b, c, rowf=rowf, col=col: (rowf(b, c), col)))
            args.append(big)
        in_specs.append(pl.BlockSpec((L, D_SMALL), lambda b, c, rowf=rowf: (rowf(b, c), 0)))
        args.append(small)
    in_specs += [pl.BlockSpec((None, 2, D_SMALL, qw), lambda b, c: (layer, 0, 0, 0)),
                 pl.BlockSpec((None, 2, 1, qw), lambda b, c: (layer, 0, 0, 0))]
    args += [w_up_pad, b_up]
    if state is not None:
        in_specs.append(pl.BlockSpec((None, None, 2, H_C, DK_C, DV_C), lambda b, c: (b, layer, 0, 0, 0, 0)))
        args.append(state)
    out_specs = [pl.BlockSpec((L, vw), lambda b, c: (b * nc + c, 0)),
                 pl.BlockSpec((L, vw), lambda b, c: (b * nc + (nc - 1 - c), 0))]
    out_shape = [jax.ShapeDtypeStruct((B * N, vw), F32)] * 2
    if emit_state:
        out_specs.append(pl.BlockSpec((None, 2, H_C, DK_C, DV_C), lambda b, c: (b, 0, 0, 0, 0)))
        out_shape.append(jax.ShapeDtypeStruct((B, 2, H_C, DK_C, DV_C), F32))
    return pl.pallas_call(
        functools.partial(_gla_kernel, zero_init=state is None, emit_state=emit_state),
        grid=(B, nc),
        in_specs=in_specs,
        out_specs=out_specs,
        out_shape=out_shape,
        scratch_shapes=[pltpu.VMEM((2, H_C, DV_C, DK_C), F32)],
        compiler_params=_cparams(("parallel", "arbitrary")),
        name="gla_scan",
    )(*args)


def _gated_norm_kernel(*refs, silu, aliased):
    hf_ref, hb_ref, pre_ref, g_ref, o_ref = _drop_aliased(refs, 4, aliased)
    hw = DV_B
    for hd in range(BRANCH_W // hw):
        cs = slice(hd * hw, (hd + 1) * hw)
        h = hf_ref[:, cs] + hb_ref[:, cs]
        y = h * lax.rsqrt(jnp.mean(h * h, axis=-1, keepdims=True) + EPS) * g_ref[:, cs]
        pre = pre_ref[:, cs]
        gate = jax.nn.sigmoid(pre)
        if silu:
            gate = pre * gate
        o_ref[:, cs] = (y * gate).astype(o_ref.dtype)


def _gated_norm(hf, hb, big, pre_off, row0, w_norm, layer, silu, tm=256, ys=None, branch=0):
    T = hf.shape[0]
    blk = pl.BlockSpec((tm, BRANCH_W), lambda i: (i, 0))
    args = [hf, hb, big, w_norm]
    out_spec, out_shape, x_specs, x_args, aliases = _branch_placement(
        ys, branch, row0 // tm, T, (tm, BRANCH_W), lambda i: (i, 0))
    return pl.pallas_call(
        functools.partial(_gated_norm_kernel, silu=silu, aliased=ys is not None),
        grid=(T // tm,),
        in_specs=[blk, blk,
                  pl.BlockSpec((tm, BRANCH_W), lambda i: (row0 // tm + i, pre_off // BRANCH_W)),
                  pl.BlockSpec((None, 1, BRANCH_W), lambda i: (layer, 0, 0))] + x_specs,
        out_specs=out_spec,
        out_shape=out_shape,
        input_output_aliases=aliases(len(args)),
        compiler_params=_cparams(("parallel",)),
        name="gated_head_norm",
    )(*args, *x_args)


MOD_ROWS = 16


def _modulation_kernel(c_ref, w_ref, b_ref, o_ref):
    c = c_ref[...]
    x = c * jax.nn.sigmoid(c)
    x_hi = x.astype(BF16)
    x_mid = (x - x_hi.astype(F32)).astype(BF16)
    w = w_ref[...]
    w_hi = w.astype(BF16)
    w_mid = (w - w_hi.astype(F32)).astype(BF16)
    part = _dot(x_hi, w_hi) + _dot(x_hi, w_mid) + _dot(x_mid, w_hi)

    @pl.when(pl.program_id(1) == 0)
    def _():
        o_ref[...] = part + b_ref[...]

    @pl.when(pl.program_id(1) > 0)
    def _():
        o_ref[...] += part


def _modulation_tables(cond, w_ada, b_ada, layer, tn=2048, tk=1024):
    D = cond.shape[1]
    N = w_ada.shape[-1]
    tn, tk = min(tn, N), min(tk, D)
    m = pl.pallas_call(
        _modulation_kernel,
        grid=(N // tn, D // tk),
        in_specs=[pl.BlockSpec((MOD_ROWS, tk), lambda j, k: (0, k)),
                  pl.BlockSpec((None, tk, tn), lambda j, k: (layer, k, j)),
                  pl.BlockSpec((None, 1, tn), lambda j, k: (layer, 0, j))],
        out_specs=pl.BlockSpec((MOD_ROWS, tn), lambda j, k: (0, j)),
        out_shape=jax.ShapeDtypeStruct((MOD_ROWS, N), F32),
        compiler_params=_cparams(("parallel", "arbitrary")),
        name="adaln_modulation",
    )(cond, w_ada, b_ada)
    return m[:N_SEG].reshape(N_SEG, 6, 1, D)


def _mixers(big, small, l, lam_init, shared, caches, ys):
    (w_lam, w_subln, gate_bias, w_b_norm, w_up_pad, b_up, w_c_norm, w_sink, rope_a, rope_d) = shared
    (ca_k, ca_v, sb_c, sb_n, sb_m, sc_s, cd_k, cd_v) = caches
    gate_cols = small[:, :N_B_IF] + gate_bias[l]
    gate_rows = gate_cols.T

    ys = _diff_attention(big, 0, BATCH, SEQ, l, w_lam, w_subln, lam_init, ys=ys, branch=0)
    ys = _diff_attention(big, T_CTX, DEC_BATCH, DEC_SEQ, l, w_lam, w_subln, lam_init, cache=(ca_k, ca_v),
                         rope_tabs=rope_a, ys=ys, branch=0)

    hf_c, hb_c, c_out, n_out, m_out = _mlstm(big, gate_cols, gate_rows, 0, BATCH, SEQ, l, emit_state=True)
    hf_l, hb_l = _mlstm(big, gate_cols, gate_rows, T_CTX, DEC_BATCH, DEC_SEQ, l, state=(sb_c, sb_n, sb_m))
    ys = _gated_norm(hf_c, hb_c, big, OFF['b_o'], 0, w_b_norm, l, silu=False, ys=ys, branch=1)
    ys = _gated_norm(hf_l, hb_l, big, OFF['b_o'], T_CTX, w_b_norm, l, silu=False, ys=ys, branch=1)

    of_c, ob_c, s_out = _gla(big, small, w_up_pad, b_up, 0, BATCH, SEQ, l, emit_state=True)
    of_l, ob_l = _gla(big, small, w_up_pad, b_up, T_CTX, DEC_BATCH, DEC_SEQ, l, state=sc_s)
    ys = _gated_norm(of_c, ob_c, big, OFF['c_r'], 0, w_c_norm, l, silu=True, ys=ys, branch=2)
    ys = _gated_norm(of_l, ob_l, big, OFF['c_r'], T_CTX, w_c_norm, l, silu=True, ys=ys, branch=2)

    ys = _gqa(big, 0, BATCH, SEQ, l, w_sink, tq=SEQ, ys=ys, branch=3)
    ys = _gqa(big, T_CTX, DEC_BATCH, DEC_SEQ, l, w_sink, cache=(cd_k, cd_v), rope_tabs=rope_d, ys=ys, branch=3)

    ctx_rows = big[:T_CTX]
    ctx = (ctx_rows[:, OFF['a_k']:OFF['a_k'] + 1024].reshape(BATCH, SEQ, H_A, 2 * D_A),
           ctx_rows[:, OFF['a_v']:OFF['a_v'] + 1024].reshape(BATCH, SEQ, H_A, 2 * D_A),
           c_out, n_out.reshape(BATCH, 2, H_B, DK_B), m_out.reshape(BATCH, 2, H_B), s_out,
           ctx_rows[:, OFF['d_k']:OFF['d_k'] + 512].reshape(BATCH, SEQ, HKV_D, DH_D),
           ctx_rows[:, OFF['d_v']:OFF['d_v'] + 512].reshape(BATCH, SEQ, HKV_D, DH_D))
    return ys, ctx


def kernel(x_prompt, x_sample, c, cache_a_k, cache_a_v, state_b_C, state_b_n, state_b_m, state_c_S, cache_d_k, cache_d_v, c_ctx, w_ada, b_ada, w_norm1, w_norm2, w_in, w_a_lambda, w_a_subln, b_b_if, w_b_norm, w_c_alpha_up, b_c_alpha, w_c_norm, w_d_sink, w_branch, w_merge_gate, w_out, w_router_group, b_router_group, w_router_expert, b_router_expert, w_exp_in, w_exp_out, w_norm_f):
    rows = DEC_SEQ // GRID_W
    rope_a = _rope_lane_tables(rows, D_A)
    rope_d = _rope_lane_tables(rows, DH_D)

    w_big = jnp.concatenate([w_in[:, :, :OFF_B_IF], w_in[:, :, OFF_C_Q:OFF_C_GLR], w_in[:, :, OFF_D_Q:]],
                            axis=-1).astype(BF16)
    w_small = jnp.concatenate([w_in[:, :, OFF_B_IF:OFF_C_Q], w_in[:, :, OFF_C_GLR:OFF_D_Q],
                               jnp.zeros((DEPTH, D_MODEL, D_SMALL - N_B_IF - N_C_GLR), F32)], axis=-1).astype(BF16)
    w_gate_bf = w_merge_gate.astype(BF16)
    w_branch_bf = w_branch.astype(BF16)
    w_out_bf = w_out.astype(BF16)
    w_exp_in_bf = w_exp_in.astype(BF16)
    w_exp_out_bf = w_exp_out.astype(BF16)

    w_up_pad = jnp.zeros((DEPTH, 2, D_SMALL, H_C * DK_C), F32)
    for d in range(2):
        lo = N_B_IF + d * GATE_RANK_C
        w_up_pad = w_up_pad.at[:, d, lo:lo + GATE_RANK_C, :].set(w_c_alpha_up[:, d])
    w_up_pad = w_up_pad.astype(BF16)

    shared = (w_a_lambda, w_a_subln.reshape(DEPTH, 1, 2 * D_A), b_b_if.reshape(DEPTH, N_B_IF),
              w_b_norm.reshape(DEPTH, 1, H_B * DV_B), w_up_pad, b_c_alpha.reshape(DEPTH, 2, 1, H_C * DK_C),
              w_c_norm.reshape(DEPTH, 1, H_C * DV_C), w_d_sink, rope_a, rope_d)
    caches = (cache_a_k.reshape(DEC_BATCH, DEPTH, PAST_LEN, H_A * 2 * D_A),
              cache_a_v.reshape(DEC_BATCH, DEPTH, PAST_LEN, H_A * 2 * D_A),
              state_b_C, state_b_n.reshape(DEC_BATCH, DEPTH, 2, H_B, 1, DK_B),
              state_b_m.reshape(DEC_BATCH, DEPTH, 2, H_B, 1, 1), state_c_S,
              cache_d_k.reshape(DEC_BATCH, DEPTH, PAST_LEN, HKV_D * DH_D),
              cache_d_v.reshape(DEC_BATCH, DEPTH, PAST_LEN, HKV_D * DH_D))

    cond = jnp.concatenate([c_ctx[None, :], c, jnp.zeros((MOD_ROWS - N_SEG, D_MODEL), F32)], axis=0)
    b_ada3 = b_ada.reshape(DEPTH, 1, 6 * D_MODEL)
    x = jnp.concatenate([x_prompt.reshape(T_CTX, D_MODEL), x_sample.reshape(T_LAT, D_MODEL)], axis=0)
    ys = jnp.zeros((N_BRANCH, T_ALL, BRANCH_W), BF16)
    ctx_out = [[] for _ in range(8)]
    for l in range(DEPTH):
        lw = {'w_exp_in': w_exp_in_bf, 'w_exp_out': w_exp_out_bf}
        w_router3, b_router = _router_operands(w_router_group[l], b_router_group[l],
                                               w_router_expert[l], b_router_expert[l])
        lam_init = 0.8 - 0.6 * math.exp(-0.3 * l)
        mods = _modulation_tables(cond, w_ada, b_ada3, l)
        sh1, sc1, g1, sh2, sc2, g2 = [mods[:, i] for i in range(6)]

        u = _norm_mod(x, w_norm1[l][None, :], sh1, sc1)
        big = _mm(u, w_big, l, F32)
        small = _mm(u, w_small, l, F32, tn=D_SMALL)

        ys, ctx = _mixers(big, small, l, lam_init, shared, caches, ys)
        for lst, t in zip(ctx_out, ctx):
            lst.append(t)

        merged = _merge(u, ys, w_gate_bf, w_branch_bf, l)
        x = _mm_residual(merged, w_out_bf, l, x, g1)

        u2, logits = _norm_mod_route(x, w_norm2[l][None, :], sh2, sc2, w_router3, b_router)
        ya, yb = _hier_moe(u2, logits, lw, l)
        x = _residual_pair(x, g2, ya, yb)

    y = _norm(x, w_norm_f[None, :])
    y_prompt = y[:T_CTX].reshape(BATCH, SEQ, D_MODEL)
    y_sample = y[T_CTX:].reshape(DEC_BATCH, DEC_SEQ, D_MODEL)
    outs = [jnp.stack(t, axis=1) for t in ctx_out]
    return (y_prompt, y_sample) + tuple(outs)
```
